```python
import math
import jax, jax.numpy as jnp
from jax import lax
import numpy as np


D_MODEL = 1024
BATCH = 1
SEQ = 16384
DEPTH = 2
DEC_BATCH = 8
DEC_SEQ = 8192
PAST_LEN = 128

HEAD_DIM = 64
D_FF = 2816
RMS_EPS = 1e-6
ROPE_THETA = 10000.0
NEG_INF = -1e30
DILATED_GROUPS = ((128, 1), (512, 4), (2048, 16))
N_GROUPS_A = 3
HEADS_PER_GROUP_A = 8
QKV_A = 3 * N_GROUPS_A * HEADS_PER_GROUP_A * HEAD_DIM
OUT_A = HEADS_PER_GROUP_A * HEAD_DIM
Q_BLOCK = 128
HEADS_B = 16
QKV_B = 3 * HEADS_B * HEAD_DIM
OUT_B = HEADS_B * HEAD_DIM
GRID_W = 64
NA_ROWS_MAX = 8
NA_COLS = 16
N_LAYERS_A = (DEPTH + 1) // 2
N_LAYERS_B = DEPTH // 2

kernel_name = "hybrid_dilated_neighbourhood_encoder"


def rms_norm(x, g):
    xf = x.astype(jnp.float32)
    y = xf * lax.rsqrt(jnp.mean(xf * xf, axis=-1, keepdims=True) + RMS_EPS)
    return (y * g.astype(jnp.float32)).astype(x.dtype)


def swiglu_ffn(x, g, w_in, w_out):
    h = rms_norm(x, g) @ w_in
    gate, up = jnp.split(h, 2, axis=-1)
    return (jax.nn.silu(gate) * up) @ w_out


def rope(x):
    S = x.shape[1]
    half = HEAD_DIM // 2
    inv_freq = ROPE_THETA ** (-jnp.arange(half, dtype=jnp.float32) / half)
    ang = jnp.arange(S, dtype=jnp.float32)[:, None] * inv_freq[None, :]
    bshape = (1, S) + (1,) * (x.ndim - 3) + (half,)
    cos = jnp.cos(ang).reshape(bshape)
    sin = jnp.sin(ang).reshape(bshape)
    xf = x.astype(jnp.float32)
    x1, x2 = xf[..., :half], xf[..., half:]
    return jnp.concatenate([x1 * cos - x2 * sin, x2 * cos + x1 * sin], axis=-1).astype(x.dtype)


def dilated_window_attention(q, k, v, dilation, side):
    B, S, H, Dh = q.shape
    L = S // dilation
    qb = min(Q_BLOCK, L)
    nb = -(-L // qb)
    lq = nb * qb
    kb_len = qb + 2 * side

    def by_stride(a):
        return a.reshape(B, L, dilation, H, Dh).transpose(0, 2, 1, 3, 4)

    qs = jnp.pad(by_stride(q), ((0, 0), (0, 0), (0, lq - L), (0, 0), (0, 0)))
    qs = qs.reshape(B, dilation, nb, qb, H, Dh)
    pad_k = ((0, 0), (0, 0), (side, side + lq - L), (0, 0), (0, 0))
    ks = jnp.pad(by_stride(k), pad_k)
    vs = jnp.pad(by_stride(v), pad_k)
    idx = jnp.arange(nb)[:, None] * qb + jnp.arange(kb_len)[None, :]
    kb = ks[:, :, idx]
    vb = vs[:, :, idx]
    s = jnp.einsum('bdnqhe,bdnkhe->bdnhqk', qs, kb).astype(jnp.float32)
    qpos = jnp.arange(nb)[:, None, None] * qb + jnp.arange(qb)[None, :, None]
    kpos = jnp.arange(nb)[:, None, None] * qb + jnp.arange(kb_len)[None, None, :] - side
    valid = (kpos >= 0) & (kpos < L) & (jnp.abs(qpos - kpos) <= side)
    s = jnp.where(valid[None, None, :, None], s, NEG_INF)
    m = jnp.max(s, axis=-1, keepdims=True)
    p = jnp.exp(s - m)
    den = jnp.sum(p, axis=-1)
    o = jnp.einsum('bdnhqk,bdnkhe->bdnqhe', p, vb.astype(jnp.float32))
    o = o / jnp.swapaxes(den, 3, 4)[..., None]
    lse = jnp.swapaxes(m[..., 0] + jnp.log(den), 3, 4)

    def back(a):
        a = a.reshape((B, dilation, lq) + a.shape[4:])[:, :, :L]
        a = jnp.swapaxes(a, 1, 2)
        return a.reshape((B, S) + a.shape[3:])

    return back(o), back(lse)


def mixer_a(h, w_qkv, q_gain, k_gain, w_o):
    B, S, _ = h.shape
    scale = HEAD_DIM ** -0.5
    qkv = (h @ w_qkv).reshape(B, S, 3, N_GROUPS_A, HEADS_PER_GROUP_A, HEAD_DIM)
    q = rope(rms_norm(qkv[:, :, 0], q_gain)) * scale
    k = rope(rms_norm(qkv[:, :, 1], k_gain))
    v = qkv[:, :, 2]
    outs, lses = [], []
    for g, (window, dil) in enumerate(DILATED_GROUPS):
        o_g, lse_g = dilated_window_attention(q[:, :, g], k[:, :, g], v[:, :, g], dil, window // (2 * dil))
        outs.append(o_g)
        lses.append(lse_g)
    weights = jax.nn.softmax(jnp.stack(lses), axis=0)
    o = jnp.sum(weights[..., None] * jnp.stack(outs), axis=0)
    return o.astype(h.dtype).reshape(B, S, OUT_A) @ w_o


def mixer_b(h, w_qkv, q_gain, k_gain, rpb, w_o):
    B, S, _ = h.shape
    rows = S // GRID_W
    kh = min(NA_ROWS_MAX, rows)
    scale = HEAD_DIM ** -0.5
    qkv = (h @ w_qkv).reshape(B, S, 3, HEADS_B, HEAD_DIM)
    qg = (rms_norm(qkv[:, :, 0], q_gain) * scale).reshape(B, rows, GRID_W, HEADS_B, HEAD_DIM)
    kg = rms_norm(qkv[:, :, 1], k_gain).reshape(B, rows, GRID_W, HEADS_B, HEAD_DIM)
    vg = qkv[:, :, 2].reshape(B, rows, GRID_W, HEADS_B, HEAD_DIM)
    row_start = jnp.clip(jnp.arange(rows) - kh // 2, 0, rows - kh)
    col = jnp.arange(GRID_W)
    col_start = jnp.clip(col - NA_COLS // 2, 0, GRID_W - NA_COLS)
    col_mask = (col[None, :] >= col_start[:, None]) & (col[None, :] < col_start[:, None] + NA_COLS)
    dc = jnp.clip(col[None, :] - col[:, None] + NA_COLS - 1, 0, 2 * NA_COLS - 2)

    def row_block(r):
        rs = row_start[r]
        q_r = lax.dynamic_index_in_dim(qg, r, axis=1, keepdims=False)
        k_r = lax.dynamic_slice_in_dim(kg, rs, kh, axis=1)
        v_r = lax.dynamic_slice_in_dim(vg, rs, kh, axis=1)
        s = jnp.einsum('bqhe,bjkhe->bhqjk', q_r, k_r).astype(jnp.float32)
        dr = rs + jnp.arange(kh) - r + NA_ROWS_MAX - 1
        bias = rpb[:, dr[None, :, None], dc[:, None, :]].astype(jnp.float32)
        s = jnp.where(col_mask[:, None, :], s + bias[None], NEG_INF)
        p = jax.nn.softmax(s.reshape(B, HEADS_B, GRID_W, kh * GRID_W), axis=-1).reshape(s.shape)
        return jnp.einsum('bhqjk,bjkhe->bqhe', p, v_r.astype(jnp.float32))

    o = lax.map(row_block, jnp.arange(rows))
    o = o.transpose(1, 0, 2, 3, 4).reshape(B, S, OUT_B).astype(h.dtype)
    return o @ w_o


def setup_inputs(seed: int = 0) -> dict:
    key = jax.random.key(seed)
    ks = jax.random.split(key, 20)
    f32 = jnp.float32

    def nrm(k, shape, fan_in):
        return jax.random.normal(k, shape, f32) * (fan_in ** -0.5)

    def gain(k, shape):
        return 1.0 + 0.02 * jax.random.normal(k, shape, f32)

    return {
        "x_prompt": jax.random.normal(ks[0], (BATCH, SEQ, D_MODEL), f32),
        "x_sample": jax.random.normal(ks[1], (DEC_BATCH, DEC_SEQ, D_MODEL), f32),
        "ffn1_norm": gain(ks[2], (DEPTH, D_MODEL)),
        "ffn1_w_in": nrm(ks[3], (DEPTH, D_MODEL, 2 * D_FF), D_MODEL),
        "ffn1_w_out": nrm(ks[4], (DEPTH, D_FF, D_MODEL), D_FF),
        "mix_norm": gain(ks[5], (DEPTH, D_MODEL)),
        "ffn2_norm": gain(ks[6], (DEPTH, D_MODEL)),
        "ffn2_w_in": nrm(ks[7], (DEPTH, D_MODEL, 2 * D_FF), D_MODEL),
        "ffn2_w_out": nrm(ks[8], (DEPTH, D_FF, D_MODEL), D_FF),
        "a_w_qkv": nrm(ks[9], (N_LAYERS_A, D_MODEL, QKV_A), D_MODEL),
        "a_q_norm": gain(ks[10], (N_LAYERS_A, HEAD_DIM)),
        "a_k_norm": gain(ks[11], (N_LAYERS_A, HEAD_DIM)),
        "a_w_o": nrm(ks[12], (N_LAYERS_A, OUT_A, D_MODEL), OUT_A),
        "b_w_qkv": nrm(ks[13], (N_LAYERS_B, D_MODEL, QKV_B), D_MODEL),
        "b_q_norm": gain(ks[14], (N_LAYERS_B, HEAD_DIM)),
        "b_k_norm": gain(ks[15], (N_LAYERS_B, HEAD_DIM)),
        "b_rpb": 0.1 * jax.random.normal(ks[16], (N_LAYERS_B, HEADS_B, 2 * NA_ROWS_MAX - 1, 2 * NA_COLS - 1), f32),
        "b_w_o": nrm(ks[17], (N_LAYERS_B, OUT_B, D_MODEL), OUT_B),
    }


def reference(x_prompt, x_sample, ffn1_norm, ffn1_w_in, ffn1_w_out, mix_norm, ffn2_norm, ffn2_w_in, ffn2_w_out,
              a_w_qkv, a_q_norm, a_k_norm, a_w_o, b_w_qkv, b_q_norm, b_k_norm, b_rpb, b_w_o):
    def trunk(x):
        for i in range(DEPTH):
            j = i // 2
            x = x + 0.5 * swiglu_ffn(x, ffn1_norm[i], ffn1_w_in[i], ffn1_w_out[i])
            h = rms_norm(x, mix_norm[i])
            if i % 2 == 0:
                x = x + mixer_a(h, a_w_qkv[j], a_q_norm[j], a_k_norm[j], a_w_o[j])
            else:
                x = x + mixer_b(h, b_w_qkv[j], b_q_norm[j], b_k_norm[j], b_rpb[j], b_w_o[j])
            x = x + 0.5 * swiglu_ffn(x, ffn2_norm[i], ffn2_w_in[i], ffn2_w_out[i])
        return x

    y_prompt = trunk(x_prompt)
    y_sample = trunk(x_sample)
    return (y_prompt, y_sample)
```

```python
import functools

import jax
import jax.numpy as jnp
from jax import lax
from jax.experimental import pallas as pl
from jax.experimental.pallas import tpu as pltpu

D_MODEL = 1024
HEAD_DIM = 64
D_FF = 2816
RMS_EPS = 1e-6
ROPE_THETA = 10000.0
NEG_INF = -1e30
DILATED_GROUPS = ((128, 1), (512, 4), (2048, 16))
HEADS_A = 8
N_GROUPS_A = 3
QKV_A = 3 * N_GROUPS_A * HEADS_A * HEAD_DIM
OUT_A = HEADS_A * HEAD_DIM
HEADS_B = 16
QKV_B = 3 * HEADS_B * HEAD_DIM
OUT_B = HEADS_B * HEAD_DIM
GRID_W = 64
NA_ROWS = 8
NA_COLS = 16

LANES = 128
VMEM_LIMIT_BYTES = 56 * 1024 * 1024

TOKEN_TILE = 512
FF_CHUNK = 1408
QKV_CHUNK = 512
ATT_A_TILE = 512
ATT_A_QB = 128
ATT_A_SIDE = 64
ATT_B_ROWS = 8
ATT_B_HALO = 4

BF16 = jnp.bfloat16
F32 = jnp.float32


def _params(*semantics):
    return pltpu.CompilerParams(dimension_semantics=semantics,
                                vmem_limit_bytes=VMEM_LIMIT_BYTES)


def _rms_rows(x, g):
    ms = jnp.mean(x * x, axis=-1, keepdims=True)
    return (x * lax.rsqrt(ms + RMS_EPS)) * g


def _ffn_kernel(x_ref, g_ref, wg_ref, wu_ref, wo_ref, o_ref, hn_ref, acc_ref):
    j = pl.program_id(1)

    @pl.when(j == 0)
    def _():
        hn_ref[...] = _rms_rows(x_ref[...], g_ref[...]).astype(BF16)

    h = hn_ref[...]
    gate = jnp.dot(h, wg_ref[...], preferred_element_type=F32)
    up = jnp.dot(h, wu_ref[...], preferred_element_type=F32)
    a = (gate * jax.nn.sigmoid(gate) * up).astype(BF16)
    part = jnp.dot(a, wo_ref[...], preferred_element_type=F32)

    @pl.when(j == 0)
    def _():
        acc_ref[...] = part

    @pl.when(j > 0)
    def _():
        acc_ref[...] += part

    @pl.when(j == pl.num_programs(1) - 1)
    def _():
        o_ref[...] = x_ref[...] + 0.5 * acc_ref[...]


def _ffn(x, g, w_in, w_out):
    t = x.shape[0]
    nf = D_FF // FF_CHUNK
    return pl.pallas_call(
        _ffn_kernel,
        grid=(t // TOKEN_TILE, nf),
        in_specs=[
            pl.BlockSpec((TOKEN_TILE, D_MODEL), lambda i, j: (i, 0)),
            pl.BlockSpec((1, D_MODEL), lambda i, j: (0, 0)),
            pl.BlockSpec((D_MODEL, FF_CHUNK), lambda i, j: (0, j)),
            pl.BlockSpec((D_MODEL, FF_CHUNK), lambda i, j: (0, j + nf)),
            pl.BlockSpec((FF_CHUNK, D_MODEL), lambda i, j: (j, 0)),
        ],
        out_specs=pl.BlockSpec((TOKEN_TILE, D_MODEL), lambda i, j: (i, 0)),
        out_shape=jax.ShapeDtypeStruct((t, D_MODEL), F32),
        scratch_shapes=[pltpu.VMEM((TOKEN_TILE, D_MODEL), BF16),
                        pltpu.VMEM((TOKEN_TILE, D_MODEL), F32)],
        compiler_params=_params("parallel", "arbitrary"),
        name="ffn",
    )(x, g.reshape(1, D_MODEL), w_in, w_in, w_out)


def _qkv_kernel(*refs, n_norm_chunks, use_rope):
    if use_rope:
        x_ref, g_ref, w_ref, gain_ref, seg_ref, cos_ref, sin_ref, o_ref, hn_ref = refs
    else:
        x_ref, g_ref, w_ref, gain_ref, seg_ref, o_ref, hn_ref = refs
    j = pl.program_id(1)

    @pl.when(j == 0)
    def _():
        hn_ref[...] = _rms_rows(x_ref[...], g_ref[...]).astype(BF16)

    y = jnp.dot(hn_ref[...], w_ref[...], preferred_element_type=F32)

    @pl.when(j < n_norm_chunks)
    def _():
        ms = jnp.dot((y * y).astype(BF16), seg_ref[...], preferred_element_type=F32)
        yn = (y * lax.rsqrt(ms + RMS_EPS)) * gain_ref[...]
        if use_rope:
            cos = cos_ref[...]
            sin = sin_ref[...]
            lane = lax.broadcasted_iota(jnp.int32, (TOKEN_TILE, LANES), 1)
            first_half = (lane % HEAD_DIM) < (HEAD_DIM // 2)
            for c in range(QKV_CHUNK // LANES):
                yc = yn[:, c * LANES:(c + 1) * LANES]
                partner = jnp.where(first_half,
                                    pltpu.roll(yc, LANES - HEAD_DIM // 2, axis=1),
                                    pltpu.roll(yc, HEAD_DIM // 2, axis=1))
                o_ref[:, c * LANES:(c + 1) * LANES] = (yc * cos + partner * sin).astype(BF16)
        else:
            o_ref[...] = yn.astype(BF16)

    @pl.when(j >= n_norm_chunks)
    def _():
        o_ref[...] = y.astype(BF16)


def _qkv(x, g, w, gains, seg, n_norm_chunks, rope_tables, seq_len):
    t = x.shape[0]
    n = w.shape[1]
    use_rope = rope_tables is not None
    tiles_per_seq = seq_len // TOKEN_TILE
    in_specs = [
        pl.BlockSpec((TOKEN_TILE, D_MODEL), lambda i, j: (i, 0)),
        pl.BlockSpec((1, D_MODEL), lambda i, j: (0, 0)),
        pl.BlockSpec((D_MODEL, QKV_CHUNK), lambda i, j: (0, j)),
        pl.BlockSpec((1, QKV_CHUNK), lambda i, j: (0, j)),
        pl.BlockSpec((QKV_CHUNK, QKV_CHUNK), lambda i, j: (0, 0)),
    ]
    args = [x, g.reshape(1, D_MODEL), w, gains, seg]
    if use_rope:
        in_specs += [pl.BlockSpec((TOKEN_TILE, LANES), lambda i, j: (i % tiles_per_seq, 0))] * 2
        args += list(rope_tables)
    return pl.pallas_call(
        functools.partial(_qkv_kernel, n_norm_chunks=n_norm_chunks, use_rope=use_rope),
        grid=(t // TOKEN_TILE, n // QKV_CHUNK),
        in_specs=in_specs,
        out_specs=pl.BlockSpec((TOKEN_TILE, QKV_CHUNK), lambda i, j: (i, j)),
        out_shape=jax.ShapeDtypeStruct((t, n), BF16),
        scratch_shapes=[pltpu.VMEM((TOKEN_TILE, D_MODEL), BF16)],
        compiler_params=_params("parallel", "arbitrary"),
        name="qkv_rope" if use_rope else "qkv",
    )(*args)


def _attn_a_kernel(q_ref, kp_ref, kc_ref, kn_ref, vp_ref, vc_ref, vn_ref,
                   o_ref, lse_ref, kw_ref, vw_ref, *, seq_len):
    side, qb, tq = ATT_A_SIDE, ATT_A_QB, ATT_A_TILE
    kw_ref[0:side] = kp_ref[0]
    kw_ref[side:side + tq] = kc_ref[0]
    kw_ref[side + tq:] = kn_ref[0]
    vw_ref[0:side] = vp_ref[0]
    vw_ref[side:side + tq] = vc_ref[0]
    vw_ref[side + tq:] = vn_ref[0]
    q_start = pl.program_id(2) * tq
    kb = qb + 2 * side
    row = lax.broadcasted_iota(jnp.int32, (qb, kb), 0)
    col = lax.broadcasted_iota(jnp.int32, (qb, kb), 1)
    band = (col >= row) & (col - row <= 2 * side)

    def block(b, carry):
        r0 = pl.multiple_of(b * qb, qb)
        kpos = q_start + r0 + col - side
        valid = band & (kpos >= 0) & (kpos < seq_len)
        bias = jnp.where(valid, 0.0, NEG_INF).astype(F32)
        for h in range(HEADS_A):
            hs = slice(h * HEAD_DIM, (h + 1) * HEAD_DIM)
            q = q_ref[0, pl.ds(r0, qb), hs]
            k = kw_ref[pl.ds(r0, kb), hs]
            v = vw_ref[pl.ds(r0, kb), hs]
            s = lax.dot_general(q, k, (((1,), (1,)), ((), ())),
                                preferred_element_type=F32) + bias
            m = jnp.max(s, axis=-1, keepdims=True)
            p = jnp.exp(s - m)
            den = jnp.sum(p, axis=-1, keepdims=True)
            o = jnp.dot(p.astype(BF16), v, preferred_element_type=F32) / den
            o_ref[0, pl.ds(r0, qb), hs] = o.astype(BF16)
            lse_ref[0, pl.ds(r0, qb), hs] = jnp.broadcast_to(m + jnp.log(den), (qb, HEAD_DIM))
        return carry

    lax.fori_loop(0, tq // qb, block, 0)


def _attn_a_group(qkv, group, dilation, batch, seq_len):
    length = seq_len // dilation
    tq, side = ATT_A_TILE, ATT_A_SIDE
    n_tiles = length // tq
    halo_blocks = length // side
    cpt = QKV_A // OUT_A
    view = qkv.reshape(batch, length, dilation * QKV_A)

    def cur(kind):
        return pl.BlockSpec((1, tq, OUT_A),
                            lambda b, r, i: (b, i, r * cpt + kind * N_GROUPS_A + group))

    def prev(kind):
        return pl.BlockSpec((1, side, OUT_A),
                            lambda b, r, i: (b, jnp.maximum(i * (tq // side) - 1, 0),
                                             r * cpt + kind * N_GROUPS_A + group))

    def nxt(kind):
        return pl.BlockSpec((1, side, OUT_A),
                            lambda b, r, i: (b, jnp.minimum((i + 1) * (tq // side), halo_blocks - 1),
                                             r * cpt + kind * N_GROUPS_A + group))

    out_spec = pl.BlockSpec((1, tq, OUT_A), lambda b, r, i: (b, i, r))
    o, lse = pl.pallas_call(
        functools.partial(_attn_a_kernel, seq_len=length),
        grid=(batch, dilation, n_tiles),
        in_specs=[cur(0), prev(1), cur(1), nxt(1), prev(2), cur(2), nxt(2)],
        out_specs=[out_spec, out_spec],
        out_shape=[jax.ShapeDtypeStruct((batch, length, dilation * OUT_A), BF16),
                   jax.ShapeDtypeStruct((batch, length, dilation * OUT_A), F32)],
        scratch_shapes=[pltpu.VMEM((tq + 2 * side, OUT_A), BF16),
                        pltpu.VMEM((tq + 2 * side, OUT_A), BF16)],
        compiler_params=_params("parallel", "parallel", "arbitrary"),
        name=f"attn_a_d{dilation}",
    )(view, view, view, view, view, view, view)
    return o.reshape(batch * seq_len, OUT_A), lse.reshape(batch * seq_len, OUT_A)


def _merge_proj_a_kernel(x_ref, o0_ref, o1_ref, o2_ref, l0_ref, l1_ref, l2_ref, w_ref, y_ref):
    l0, l1, l2 = l0_ref[...], l1_ref[...], l2_ref[...]
    m = jnp.maximum(jnp.maximum(l0, l1), l2)
    e0, e1, e2 = jnp.exp(l0 - m), jnp.exp(l1 - m), jnp.exp(l2 - m)
    inv = 1.0 / (e0 + e1 + e2)
    o = ((e0 * inv) * o0_ref[...].astype(F32) + (e1 * inv) * o1_ref[...].astype(F32)
         + (e2 * inv) * o2_ref[...].astype(F32))
    y_ref[...] = x_ref[...] + jnp.dot(o.astype(BF16), w_ref[...], preferred_element_type=F32)


def _merge_proj_a(x, outs, lses, w_o):
    t = x.shape[0]
    row = lambda width: pl.BlockSpec((TOKEN_TILE, width), lambda i: (i, 0))
    return pl.pallas_call(
        _merge_proj_a_kernel,
        grid=(t // TOKEN_TILE,),
        in_specs=[row(D_MODEL)] + [row(OUT_A)] * 6
                 + [pl.BlockSpec((OUT_A, D_MODEL), lambda i: (0, 0))],
        out_specs=row(D_MODEL),
        out_shape=jax.ShapeDtypeStruct((t, D_MODEL), F32),
        compiler_params=_params("parallel"),
        name="merge_proj_a",
    )(x, *outs, *lses, w_o)


def _attn_b_kernel(q_ref, kp_ref, kc_ref, kn_ref, vp_ref, vc_ref, vn_ref, bias_ref,
                   o_ref, kw_ref, vw_ref, *, n_rows):
    w, halo, rows = GRID_W, ATT_B_HALO, ATT_B_ROWS
    kw_ref[0:halo * w] = kp_ref[0]
    kw_ref[halo * w:(halo + rows) * w] = kc_ref[0]
    kw_ref[(halo + rows) * w:] = kn_ref[0]
    vw_ref[0:halo * w] = vp_ref[0]
    vw_ref[halo * w:(halo + rows) * w] = vc_ref[0]
    vw_ref[(halo + rows) * w:] = vn_ref[0]
    first_row = pl.program_id(1) * rows
    n_keys = NA_ROWS * w

    def one_row(rho, carry):
        r = first_row + rho
        rs = jnp.clip(r - NA_ROWS // 2, 0, n_rows - NA_ROWS)
        win = pl.multiple_of((rs - first_row + halo) * w, w)
        dr0 = rs - r + NA_ROWS - 1
        q0 = pl.multiple_of(rho * w, w)
        for h in range(HEADS_B):
            hs = slice(h * HEAD_DIM, (h + 1) * HEAD_DIM)
            q = q_ref[0, pl.ds(q0, w), hs]
            k = kw_ref[pl.ds(win, n_keys), hs]
            v = vw_ref[pl.ds(win, n_keys), hs]
            s = lax.dot_general(q, k, (((1,), (1,)), ((), ())), preferred_element_type=F32)
            s = jnp.concatenate(
                [s[:, jj * LANES:(jj + 1) * LANES] + bias_ref[h, dr0 + 2 * jj]
                 for jj in range(n_keys // LANES)], axis=-1)
            m = jnp.max(s, axis=-1, keepdims=True)
            p = jnp.exp(s - m)
            den = jnp.sum(p, axis=-1, keepdims=True)
            o = jnp.dot(p.astype(BF16), v, preferred_element_type=F32) / den
            o_ref[0, pl.ds(q0, w), hs] = o.astype(BF16)
        return carry

    lax.fori_loop(0, rows, one_row, 0)


def _attn_b(qkv, bias, batch, seq_len):
    n_rows = seq_len // GRID_W
    tile = ATT_B_ROWS * GRID_W
    halo = ATT_B_HALO * GRID_W
    per_tile = tile // halo
    halo_blocks = seq_len // halo

    def cur(kind):
        return pl.BlockSpec((1, tile, OUT_B), lambda b, i: (b, i, kind))

    def prev(kind):
        return pl.BlockSpec((1, halo, OUT_B),
                            lambda b, i: (b, jnp.maximum(i * per_tile - 1, 0), kind))

    def nxt(kind):
        return pl.BlockSpec((1, halo, OUT_B),
                            lambda b, i: (b, jnp.minimum((i + 1) * per_tile, halo_blocks - 1), kind))

    return pl.pallas_call(
        functools.partial(_attn_b_kernel, n_rows=n_rows),
        grid=(batch, n_rows // ATT_B_ROWS),
        in_specs=[cur(0), prev(1), cur(1), nxt(1), prev(2), cur(2), nxt(2),
                  pl.BlockSpec(bias.shape, lambda b, i: (0, 0, 0, 0))],
        out_specs=pl.BlockSpec((1, tile, OUT_B), lambda b, i: (b, i, 0)),
        out_shape=jax.ShapeDtypeStruct((batch, seq_len, OUT_B), BF16),
        scratch_shapes=[pltpu.VMEM((tile + 2 * halo, OUT_B), BF16),
                        pltpu.VMEM((tile + 2 * halo, OUT_B), BF16)],
        compiler_params=_params("parallel", "arbitrary"),
        name="attn_b",
    )(qkv, qkv, qkv, qkv, qkv, qkv, qkv, bias)


def _proj_b_kernel(x_ref, o_ref, w_ref, y_ref):
    y_ref[...] = x_ref[...] + jnp.dot(o_ref[...], w_ref[...], preferred_element_type=F32)


def _proj_b(x, o, w_o):
    t = x.shape[0]
    return pl.pallas_call(
        _proj_b_kernel,
        grid=(t // TOKEN_TILE,),
        in_specs=[pl.BlockSpec((TOKEN_TILE, D_MODEL), lambda i: (i, 0)),
                  pl.BlockSpec((TOKEN_TILE, OUT_B), lambda i: (i, 0)),
                  pl.BlockSpec((OUT_B, D_MODEL), lambda i: (0, 0))],
        out_specs=pl.BlockSpec((TOKEN_TILE, D_MODEL), lambda i: (i, 0)),
        out_shape=jax.ShapeDtypeStruct((t, D_MODEL), F32),
        compiler_params=_params("parallel"),
        name="proj_b",
    )(x, o, w_o)


def _segment_mean_matrix():
    head = jnp.arange(QKV_CHUNK) // HEAD_DIM
    return jnp.where(head[:, None] == head[None, :], 1.0 / HEAD_DIM, 0.0).astype(BF16)


def _rope_tables(seq_len):
    half = HEAD_DIM // 2
    inv_freq = ROPE_THETA ** (-jnp.arange(half, dtype=F32) / half)
    ang = jnp.arange(seq_len, dtype=F32)[:, None] * inv_freq[None, :]
    cos, sin = jnp.cos(ang), jnp.sin(ang)
    reps = LANES // HEAD_DIM
    return (jnp.tile(jnp.concatenate([cos, cos], axis=-1), (1, reps)),
            jnp.tile(jnp.concatenate([-sin, sin], axis=-1), (1, reps)))


def _chunk_gains(q_gain, k_gain, heads, total_cols):
    q = jnp.tile(q_gain * (HEAD_DIM ** -0.5), heads)
    k = jnp.tile(k_gain, heads)
    rest = jnp.ones((total_cols - 2 * heads * HEAD_DIM,), F32)
    return jnp.concatenate([q, k, rest]).reshape(1, total_cols)


def _bias_table_b(rpb):
    col = jnp.arange(GRID_W)
    col_start = jnp.clip(col - NA_COLS // 2, 0, GRID_W - NA_COLS)
    col_mask = (col[None, :] >= col_start[:, None]) & (col[None, :] < col_start[:, None] + NA_COLS)
    dc = jnp.clip(col[None, :] - col[:, None] + NA_COLS - 1, 0, 2 * NA_COLS - 2)
    t = jnp.where(col_mask[None, None], rpb[:, :, dc], NEG_INF)
    return jnp.concatenate([t[:, :-1], t[:, 1:]], axis=-1).astype(F32)


def _mixer_a(x, batch, seq_len, mix_g, w_qkv, q_gain, k_gain, w_o, seg, rope_tables):
    gains = _chunk_gains(q_gain, k_gain, N_GROUPS_A * HEADS_A, QKV_A)
    qkv = _qkv(x, mix_g, w_qkv, gains, seg, 2 * N_GROUPS_A, rope_tables, seq_len)
    outs, lses = [], []
    for group, (window, dilation) in enumerate(DILATED_GROUPS):
        assert window // (2 * dilation) == ATT_A_SIDE
        o, lse = _attn_a_group(qkv, group, dilation, batch, seq_len)
        outs.append(o)
        lses.append(lse)
    return _merge_proj_a(x, outs, lses, w_o)


def _mixer_b(x, batch, seq_len, mix_g, w_qkv, q_gain, k_gain, rpb, w_o, seg):
    gains = _chunk_gains(q_gain, k_gain, HEADS_B, QKV_B)
    qkv = _qkv(x, mix_g, w_qkv, gains, seg, 2 * OUT_B // QKV_CHUNK, None, seq_len)
    o = _attn_b(qkv.reshape(batch, seq_len, QKV_B), _bias_table_b(rpb), batch, seq_len)
    return _proj_b(x, o.reshape(batch * seq_len, OUT_B), w_o)


def kernel(x_prompt, x_sample, ffn1_norm, ffn1_w_in, ffn1_w_out, mix_norm, ffn2_norm, ffn2_w_in, ffn2_w_out, a_w_qkv, a_q_norm, a_k_norm, a_w_o, b_w_qkv, b_q_norm, b_k_norm, b_rpb, b_w_o):
    depth = ffn1_norm.shape[0]
    bf = lambda w: w.astype(BF16)
    ffn1_w_in, ffn1_w_out, ffn2_w_in, ffn2_w_out = map(bf, (ffn1_w_in, ffn1_w_out, ffn2_w_in, ffn2_w_out))
    a_w_qkv, a_w_o, b_w_qkv, b_w_o = map(bf, (a_w_qkv, a_w_o, b_w_qkv, b_w_o))
    seg = _segment_mean_matrix()

    def trunk(x3):
        batch, seq_len, _ = x3.shape
        x = x3.reshape(batch * seq_len, D_MODEL)
        rope_tables = _rope_tables(seq_len)
        for i in range(depth):
            j = i // 2
            x = _ffn(x, ffn1_norm[i], ffn1_w_in[i], ffn1_w_out[i])
            if i % 2 == 0:
                x = _mixer_a(x, batch, seq_len, mix_norm[i], a_w_qkv[j], a_q_norm[j], a_k_norm[j],
                             a_w_o[j], seg, rope_tables)
            else:
                x = _mixer_b(x, batch, seq_len, mix_norm[i], b_w_qkv[j], b_q_norm[j], b_k_norm[j],
                             b_rpb[j], b_w_o[j], seg)
            x = _ffn(x, ffn2_norm[i], ffn2_w_in[i], ffn2_w_out[i])
        return x.reshape(batch, seq_len, D_MODEL)

    return (trunk(x_prompt), trunk(x_sample))
```

```python
import functools

import jax
import jax.numpy as jnp
import numpy as np
from jax import lax
from jax.experimental import pallas as pl
from jax.experimental.pallas import tpu as pltpu

D_MODEL = 1024
HEAD_DIM = 64
D_FF = 2816
RMS_EPS = 1e-6
ROPE_THETA = 10000.0
NEG_INF = -1e30
DILATED_GROUPS = ((128, 1), (512, 4), (2048, 16))
HEADS_A = 8
N_GROUPS_A = 3
QKV_A = 3 * N_GROUPS_A * HEADS_A * HEAD_DIM
OUT_A = HEADS_A * HEAD_DIM
HEADS_B = 16
QKV_B = 3 * HEADS_B * HEAD_DIM
OUT_B = HEADS_B * HEAD_DIM
GRID_W = 64
NA_ROWS = 8
NA_COLS = 16

LANES = 128
VMEM_LIMIT_BYTES = 56 * 1024 * 1024

TOKEN_TILE = 512
FF_CHUNK = 1408
QKV_CHUNK = 512
PAIRS_PER_CHUNK = QKV_CHUNK // LANES
ATT_A_TILE = 512
ATT_A_QB = 128
ATT_A_SIDE = 64
ATT_B_ROWS = 8
ATT_B_HALO = 4
ATT_B_SUB = 2
ATT_B_WIN = NA_ROWS + ATT_B_SUB

BF16 = jnp.bfloat16
F32 = jnp.float32
NT_DIMS = (((1,), (1,)), ((), ()))


def _params(*semantics):
    return pltpu.CompilerParams(dimension_semantics=semantics,
                                vmem_limit_bytes=VMEM_LIMIT_BYTES)


def _rms_rows(x, g):
    ms = jnp.mean(x * x, axis=-1, keepdims=True)
    return (x * lax.rsqrt(ms + RMS_EPS)) * g


def _ffn_kernel(x_ref, g_ref, wg_ref, wu_ref, wo_ref, o_ref, hn_ref, acc_ref):
    j = pl.program_id(1)

    @pl.when(j == 0)
    def _():
        hn_ref[...] = _rms_rows(x_ref[...], g_ref[...]).astype(BF16)

    h = hn_ref[...]
    gate = jnp.dot(h, wg_ref[...], preferred_element_type=F32)
    up = jnp.dot(h, wu_ref[...], preferred_element_type=F32)
    a = (gate * jax.nn.sigmoid(gate) * up).astype(BF16)
    part = jnp.dot(a, wo_ref[...], preferred_element_type=F32)

    @pl.when(j == 0)
    def _():
        acc_ref[...] = part

    @pl.when(j > 0)
    def _():
        acc_ref[...] += part

    @pl.when(j == pl.num_programs(1) - 1)
    def _():
        o_ref[...] = x_ref[...] + 0.5 * acc_ref[...]


def _ffn(x, g, w_in, w_out):
    t = x.shape[0]
    nf = D_FF // FF_CHUNK
    return pl.pallas_call(
        _ffn_kernel,
        grid=(t // TOKEN_TILE, nf),
        in_specs=[
            pl.BlockSpec((TOKEN_TILE, D_MODEL), lambda i, j: (i, 0)),
            pl.BlockSpec((1, D_MODEL), lambda i, j: (0, 0)),
            pl.BlockSpec((D_MODEL, FF_CHUNK), lambda i, j: (0, j)),
            pl.BlockSpec((D_MODEL, FF_CHUNK), lambda i, j: (0, j + nf)),
            pl.BlockSpec((FF_CHUNK, D_MODEL), lambda i, j: (j, 0)),
        ],
        out_specs=pl.BlockSpec((TOKEN_TILE, D_MODEL), lambda i, j: (i, 0)),
        out_shape=jax.ShapeDtypeStruct((t, D_MODEL), F32),
        scratch_shapes=[pltpu.VMEM((TOKEN_TILE, D_MODEL), BF16),
                        pltpu.VMEM((TOKEN_TILE, D_MODEL), F32)],
        compiler_params=_params("parallel", "arbitrary"),
        name="ffn",
    )(x, g.reshape(1, D_MODEL), w_in, w_in, w_out)


def _project_chunk(hn_ref, w_ref, gain_ref, seg_ref, c, normed):
    cols = slice(c * QKV_CHUNK, (c + 1) * QKV_CHUNK)
    y = jnp.dot(hn_ref[...], w_ref[:, cols], preferred_element_type=F32)
    if normed:
        ms = jnp.dot((y * y).astype(BF16), seg_ref[...], preferred_element_type=F32)
        y = (y * lax.rsqrt(ms + RMS_EPS)) * gain_ref[:, cols]
    return y


def _qkv_a_kernel(x_ref, g_ref, w_ref, gain_ref, seg_ref, cos_ref, sin_ref, *rest):
    n_chunks = 3 * N_GROUPS_A
    out_refs, (hn_ref, ys_ref) = rest[:n_chunks], rest[n_chunks:]
    hn_ref[...] = _rms_rows(x_ref[...], g_ref[...]).astype(BF16)
    cos, sin = cos_ref[...], sin_ref[...]
    for c in range(n_chunks):
        kind, group = divmod(c, N_GROUPS_A)
        dilation = DILATED_GROUPS[group][1]
        normed = kind < 2
        y = _project_chunk(hn_ref, w_ref, gain_ref, seg_ref, c, normed)
        if normed:
            y = jnp.concatenate(
                [y[:, p * LANES:(p + 1) * LANES] * cos
                 + pltpu.roll(y[:, p * LANES:(p + 1) * LANES], LANES // 2, axis=1) * sin
                 for p in range(PAIRS_PER_CHUNK)], axis=-1)
        if dilation == 1:
            out_refs[c][0, 0] = y.astype(BF16)
        else:
            slot = kind * (N_GROUPS_A - 1) + group - 1
            rows = TOKEN_TILE // dilation
            for p in range(PAIRS_PER_CHUNK):
                cols = slice(p * LANES, (p + 1) * LANES)
                ys_ref[slot, p] = y[:, cols]
                for r in range(dilation):
                    out_refs[c][0, r, :, cols] = ys_ref[slot, p, pl.ds(r, rows, stride=dilation), :].astype(BF16)


def _qkv_a(x, g, w, gains, seg, rope_tables, batch, seq_len):
    t = x.shape[0]
    tps = seq_len // TOKEN_TILE
    const = lambda i: (0, 0)
    in_specs = [
        pl.BlockSpec((TOKEN_TILE, D_MODEL), lambda i: (i, 0)),
        pl.BlockSpec((1, D_MODEL), const),
        pl.BlockSpec((D_MODEL, QKV_A), const),
        pl.BlockSpec((1, QKV_A), const),
        pl.BlockSpec((QKV_CHUNK, QKV_CHUNK), const),
        pl.BlockSpec((TOKEN_TILE, LANES), lambda i: (i % tps, 0)),
        pl.BlockSpec((TOKEN_TILE, LANES), lambda i: (i % tps, 0)),
    ]
    out_specs, out_shapes = [], []
    for c in range(3 * N_GROUPS_A):
        d = DILATED_GROUPS[c % N_GROUPS_A][1]
        out_specs.append(pl.BlockSpec((1, d, TOKEN_TILE // d, QKV_CHUNK),
                                      lambda i: (i // tps, 0, i % tps, 0)))
        out_shapes.append(jax.ShapeDtypeStruct((batch, d, seq_len // d, QKV_CHUNK), BF16))
    return pl.pallas_call(
        _qkv_a_kernel,
        grid=(t // TOKEN_TILE,),
        in_specs=in_specs,
        out_specs=out_specs,
        out_shape=out_shapes,
        scratch_shapes=[pltpu.VMEM((TOKEN_TILE, D_MODEL), BF16),
                        pltpu.VMEM((3 * (N_GROUPS_A - 1), PAIRS_PER_CHUNK, TOKEN_TILE, LANES), F32)],
        compiler_params=_params("parallel"),
        name="qkv_a",
    )(x, g.reshape(1, D_MODEL), w, gains, seg, *rope_tables)


def _qkv_b_kernel(x_ref, g_ref, w_ref, gain_ref, seg_ref, o_ref, hn_ref):
    hn_ref[...] = _rms_rows(x_ref[...], g_ref[...]).astype(BF16)
    for c in range(QKV_B // QKV_CHUNK):
        y = _project_chunk(hn_ref, w_ref, gain_ref, seg_ref, c, c < 2 * OUT_B // QKV_CHUNK)
        for p in range(PAIRS_PER_CHUNK):
            o_ref[0, c * PAIRS_PER_CHUNK + p] = y[:, p * LANES:(p + 1) * LANES].astype(BF16)


def _qkv_b(x, g, w, gains, seg, batch, seq_len):
    t = x.shape[0]
    tps = seq_len // TOKEN_TILE
    n_pairs = QKV_B // LANES
    const = lambda i: (0, 0)
    return pl.pallas_call(
        _qkv_b_kernel,
        grid=(t // TOKEN_TILE,),
        in_specs=[
            pl.BlockSpec((TOKEN_TILE, D_MODEL), lambda i: (i, 0)),
            pl.BlockSpec((1, D_MODEL), const),
            pl.BlockSpec((D_MODEL, QKV_B), const),
            pl.BlockSpec((1, QKV_B), const),
            pl.BlockSpec((QKV_CHUNK, QKV_CHUNK), const),
        ],
        out_specs=pl.BlockSpec((1, n_pairs, TOKEN_TILE, LANES), lambda i: (i // tps, 0, i % tps, 0)),
        out_shape=jax.ShapeDtypeStruct((batch, n_pairs, seq_len, LANES), BF16),
        scratch_shapes=[pltpu.VMEM((TOKEN_TILE, D_MODEL), BF16)],
        compiler_params=_params("parallel"),
        name="qkv_b",
    )(x, g.reshape(1, D_MODEL), w, gains, seg)


def _attn_a_kernel(q_ref, kp_ref, kc_ref, kn_ref, vp_ref, vc_ref, vn_ref,
                   o_ref, lse_ref, kw_ref, vw_ref, s_ref, p_ref, m_ref, *, seq_len):
    side, qb, tq = ATT_A_SIDE, ATT_A_QB, ATT_A_TILE
    kb = qb + 2 * side
    nb = tq // qb
    kw_ref[0:side] = kp_ref[0, 0]
    kw_ref[side:side + tq] = kc_ref[0, 0]
    kw_ref[side + tq:] = kn_ref[0, 0]
    ones = jnp.ones((tq + 2 * side, LANES), BF16)
    for p in range(PAIRS_PER_CHUNK):
        src = slice(p * LANES, (p + 1) * LANES)
        dst = slice(2 * p * LANES, (2 * p + 1) * LANES)
        vw_ref[0:side, dst] = vp_ref[0, 0, :, src]
        vw_ref[side:side + tq, dst] = vc_ref[0, 0, :, src]
        vw_ref[side + tq:, dst] = vn_ref[0, 0, :, src]
        vw_ref[:, (2 * p + 1) * LANES:(2 * p + 2) * LANES] = ones

    q_start = pl.program_id(2) * tq
    row = lax.broadcasted_iota(jnp.int32, (qb, kb), 0)
    col = lax.broadcasted_iota(jnp.int32, (qb, kb), 1)
    band = (col >= row) & (col - row <= 2 * side)
    biases = []
    for b in range(nb):
        kpos = q_start + b * qb + col - side
        bias = jnp.where(band & (kpos >= 0) & (kpos < seq_len), 0.0, NEG_INF).astype(F32)
        biases.append(jnp.concatenate([bias, bias], axis=0))
    bias4 = jnp.stack(biases)

    lane = lax.broadcasted_iota(jnp.int32, (tq, LANES), 1)
    q_head_a = (lane % HEAD_DIM) < (HEAD_DIM // 2)
    v_head_a = lax.broadcasted_iota(jnp.int32, (qb, LANES), 1) < HEAD_DIM
    zero = jnp.zeros((tq, LANES), BF16)

    for p in range(PAIRS_PER_CHUNK):
        cols = slice(p * LANES, (p + 1) * LANES)
        q2 = q_ref[0, 0, :, cols]
        qa = jnp.where(q_head_a, q2, zero)
        qbb = jnp.where(q_head_a, zero, q2)
        for b in range(nb):
            rows = slice(b * qb, (b + 1) * qb)
            lhs = jnp.concatenate([qa[rows], qbb[rows]], axis=0)
            s_ref[p * nb + b] = lax.dot_general(lhs, kw_ref[b * qb:b * qb + kb, cols], NT_DIMS,
                                                preferred_element_type=F32)
        blk = slice(p * nb, (p + 1) * nb)
        s = s_ref[blk] + bias4
        m = jnp.max(s, axis=-1, keepdims=True)
        p_ref[blk] = jnp.exp(s - m).astype(BF16)
        m_ref[blk] = jnp.broadcast_to(m, (nb, 2 * qb, LANES))
        for b in range(nb):
            rows = slice(b * qb, (b + 1) * qb)
            pv = jnp.dot(p_ref[p * nb + b], vw_ref[b * qb:b * qb + kb, 2 * p * LANES:(2 * p + 2) * LANES],
                         preferred_element_type=F32)
            den = pv[:, LANES:]
            o = pv[:, :LANES] / den
            lse = m_ref[p * nb + b] + jnp.log(den)
            o_ref[0, 0, rows, cols] = jnp.where(v_head_a, o[:qb], o[qb:]).astype(BF16)
            lse_ref[0, 0, rows, cols] = jnp.where(v_head_a, lse[:qb], lse[qb:])


def _attn_a_group(q, k, v, dilation, batch, seq_len):
    length = seq_len // dilation
    tq, side = ATT_A_TILE, ATT_A_SIDE
    n_tiles = length // tq
    halo_blocks = length // side
    nb = tq // ATT_A_QB
    kb = ATT_A_QB + 2 * side
    cur = pl.BlockSpec((1, 1, tq, OUT_A), lambda b, r, i: (b, r, i, 0))
    prev = pl.BlockSpec((1, 1, side, OUT_A),
                        lambda b, r, i: (b, r, jnp.maximum(i * (tq // side) - 1, 0), 0))
    nxt = pl.BlockSpec((1, 1, side, OUT_A),
                       lambda b, r, i: (b, r, jnp.minimum((i + 1) * (tq // side), halo_blocks - 1), 0))
    n_blk = PAIRS_PER_CHUNK * nb
    return pl.pallas_call(
        functools.partial(_attn_a_kernel, seq_len=length),
        grid=(batch, dilation, n_tiles),
        in_specs=[cur, prev, cur, nxt, prev, cur, nxt],
        out_specs=[cur, cur],
        out_shape=[jax.ShapeDtypeStruct((batch, dilation, length, OUT_A), BF16),
                   jax.ShapeDtypeStruct((batch, dilation, length, OUT_A), F32)],
        scratch_shapes=[pltpu.VMEM((tq + 2 * side, OUT_A), BF16),
                        pltpu.VMEM((tq + 2 * side, 2 * OUT_A), BF16),
                        pltpu.VMEM((n_blk, 2 * ATT_A_QB, kb), F32),
                        pltpu.VMEM((n_blk, 2 * ATT_A_QB, kb), BF16),
                        pltpu.VMEM((n_blk, 2 * ATT_A_QB, LANES), F32)],
        compiler_params=_params("parallel", "parallel", "arbitrary"),
        name=f"attn_a_d{dilation}",
    )(q, k, k, k, v, v, v)


def _merge_proj_a_kernel(x_ref, o0_ref, o1_ref, o2_ref, l0_ref, l1_ref, l2_ref, w_ref, y_ref,
                         ot_ref, lt_ref):
    for slot, (o_ref, l_ref, dilation) in enumerate(
            ((o1_ref, l1_ref, DILATED_GROUPS[1][1]), (o2_ref, l2_ref, DILATED_GROUPS[2][1]))):
        rows = TOKEN_TILE // dilation
        for r in range(dilation):
            for p in range(PAIRS_PER_CHUNK):
                cols = slice(p * LANES, (p + 1) * LANES)
                ot_ref[slot, p, pl.ds(r, rows, stride=dilation), :] = o_ref[0, r, :, cols].astype(F32)
                lt_ref[slot, p, pl.ds(r, rows, stride=dilation), :] = l_ref[0, r, :, cols]
    merged = []
    for p in range(PAIRS_PER_CHUNK):
        cols = slice(p * LANES, (p + 1) * LANES)
        l0, l1, l2 = l0_ref[0, 0, :, cols], lt_ref[0, p], lt_ref[1, p]
        m = jnp.maximum(jnp.maximum(l0, l1), l2)
        e0, e1, e2 = jnp.exp(l0 - m), jnp.exp(l1 - m), jnp.exp(l2 - m)
        inv = 1.0 / (e0 + e1 + e2)
        merged.append((e0 * inv) * o0_ref[0, 0, :, cols].astype(F32) + (e1 * inv) * ot_ref[0, p]
                      + (e2 * inv) * ot_ref[1, p])
    o = jnp.concatenate(merged, axis=-1)
    y_ref[...] = x_ref[...] + jnp.dot(o.astype(BF16), w_ref[...], preferred_element_type=F32)


def _merge_proj_a(x, outs, lses, w_o, seq_len):
    t = x.shape[0]
    tps = seq_len // TOKEN_TILE
    row = pl.BlockSpec((TOKEN_TILE, D_MODEL), lambda i: (i, 0))
    grp = [pl.BlockSpec((1, d, TOKEN_TILE // d, OUT_A), lambda i: (i // tps, 0, i % tps, 0))
           for _, d in DILATED_GROUPS]
    return pl.pallas_call(
        _merge_proj_a_kernel,
        grid=(t // TOKEN_TILE,),
        in_specs=[row] + grp + grp + [pl.BlockSpec((OUT_A, D_MODEL), lambda i: (0, 0))],
        out_specs=row,
        out_shape=jax.ShapeDtypeStruct((t, D_MODEL), F32),
        scratch_shapes=[pltpu.VMEM((N_GROUPS_A - 1, PAIRS_PER_CHUNK, TOKEN_TILE, LANES), F32),
                        pltpu.VMEM((N_GROUPS_A - 1, PAIRS_PER_CHUNK, TOKEN_TILE, LANES), F32)],
        compiler_params=_params("parallel"),
        name="merge_proj_a",
    )(x, *outs, *lses, w_o)


def _attn_b_kernel(q_ref, kp_ref, kc_ref, kn_ref, vp_ref, vc_ref, vn_ref, bias_ref,
                   o_ref, kw_ref, vw_ref, s_ref, p_ref):
    w, halo, rows = GRID_W, ATT_B_HALO, ATT_B_ROWS
    n_sub = rows // ATT_B_SUB
    sub_q = ATT_B_SUB * w
    win = ATT_B_WIN * w
    kw_ref[0:halo * w] = kp_ref[0, 0]
    kw_ref[halo * w:(halo + rows) * w] = kc_ref[0, 0]
    kw_ref[(halo + rows) * w:] = kn_ref[0, 0]
    vw_ref[0:halo * w, :LANES] = vp_ref[0, 0]
    vw_ref[halo * w:(halo + rows) * w, :LANES] = vc_ref[0, 0]
    vw_ref[(halo + rows) * w:, :LANES] = vn_ref[0, 0]
    vw_ref[:, LANES:] = jnp.ones(((rows + 2 * halo) * w, LANES), BF16)

    i = pl.program_id(2)
    lo_min = jnp.where(i == 0, halo, 0)
    lo_max = jnp.where(i == pl.num_programs(2) - 1, halo, rows)
    head_a = lax.broadcasted_iota(jnp.int32, (rows * w, LANES), 1) < HEAD_DIM
    q2 = q_ref[0, 0]
    zero = jnp.zeros_like(q2)
    qa = jnp.where(head_a, q2, zero)
    qb = jnp.where(head_a, zero, q2)
    starts = []
    for sb in range(n_sub):
        start = pl.multiple_of(jnp.clip(ATT_B_SUB * sb, lo_min, lo_max) * w, LANES)
        starts.append(start)
        rs = slice(sb * sub_q, (sb + 1) * sub_q)
        lhs = jnp.concatenate([qa[rs], qb[rs]], axis=0)
        s_ref[sb] = lax.dot_general(lhs, kw_ref[pl.ds(start, win), :], NT_DIMS,
                                    preferred_element_type=F32)
    s = s_ref[...] + jnp.concatenate([bias_ref[0, 0], bias_ref[1, 0]], axis=1)
    m = jnp.max(s, axis=-1, keepdims=True)
    p_ref[...] = jnp.exp(s - m).astype(BF16)
    out_a = lax.broadcasted_iota(jnp.int32, (sub_q, LANES), 1) < HEAD_DIM
    for sb in range(n_sub):
        pv = jnp.dot(p_ref[sb], vw_ref[pl.ds(starts[sb], win), :], preferred_element_type=F32)
        o = pv[:, :LANES] / pv[:, LANES:]
        o_ref[0, sb * sub_q:(sb + 1) * sub_q, :] = jnp.where(out_a, o[:sub_q], o[sub_q:]).astype(BF16)


def _attn_b(qkv, bias, batch, seq_len):
    n_rows = seq_len // GRID_W
    tile = ATT_B_ROWS * GRID_W
    halo = ATT_B_HALO * GRID_W
    n_tiles = n_rows // ATT_B_ROWS
    assert n_tiles >= 2
    per_tile = tile // halo
    halo_blocks = seq_len // halo
    n_pairs = HEADS_B // 2
    n_sub = ATT_B_ROWS // ATT_B_SUB
    sub_q = ATT_B_SUB * GRID_W
    win = ATT_B_WIN * GRID_W

    def cur(kind):
        return pl.BlockSpec((1, 1, tile, LANES), lambda hp, b, i: (b, kind * n_pairs + hp, i, 0))

    def prev(kind):
        return pl.BlockSpec((1, 1, halo, LANES),
                            lambda hp, b, i: (b, kind * n_pairs + hp, jnp.maximum(i * per_tile - 1, 0), 0))

    def nxt(kind):
        return pl.BlockSpec((1, 1, halo, LANES),
                            lambda hp, b, i: (b, kind * n_pairs + hp,
                                              jnp.minimum((i + 1) * per_tile, halo_blocks - 1), 0))

    def tile_kind(i):
        return jnp.where(i == 0, 0, jnp.where(i == n_tiles - 1, 2, 1))

    return pl.pallas_call(
        _attn_b_kernel,
        grid=(n_pairs, batch, n_tiles),
        in_specs=[cur(0), prev(1), cur(1), nxt(1), prev(2), cur(2), nxt(2),
                  pl.BlockSpec((2, 1, n_sub, sub_q, win), lambda hp, b, i: (hp, tile_kind(i), 0, 0, 0))],
        out_specs=pl.BlockSpec((1, tile, LANES), lambda hp, b, i: (b, i, hp)),
        out_shape=jax.ShapeDtypeStruct((batch, seq_len, OUT_B), BF16),
        scratch_shapes=[pltpu.VMEM((tile + 2 * halo, LANES), BF16),
                        pltpu.VMEM((tile + 2 * halo, 2 * LANES), BF16),
                        pltpu.VMEM((n_sub, 2 * sub_q, win), F32),
                        pltpu.VMEM((n_sub, 2 * sub_q, win), BF16)],
        compiler_params=_params("arbitrary", "arbitrary", "arbitrary"),
        name="attn_b",
    )(qkv, qkv, qkv, qkv, qkv, qkv, qkv, bias)


def _proj_b_kernel(x_ref, o_ref, w_ref, y_ref):
    y_ref[...] = x_ref[...] + jnp.dot(o_ref[...], w_ref[...], preferred_element_type=F32)


def _proj_b(x, o, w_o):
    t = x.shape[0]
    return pl.pallas_call(
        _proj_b_kernel,
        grid=(t // TOKEN_TILE,),
        in_specs=[pl.BlockSpec((TOKEN_TILE, D_MODEL), lambda i: (i, 0)),
                  pl.BlockSpec((TOKEN_TILE, OUT_B), lambda i: (i, 0)),
                  pl.BlockSpec((OUT_B, D_MODEL), lambda i: (0, 0))],
        out_specs=pl.BlockSpec((TOKEN_TILE, D_MODEL), lambda i: (i, 0)),
        out_shape=jax.ShapeDtypeStruct((t, D_MODEL), F32),
        compiler_params=_params("parallel"),
        name="proj_b",
    )(x, o, w_o)


def _rope_layout():
    n = np.arange(QKV_CHUNK)
    pair, lane = n // LANES, n % LANES
    second, hb, i = lane // HEAD_DIM, (lane % HEAD_DIM) // (HEAD_DIM // 2), lane % (HEAD_DIM // 2)
    head = 2 * pair + hb
    return head * HEAD_DIM + second * (HEAD_DIM // 2) + i, head


def _segment_mean_matrix(head_of_column):
    same = head_of_column[:, None] == head_of_column[None, :]
    return jnp.asarray(np.where(same, 1.0 / HEAD_DIM, 0.0), BF16)


def _rope_tables(seq_len):
    half = HEAD_DIM // 2
    inv_freq = ROPE_THETA ** (-jnp.arange(half, dtype=F32) / half)
    ang = jnp.arange(seq_len, dtype=F32)[:, None] * inv_freq[None, :]
    cos, sin = jnp.cos(ang), jnp.sin(ang)
    return (jnp.concatenate([cos, cos, cos, cos], axis=-1),
            jnp.concatenate([-sin, -sin, sin, sin], axis=-1))


def _prep_a(w_qkv, q_gain, k_gain):
    src, _ = _rope_layout()
    n_qk = 2 * N_GROUPS_A
    cols = np.concatenate([c * QKV_CHUNK + src for c in range(n_qk)]
                          + [np.arange(n_qk * QKV_CHUNK, QKV_A)])
    elem = src % HEAD_DIM
    gains = jnp.concatenate([jnp.tile(q_gain[elem] * (HEAD_DIM ** -0.5), N_GROUPS_A),
                             jnp.tile(k_gain[elem], N_GROUPS_A),
                             jnp.ones((QKV_A - n_qk * QKV_CHUNK,), F32)])
    return w_qkv[:, cols].astype(BF16), gains.reshape(1, QKV_A)


def _prep_b(q_gain, k_gain):
    gains = jnp.concatenate([jnp.tile(q_gain * (HEAD_DIM ** -0.5), HEADS_B),
                             jnp.tile(k_gain, HEADS_B),
                             jnp.ones((OUT_B,), F32)])
    return gains.reshape(1, QKV_B)


def _bias_table_b(rpb):
    w, rows, halo, n_sub = GRID_W, ATT_B_ROWS, ATT_B_HALO, ATT_B_ROWS // ATT_B_SUB
    col = np.arange(w)
    col_start = np.clip(col - NA_COLS // 2, 0, w - NA_COLS)
    col_ok = (col[None, :] >= col_start[:, None]) & (col[None, :] < col_start[:, None] + NA_COLS)
    dc = np.clip(col[None, :] - col[:, None] + NA_COLS - 1, 0, 2 * NA_COLS - 2)
    lo_lim = np.array([[halo, rows], [0, rows], [0, halo]])
    rho = ATT_B_SUB * np.arange(n_sub)[:, None] + np.arange(ATT_B_SUB)[None, :]
    lo = np.clip(rho[None], lo_lim[:, 0, None, None], lo_lim[:, 1, None, None])
    ws = np.clip(ATT_B_SUB * np.arange(n_sub)[None], lo_lim[:, 0, None], lo_lim[:, 1, None])
    jl = ws[:, :, None, None] + np.arange(ATT_B_WIN)[None, None, None, :]
    row_ok = (jl >= lo[..., None]) & (jl < lo[..., None] + NA_ROWS)
    dr = np.clip(jl - rho[None, :, :, None] + NA_ROWS // 2 - 1, 0, 2 * NA_ROWS - 2)
    cols_tbl = rpb[:, :, dc]
    tbl = cols_tbl[:, dr]
    ok = row_ok[None, ..., None, None] & col_ok[None, None, None, None, None]
    tbl = jnp.where(ok, tbl, NEG_INF)
    tbl = tbl.transpose(0, 1, 2, 3, 5, 4, 6)
    return tbl.reshape(HEADS_B, 3, n_sub, ATT_B_SUB * w, ATT_B_WIN * w).astype(F32)


def _mixer_a(x, batch, seq_len, mix_g, w_qkv, gains, w_o, seg, rope_tables):
    qkv = _qkv_a(x, mix_g, w_qkv, gains, seg, rope_tables, batch, seq_len)
    outs, lses = [], []
    for group, (window, dilation) in enumerate(DILATED_GROUPS):
        assert window // (2 * dilation) == ATT_A_SIDE
        o, lse = _attn_a_group(qkv[group], qkv[N_GROUPS_A + group], qkv[2 * N_GROUPS_A + group],
                               dilation, batch, seq_len)
        outs.append(o)
        lses.append(lse)
    return _merge_proj_a(x, outs, lses, w_o, seq_len)


def _mixer_b(x, batch, seq_len, mix_g, w_qkv, gains, bias, w_o, seg):
    qkv = _qkv_b(x, mix_g, w_qkv, gains, seg, batch, seq_len)
    o = _attn_b(qkv, bias, batch, seq_len)
    return _proj_b(x, o.reshape(batch * seq_len, OUT_B), w_o)


def kernel(x_prompt, x_sample, ffn1_norm, ffn1_w_in, ffn1_w_out, mix_norm, ffn2_norm, ffn2_w_in, ffn2_w_out, a_w_qkv, a_q_norm, a_k_norm, a_w_o, b_w_qkv, b_q_norm, b_k_norm, b_rpb, b_w_o):
    depth = ffn1_norm.shape[0]
    bf = lambda w: w.astype(BF16)
    ffn1_w_in, ffn1_w_out, ffn2_w_in, ffn2_w_out = map(bf, (ffn1_w_in, ffn1_w_out, ffn2_w_in, ffn2_w_out))
    a_w_o, b_w_qkv, b_w_o = map(bf, (a_w_o, b_w_qkv, b_w_o))
    seg_a = _segment_mean_matrix(_rope_layout()[1])
    seg_b = _segment_mean_matrix(np.arange(QKV_CHUNK) // HEAD_DIM)
    prep_a = [_prep_a(a_w_qkv[j], a_q_norm[j], a_k_norm[j]) for j in range(a_w_qkv.shape[0])]
    prep_b = [(_prep_b(b_q_norm[j], b_k_norm[j]), _bias_table_b(b_rpb[j])) for j in range(b_w_qkv.shape[0])]

    def trunk(x3):
        batch, seq_len, _ = x3.shape
        x = x3.reshape(batch * seq_len, D_MODEL)
        rope_tables = _rope_tables(seq_len)
        for i in range(depth):
            j = i // 2
            x = _ffn(x, ffn1_norm[i], ffn1_w_in[i], ffn1_w_out[i])
            if i % 2 == 0:
                w_qkv, gains = prep_a[j]
                x = _mixer_a(x, batch, seq_len, mix_norm[i], w_qkv, gains, a_w_o[j], seg_a, rope_tables)
            else:
                gains, bias = prep_b[j]
                x = _mixer_b(x, batch, seq_len, mix_norm[i], b_w_qkv[j], gains, bias, b_w_o[j], seg_b)
            x = _ffn(x, ffn2_norm[i], ffn2_w_in[i], ffn2_w_out[i])
        return x.reshape(batch, seq_len, D_MODEL)

    return (trunk(x_prompt), trunk(x_sample))
```

```python
import functools

import jax
import jax.numpy as jnp
import numpy as np
from jax import lax
from jax.experimental import pallas as pl
from jax.experimental.pallas import tpu as pltpu

D_MODEL = 1024
HEAD_DIM = 64
D_FF = 2816
RMS_EPS = 1e-6
ROPE_THETA = 10000.0
NEG_INF = -1e30
DILATED_GROUPS = ((128, 1), (512, 4), (2048, 16))
HEADS_A = 8
N_GROUPS_A = 3
QKV_A = 3 * N_GROUPS_A * HEADS_A * HEAD_DIM
OUT_A = HEADS_A * HEAD_DIM
HEADS_B = 16
QKV_B = 3 * HEADS_B * HEAD_DIM
OUT_B = HEADS_B * HEAD_DIM
GRID_W = 64
NA_ROWS = 8
NA_COLS = 16

LANES = 128
VMEM_LIMIT_BYTES = 56 * 1024 * 1024

TOKEN_TILE = 512
FFN_TILE = 1024
FF_CHUNK = 512
SEG_COLS = 256
QKV_CHUNK = 512
PAIRS_PER_CHUNK = QKV_CHUNK // LANES
ATT_A_TILE = 512
ATT_A_QB = 128
ATT_A_SIDE = 64
ATT_B_ROWS = 8
ATT_B_HALO = 4
ATT_B_PAIRS = 2
ATT_B_SUB = 2
ATT_B_WIN = NA_ROWS + ATT_B_SUB

BF16 = jnp.bfloat16
F32 = jnp.float32
NT_DIMS = (((1,), (1,)), ((), ()))


def _params(*semantics):
    return pltpu.CompilerParams(dimension_semantics=semantics,
                                vmem_limit_bytes=VMEM_LIMIT_BYTES)


def _rms_rows(x, g):
    ms = jnp.mean(x * x, axis=-1, keepdims=True)
    return (x * lax.rsqrt(ms + RMS_EPS)) * g


def _ffn_kernel(x_ref, g_ref, win_ref, wout_ref, o_ref, hn_ref):
    hn_ref[...] = _rms_rows(x_ref[...], g_ref[...]).astype(BF16)
    for c0 in range(0, D_FF, FF_CHUNK):
        width = min(FF_CHUNK, D_FF - c0)
        h = hn_ref[...]
        gate = jnp.dot(h, win_ref[:, c0:c0 + width], preferred_element_type=F32)
        up = jnp.dot(h, win_ref[:, D_FF + c0:D_FF + c0 + width], preferred_element_type=F32)
        a = (gate * jax.nn.sigmoid(gate) * up).astype(BF16)
        half = 0.5 * jnp.dot(a, wout_ref[c0:c0 + width, :], preferred_element_type=F32)
        if c0 == 0:
            o_ref[...] = x_ref[...] + half
        else:
            o_ref[...] += half


def _ffn(x, g, w_in, w_out):
    t = x.shape[0]
    const = lambda i: (0, 0)
    resident = pl.Buffered(1)
    return pl.pallas_call(
        _ffn_kernel,
        grid=(t // FFN_TILE,),
        in_specs=[
            pl.BlockSpec((FFN_TILE, D_MODEL), lambda i: (i, 0)),
            pl.BlockSpec((1, D_MODEL), const),
            pl.BlockSpec((D_MODEL, 2 * D_FF), const, pipeline_mode=resident),
            pl.BlockSpec((D_FF, D_MODEL), const, pipeline_mode=resident),
        ],
        out_specs=pl.BlockSpec((FFN_TILE, D_MODEL), lambda i: (i, 0)),
        out_shape=jax.ShapeDtypeStruct((t, D_MODEL), F32),
        scratch_shapes=[pltpu.VMEM((FFN_TILE, D_MODEL), BF16)],
        compiler_params=_params("parallel"),
        name="ffn",
    )(x, g.reshape(1, D_MODEL), w_in, w_out)


def _project_chunk(hn_ref, w_ref, gain_ref, seg_ref, c, normed):
    cols = slice(c * QKV_CHUNK, (c + 1) * QKV_CHUNK)
    y = jnp.dot(hn_ref[...], w_ref[:, cols], preferred_element_type=F32)
    if normed:
        y2 = (y * y).astype(BF16)
        ms = jnp.concatenate(
            [jnp.dot(y2[:, s0:s0 + SEG_COLS], seg_ref[...], preferred_element_type=F32)
             for s0 in range(0, QKV_CHUNK, SEG_COLS)], axis=-1)
        y = (y * lax.rsqrt(ms + RMS_EPS)) * gain_ref[:, cols]
    return y


def _qkv_a_kernel(x_ref, g_ref, w_ref, gain_ref, seg_ref, cos_ref, sin_ref, *rest):
    n_chunks = 3 * N_GROUPS_A
    out_refs, (hn_ref, ys_ref) = rest[:n_chunks], rest[n_chunks:]
    hn_ref[...] = _rms_rows(x_ref[...], g_ref[...]).astype(BF16)
    cos, sin = cos_ref[...], sin_ref[...]
    for c in range(n_chunks):
        kind, group = divmod(c, N_GROUPS_A)
        dilation = DILATED_GROUPS[group][1]
        normed = kind < 2
        y = _project_chunk(hn_ref, w_ref, gain_ref, seg_ref, c, normed)
        if normed:
            y = jnp.concatenate(
                [y[:, p * LANES:(p + 1) * LANES] * cos
                 + pltpu.roll(y[:, p * LANES:(p + 1) * LANES], LANES // 2, axis=1) * sin
                 for p in range(PAIRS_PER_CHUNK)], axis=-1)
        if dilation == 1:
            out_refs[c][0, 0] = y.astype(BF16)
        else:
            slot = kind * (N_GROUPS_A - 1) + group - 1
            rows = TOKEN_TILE // dilation
            for p in range(PAIRS_PER_CHUNK):
                cols = slice(p * LANES, (p + 1) * LANES)
                ys_ref[slot, p] = y[:, cols]
                for r in range(dilation):
                    out_refs[c][0, r, :, cols] = ys_ref[slot, p, pl.ds(r, rows, stride=dilation), :].astype(BF16)


def _qkv_a(x, g, w, gains, seg, rope_tables, batch, seq_len):
    t = x.shape[0]
    tps = seq_len // TOKEN_TILE
    const = lambda i: (0, 0)
    in_specs = [
        pl.BlockSpec((TOKEN_TILE, D_MODEL), lambda i: (i, 0)),
        pl.BlockSpec((1, D_MODEL), const),
        pl.BlockSpec((D_MODEL, QKV_A), const),
        pl.BlockSpec((1, QKV_A), const),
        pl.BlockSpec((SEG_COLS, SEG_COLS), const),
        pl.BlockSpec((TOKEN_TILE, LANES), lambda i: (i % tps, 0)),
        pl.BlockSpec((TOKEN_TILE, LANES), lambda i: (i % tps, 0)),
    ]
    out_specs, out_shapes = [], []
    for c in range(3 * N_GROUPS_A):
        d = DILATED_GROUPS[c % N_GROUPS_A][1]
        out_specs.append(pl.BlockSpec((1, d, TOKEN_TILE // d, QKV_CHUNK),
                                      lambda i: (i // tps, 0, i % tps, 0)))
        out_shapes.append(jax.ShapeDtypeStruct((batch, d, seq_len // d, QKV_CHUNK), BF16))
    return pl.pallas_call(
        _qkv_a_kernel,
        grid=(t // TOKEN_TILE,),
        in_specs=in_specs,
        out_specs=out_specs,
        out_shape=out_shapes,
        scratch_shapes=[pltpu.VMEM((TOKEN_TILE, D_MODEL), BF16),
                        pltpu.VMEM((3 * (N_GROUPS_A - 1), PAIRS_PER_CHUNK, TOKEN_TILE, LANES), F32)],
        compiler_params=_params("parallel"),
        name="qkv_a",
    )(x, g.reshape(1, D_MODEL), w, gains, seg, *rope_tables)


def _qkv_b_kernel(x_ref, g_ref, w_ref, gain_ref, seg_ref, o_ref, hn_ref):
    hn_ref[...] = _rms_rows(x_ref[...], g_ref[...]).astype(BF16)
    for c in range(QKV_B // QKV_CHUNK):
        y = _project_chunk(hn_ref, w_ref, gain_ref, seg_ref, c, c < 2 * OUT_B // QKV_CHUNK)
        for p in range(PAIRS_PER_CHUNK):
            o_ref[0, c * PAIRS_PER_CHUNK + p] = y[:, p * LANES:(p + 1) * LANES].astype(BF16)


def _qkv_b(x, g, w, gains, seg, batch, seq_len):
    t = x.shape[0]
    tps = seq_len // TOKEN_TILE
    n_pairs = QKV_B // LANES
    const = lambda i: (0, 0)
    return pl.pallas_call(
        _qkv_b_kernel,
        grid=(t // TOKEN_TILE,),
        in_specs=[
            pl.BlockSpec((TOKEN_TILE, D_MODEL), lambda i: (i, 0)),
            pl.BlockSpec((1, D_MODEL), const),
            pl.BlockSpec((D_MODEL, QKV_B), const),
            pl.BlockSpec((1, QKV_B), const),
            pl.BlockSpec((SEG_COLS, SEG_COLS), const),
        ],
        out_specs=pl.BlockSpec((1, n_pairs, TOKEN_TILE, LANES), lambda i: (i // tps, 0, i % tps, 0)),
        out_shape=jax.ShapeDtypeStruct((batch, n_pairs, seq_len, LANES), BF16),
        scratch_shapes=[pltpu.VMEM((TOKEN_TILE, D_MODEL), BF16)],
        compiler_params=_params("parallel"),
        name="qkv_b",
    )(x, g.reshape(1, D_MODEL), w, gains, seg)


def _attn_a_kernel(q_ref, kp_ref, kc_ref, kn_ref, vp_ref, vc_ref, vn_ref,
                   o_ref, lse_ref, kw_ref, vw_ref, s_ref, p_ref, m_ref, *, seq_len):
    side, qb, tq = ATT_A_SIDE, ATT_A_QB, ATT_A_TILE
    kb = qb + 2 * side
    nb = tq // qb
    kw_ref[0:side] = kp_ref[0, 0]
    kw_ref[side:side + tq] = kc_ref[0, 0]
    kw_ref[side + tq:] = kn_ref[0, 0]
    ones = jnp.ones((tq + 2 * side, LANES), BF16)
    for p in range(PAIRS_PER_CHUNK):
        src = slice(p * LANES, (p + 1) * LANES)
        dst = slice(2 * p * LANES, (2 * p + 1) * LANES)
        vw_ref[0:side, dst] = vp_ref[0, 0, :, src]
        vw_ref[side:side + tq, dst] = vc_ref[0, 0, :, src]
        vw_ref[side + tq:, dst] = vn_ref[0, 0, :, src]
        vw_ref[:, (2 * p + 1) * LANES:(2 * p + 2) * LANES] = ones

    q_start = pl.program_id(2) * tq
    row = lax.broadcasted_iota(jnp.int32, (qb, kb), 0)
    col = lax.broadcasted_iota(jnp.int32, (qb, kb), 1)
    band = (col >= row) & (col - row <= 2 * side)
    biases = []
    for b in range(nb):
        kpos = q_start + b * qb + col - side
        bias = jnp.where(band & (kpos >= 0) & (kpos < seq_len), 0.0, NEG_INF).astype(F32)
        biases.append(jnp.concatenate([bias, bias], axis=0))
    bias4 = jnp.stack(biases)

    lane = lax.broadcasted_iota(jnp.int32, (tq, LANES), 1)
    q_head_a = (lane % HEAD_DIM) < (HEAD_DIM // 2)
    v_head_a = lax.broadcasted_iota(jnp.int32, (qb, LANES), 1) < HEAD_DIM
    zero = jnp.zeros((tq, LANES), BF16)

    for p in range(PAIRS_PER_CHUNK):
        cols = slice(p * LANES, (p + 1) * LANES)
        q2 = q_ref[0, 0, :, cols]
        qa = jnp.where(q_head_a, q2, zero)
        qbb = jnp.where(q_head_a, zero, q2)
        for b in range(nb):
            rows = slice(b * qb, (b + 1) * qb)
            lhs = jnp.concatenate([qa[rows], qbb[rows]], axis=0)
            s_ref[p * nb + b] = lax.dot_general(lhs, kw_ref[b * qb:b * qb + kb, cols], NT_DIMS,
                                                preferred_element_type=F32)
        blk = slice(p * nb, (p + 1) * nb)
        s = s_ref[blk] + bias4
        m = jnp.max(s, axis=-1, keepdims=True)
        p_ref[blk] = jnp.exp(s - m).astype(BF16)
        m_ref[blk] = jnp.broadcast_to(m, (nb, 2 * qb, LANES))
        for b in range(nb):
            rows = slice(b * qb, (b + 1) * qb)
            pv = jnp.dot(p_ref[p * nb + b], vw_ref[b * qb:b * qb + kb, 2 * p * LANES:(2 * p + 2) * LANES],
                         preferred_element_type=F32)
            den = pv[:, LANES:]
            o = pv[:, :LANES] / den
            lse = m_ref[p * nb + b] + jnp.log(den)
            o_ref[0, 0, rows, cols] = jnp.where(v_head_a, o[:qb], o[qb:]).astype(BF16)
            lse_ref[0, 0, rows, cols] = jnp.where(v_head_a, lse[:qb], lse[qb:])


def _attn_a_group(q, k, v, dilation, batch, seq_len):
    length = seq_len // dilation
    tq, side = ATT_A_TILE, ATT_A_SIDE
    n_tiles = length // tq
    halo_blocks = length // side
    nb = tq // ATT_A_QB
    kb = ATT_A_QB + 2 * side
    cur = pl.BlockSpec((1, 1, tq, OUT_A), lambda b, r, i: (b, r, i, 0))
    prev = pl.BlockSpec((1, 1, side, OUT_A),
                        lambda b, r, i: (b, r, jnp.maximum(i * (tq // side) - 1, 0), 0))
    nxt = pl.BlockSpec((1, 1, side, OUT_A),
                       lambda b, r, i: (b, r, jnp.minimum((i + 1) * (tq // side), halo_blocks - 1), 0))
    n_blk = PAIRS_PER_CHUNK * nb
    return pl.pallas_call(
        functools.partial(_attn_a_kernel, seq_len=length),
        grid=(batch, dilation, n_tiles),
        in_specs=[cur, prev, cur, nxt, prev, cur, nxt],
        out_specs=[cur, cur],
        out_shape=[jax.ShapeDtypeStruct((batch, dilation, length, OUT_A), BF16),
                   jax.ShapeDtypeStruct((batch, dilation, length, OUT_A), F32)],
        scratch_shapes=[pltpu.VMEM((tq + 2 * side, OUT_A), BF16),
                        pltpu.VMEM((tq + 2 * side, 2 * OUT_A), BF16),
                        pltpu.VMEM((n_blk, 2 * ATT_A_QB, kb), F32),
                        pltpu.VMEM((n_blk, 2 * ATT_A_QB, kb), BF16),
                        pltpu.VMEM((n_blk, 2 * ATT_A_QB, LANES), F32)],
        compiler_params=_params("parallel", "parallel", "arbitrary"),
        name=f"attn_a_d{dilation}",
    )(q, k, k, k, v, v, v)


def _merge_proj_a_kernel(x_ref, o0_ref, o1_ref, o2_ref, l0_ref, l1_ref, l2_ref, w_ref, y_ref,
                         ot_ref, lt_ref):
    for slot, (o_ref, l_ref, dilation) in enumerate(
            ((o1_ref, l1_ref, DILATED_GROUPS[1][1]), (o2_ref, l2_ref, DILATED_GROUPS[2][1]))):
        rows = TOKEN_TILE // dilation
        for r in range(dilation):
            for p in range(PAIRS_PER_CHUNK):
                cols = slice(p * LANES, (p + 1) * LANES)
                ot_ref[slot, p, pl.ds(r, rows, stride=dilation), :] = o_ref[0, r, :, cols].astype(F32)
                lt_ref[slot, p, pl.ds(r, rows, stride=dilation), :] = l_ref[0, r, :, cols]
    merged = []
    for p in range(PAIRS_PER_CHUNK):
        cols = slice(p * LANES, (p + 1) * LANES)
        l0, l1, l2 = l0_ref[0, 0, :, cols], lt_ref[0, p], lt_ref[1, p]
        m = jnp.maximum(jnp.maximum(l0, l1), l2)
        e0, e1, e2 = jnp.exp(l0 - m), jnp.exp(l1 - m), jnp.exp(l2 - m)
        inv = 1.0 / (e0 + e1 + e2)
        merged.append((e0 * inv) * o0_ref[0, 0, :, cols].astype(F32) + (e1 * inv) * ot_ref[0, p]
                      + (e2 * inv) * ot_ref[1, p])
    o = jnp.concatenate(merged, axis=-1)
    y_ref[...] = x_ref[...] + jnp.dot(o.astype(BF16), w_ref[...], preferred_element_type=F32)


def _merge_proj_a(x, outs, lses, w_o, seq_len):
    t = x.shape[0]
    tps = seq_len // TOKEN_TILE
    row = pl.BlockSpec((TOKEN_TILE, D_MODEL), lambda i: (i, 0))
    grp = [pl.BlockSpec((1, d, TOKEN_TILE // d, OUT_A), lambda i: (i // tps, 0, i % tps, 0))
           for _, d in DILATED_GROUPS]
    return pl.pallas_call(
        _merge_proj_a_kernel,
        grid=(t // TOKEN_TILE,),
        in_specs=[row] + grp + grp + [pl.BlockSpec((OUT_A, D_MODEL), lambda i: (0, 0))],
        out_specs=row,
        out_shape=jax.ShapeDtypeStruct((t, D_MODEL), F32),
        scratch_shapes=[pltpu.VMEM((N_GROUPS_A - 1, PAIRS_PER_CHUNK, TOKEN_TILE, LANES), F32),
                        pltpu.VMEM((N_GROUPS_A - 1, PAIRS_PER_CHUNK, TOKEN_TILE, LANES), F32)],
        compiler_params=_params("parallel"),
        name="merge_proj_a",
    )(x, *outs, *lses, w_o)


def _attn_b_kernel(q_ref, kp_ref, kc_ref, kn_ref, vp_ref, vc_ref, vn_ref, bias_ref,
                   o_ref, kw_ref, vw_ref, s_ref, p_ref):
    w, halo, rows = GRID_W, ATT_B_HALO, ATT_B_ROWS
    n_sub = rows // ATT_B_SUB
    sub_q = ATT_B_SUB * w
    win = ATT_B_WIN * w
    ones = jnp.ones(((rows + 2 * halo) * w, LANES), BF16)
    for u in range(ATT_B_PAIRS):
        kw_ref[u, 0:halo * w] = kp_ref[0, u]
        kw_ref[u, halo * w:(halo + rows) * w] = kc_ref[0, u]
        kw_ref[u, (halo + rows) * w:] = kn_ref[0, u]
        vw_ref[u, 0:halo * w, :LANES] = vp_ref[0, u]
        vw_ref[u, halo * w:(halo + rows) * w, :LANES] = vc_ref[0, u]
        vw_ref[u, (halo + rows) * w:, :LANES] = vn_ref[0, u]
        vw_ref[u, :, LANES:] = ones

    i = pl.program_id(2)
    lo_min = jnp.where(i == 0, halo, 0)
    lo_max = jnp.where(i == pl.num_programs(2) - 1, halo, rows)
    starts = [pl.multiple_of(jnp.clip(ATT_B_SUB * sb, lo_min, lo_max) * w, LANES) for sb in range(n_sub)]
    head_a = lax.broadcasted_iota(jnp.int32, (rows * w, LANES), 1) < HEAD_DIM
    out_a = lax.broadcasted_iota(jnp.int32, (sub_q, LANES), 1) < HEAD_DIM
    zero = jnp.zeros((rows * w, LANES), BF16)
    for u in range(ATT_B_PAIRS):
        q2 = q_ref[0, u]
        qa = jnp.where(head_a, q2, zero)
        qb = jnp.where(head_a, zero, q2)
        for sb in range(n_sub):
            rs = slice(sb * sub_q, (sb + 1) * sub_q)
            lhs = jnp.concatenate([qa[rs], qb[rs]], axis=0)
            s_ref[u, sb] = lax.dot_general(lhs, kw_ref[u, pl.ds(starts[sb], win), :], NT_DIMS,
                                           preferred_element_type=F32)
    for u in range(ATT_B_PAIRS):
        s = s_ref[u] + jnp.concatenate([bias_ref[2 * u, 0], bias_ref[2 * u + 1, 0]], axis=1)
        m = jnp.max(s, axis=-1, keepdims=True)
        p_ref[u] = jnp.exp(s - m).astype(BF16)
    for u in range(ATT_B_PAIRS):
        for sb in range(n_sub):
            pv = jnp.dot(p_ref[u, sb], vw_ref[u, pl.ds(starts[sb], win), :], preferred_element_type=F32)
            o = pv[:, :LANES] / pv[:, LANES:]
            o_ref[0, sb * sub_q:(sb + 1) * sub_q, u * LANES:(u + 1) * LANES] = (
                jnp.where(out_a, o[:sub_q], o[sub_q:]).astype(BF16))


def _attn_b(qkv, bias, batch, seq_len):
    n_rows = seq_len // GRID_W
    tile = ATT_B_ROWS * GRID_W
    halo = ATT_B_HALO * GRID_W
    n_tiles = n_rows // ATT_B_ROWS
    assert n_tiles >= 2
    per_tile = tile // halo
    halo_blocks = seq_len // halo
    n_steps = HEADS_B // 2 // ATT_B_PAIRS
    n_sub = ATT_B_ROWS // ATT_B_SUB
    sub_q = ATT_B_SUB * GRID_W
    win = ATT_B_WIN * GRID_W

    def cur(kind):
        return pl.BlockSpec((1, ATT_B_PAIRS, tile, LANES), lambda hp, b, i: (b, kind * n_steps + hp, i, 0))

    def prev(kind):
        return pl.BlockSpec((1, ATT_B_PAIRS, halo, LANES),
                            lambda hp, b, i: (b, kind * n_steps + hp, jnp.maximum(i * per_tile - 1, 0), 0))

    def nxt(kind):
        return pl.BlockSpec((1, ATT_B_PAIRS, halo, LANES),
                            lambda hp, b, i: (b, kind * n_steps + hp,
                                              jnp.minimum((i + 1) * per_tile, halo_blocks - 1), 0))

    def tile_kind(i):
        return jnp.where(i == 0, 0, jnp.where(i == n_tiles - 1, 2, 1))

    return pl.pallas_call(
        _attn_b_kernel,
        grid=(n_steps, batch, n_tiles),
        in_specs=[cur(0), prev(1), cur(1), nxt(1), prev(2), cur(2), nxt(2),
                  pl.BlockSpec((2 * ATT_B_PAIRS, 1, n_sub, sub_q, win),
                               lambda hp, b, i: (hp, tile_kind(i), 0, 0, 0))],
        out_specs=pl.BlockSpec((1, tile, ATT_B_PAIRS * LANES), lambda hp, b, i: (b, i, hp)),
        out_shape=jax.ShapeDtypeStruct((batch, seq_len, OUT_B), BF16),
        scratch_shapes=[pltpu.VMEM((ATT_B_PAIRS, tile + 2 * halo, LANES), BF16),
                        pltpu.VMEM((ATT_B_PAIRS, tile + 2 * halo, 2 * LANES), BF16),
                        pltpu.VMEM((ATT_B_PAIRS, n_sub, 2 * sub_q, win), F32),
                        pltpu.VMEM((ATT_B_PAIRS, n_sub, 2 * sub_q, win), BF16)],
        compiler_params=_params("arbitrary", "arbitrary", "arbitrary"),
        name="attn_b",
    )(qkv, qkv, qkv, qkv, qkv, qkv, qkv, bias)


def _proj_b_kernel(x_ref, o_ref, w_ref, y_ref):
    y_ref[...] = x_ref[...] + jnp.dot(o_ref[...], w_ref[...], preferred_element_type=F32)


def _proj_b(x, o, w_o):
    t = x.shape[0]
    return pl.pallas_call(
        _proj_b_kernel,
        grid=(t // TOKEN_TILE,),
        in_specs=[pl.BlockSpec((TOKEN_TILE, D_MODEL), lambda i: (i, 0)),
                  pl.BlockSpec((TOKEN_TILE, OUT_B), lambda i: (i, 0)),
                  pl.BlockSpec((OUT_B, D_MODEL), lambda i: (0, 0))],
        out_specs=pl.BlockSpec((TOKEN_TILE, D_MODEL), lambda i: (i, 0)),
        out_shape=jax.ShapeDtypeStruct((t, D_MODEL), F32),
        compiler_params=_params("parallel"),
        name="proj_b",
    )(x, o, w_o)


def _rope_layout():
    n = np.arange(QKV_CHUNK)
    pair, lane = n // LANES, n % LANES
    second, hb, i = lane // HEAD_DIM, (lane % HEAD_DIM) // (HEAD_DIM // 2), lane % (HEAD_DIM // 2)
    head = 2 * pair + hb
    return head * HEAD_DIM + second * (HEAD_DIM // 2) + i, head


def _segment_mean_matrix(head_of_column):
    same = head_of_column[:, None] == head_of_column[None, :]
    return jnp.asarray(np.where(same, 1.0 / HEAD_DIM, 0.0), BF16)


def _rope_tables(seq_len):
    half = HEAD_DIM // 2
    inv_freq = ROPE_THETA ** (-jnp.arange(half, dtype=F32) / half)
    ang = jnp.arange(seq_len, dtype=F32)[:, None] * inv_freq[None, :]
    cos, sin = jnp.cos(ang), jnp.sin(ang)
    return (jnp.concatenate([cos, cos, cos, cos], axis=-1),
            jnp.concatenate([-sin, -sin, sin, sin], axis=-1))


def _prep_a(w_qkv, q_gain, k_gain):
    src, _ = _rope_layout()
    n_qk = 2 * N_GROUPS_A
    cols = np.concatenate([c * QKV_CHUNK + src for c in range(n_qk)]
                          + [np.arange(n_qk * QKV_CHUNK, QKV_A)])
    elem = src % HEAD_DIM
    gains = jnp.concatenate([jnp.tile(q_gain[elem] * (HEAD_DIM ** -0.5), N_GROUPS_A),
                             jnp.tile(k_gain[elem], N_GROUPS_A),
                             jnp.ones((QKV_A - n_qk * QKV_CHUNK,), F32)])
    return w_qkv[:, cols].astype(BF16), gains.reshape(1, QKV_A)


def _prep_b(q_gain, k_gain):
    gains = jnp.concatenate([jnp.tile(q_gain * (HEAD_DIM ** -0.5), HEADS_B),
                             jnp.tile(k_gain, HEADS_B),
                             jnp.ones((OUT_B,), F32)])
    return gains.reshape(1, QKV_B)


def _bias_table_b(rpb):
    w, rows, halo, n_sub = GRID_W, ATT_B_ROWS, ATT_B_HALO, ATT_B_ROWS // ATT_B_SUB
    col = np.arange(w)
    col_start = np.clip(col - NA_COLS // 2, 0, w - NA_COLS)
    col_ok = (col[None, :] >= col_start[:, None]) & (col[None, :] < col_start[:, None] + NA_COLS)
    dc = np.clip(col[None, :] - col[:, None] + NA_COLS - 1, 0, 2 * NA_COLS - 2)
    lo_lim = np.array([[halo, rows], [0, rows], [0, halo]])
    rho = ATT_B_SUB * np.arange(n_sub)[:, None] + np.arange(ATT_B_SUB)[None, :]
    lo = np.clip(rho[None], lo_lim[:, 0, None, None], lo_lim[:, 1, None, None])
    ws = np.clip(ATT_B_SUB * np.arange(n_sub)[None], lo_lim[:, 0, None], lo_lim[:, 1, None])
    jl = ws[:, :, None, None] + np.arange(ATT_B_WIN)[None, None, None, :]
    row_ok = (jl >= lo[..., None]) & (jl < lo[..., None] + NA_ROWS)
    dr = np.clip(jl - rho[None, :, :, None] + NA_ROWS // 2 - 1, 0, 2 * NA_ROWS - 2)
    cols_tbl = rpb[:, :, dc]
    tbl = cols_tbl[:, dr]
    ok = row_ok[None, ..., None, None] & col_ok[None, None, None, None, None]
    tbl = jnp.where(ok, tbl, NEG_INF)
    tbl = tbl.transpose(0, 1, 2, 3, 5, 4, 6)
    return tbl.reshape(HEADS_B, 3, n_sub, ATT_B_SUB * w, ATT_B_WIN * w).astype(F32)


def _mixer_a(x, batch, seq_len, mix_g, w_qkv, gains, w_o, seg, rope_tables):
    qkv = _qkv_a(x, mix_g, w_qkv, gains, seg, rope_tables, batch, seq_len)
    outs, lses = [], []
    for group, (window, dilation) in enumerate(DILATED_GROUPS):
        assert window // (2 * dilation) == ATT_A_SIDE
        o, lse = _attn_a_group(qkv[group], qkv[N_GROUPS_A + group], qkv[2 * N_GROUPS_A + group],
                               dilation, batch, seq_len)
        outs.append(o)
        lses.append(lse)
    return _merge_proj_a(x, outs, lses, w_o, seq_len)


def _mixer_b(x, batch, seq_len, mix_g, w_qkv, gains, bias, w_o, seg):
    qkv = _qkv_b(x, mix_g, w_qkv, gains, seg, batch, seq_len)
    o = _attn_b(qkv, bias, batch, seq_len)
    return _proj_b(x, o.reshape(batch * seq_len, OUT_B), w_o)


def kernel(x_prompt, x_sample, ffn1_norm, ffn1_w_in, ffn1_w_out, mix_norm, ffn2_norm, ffn2_w_in, ffn2_w_out, a_w_qkv, a_q_norm, a_k_norm, a_w_o, b_w_qkv, b_q_norm, b_k_norm, b_rpb, b_w_o):
    depth = ffn1_norm.shape[0]
    bf = lambda w: w.astype(BF16)
    ffn1_w_in, ffn1_w_out, ffn2_w_in, ffn2_w_out = map(bf, (ffn1_w_in, ffn1_w_out, ffn2_w_in, ffn2_w_out))
    a_w_o, b_w_qkv, b_w_o = map(bf, (a_w_o, b_w_qkv, b_w_o))
    seg_a = _segment_mean_matrix(_rope_layout()[1][:SEG_COLS])
    seg_b = _segment_mean_matrix(np.arange(SEG_COLS) // HEAD_DIM)
    prep_a = [_prep_a(a_w_qkv[j], a_q_norm[j], a_k_norm[j]) for j in range(a_w_qkv.shape[0])]
    prep_b = [(_prep_b(b_q_norm[j], b_k_norm[j]), _bias_table_b(b_rpb[j])) for j in range(b_w_qkv.shape[0])]

    def trunk(x3):
        batch, seq_len, _ = x3.shape
        x = x3.reshape(batch * seq_len, D_MODEL)
        rope_tables = _rope_tables(seq_len)
        for i in range(depth):
            j = i // 2
            x = _ffn(x, ffn1_norm[i], ffn1_w_in[i], ffn1_w_out[i])
            if i % 2 == 0:
                w_qkv, gains = prep_a[j]
                x = _mixer_a(x, batch, seq_len, mix_norm[i], w_qkv, gains, a_w_o[j], seg_a, rope_tables)
            else:
                gains, bias = prep_b[j]
                x = _mixer_b(x, batch, seq_len, mix_norm[i], b_w_qkv[j], gains, bias, b_w_o[j], seg_b)
            x = _ffn(x, ffn2_norm[i], ffn2_w_in[i], ffn2_w_out[i])
        return x.reshape(batch, seq_len, D_MODEL)

    return (trunk(x_prompt), trunk(x_sample))
```

```python
import functools

import jax
import jax.numpy as jnp
import numpy as np
from jax import lax
from jax.experimental import pallas as pl
from jax.experimental.pallas import tpu as pltpu

D_MODEL = 1024
HEAD_DIM = 64
D_FF = 2816
RMS_EPS = 1e-6
ROPE_THETA = 10000.0
NEG_INF = -1e30
DILATED_GROUPS = ((128, 1), (512, 4), (2048, 16))
HEADS_A = 8
N_GROUPS_A = 3
QKV_A = 3 * N_GROUPS_A * HEADS_A * HEAD_DIM
OUT_A = HEADS_A * HEAD_DIM
HEADS_B = 16
QKV_B = 3 * HEADS_B * HEAD_DIM
OUT_B = HEADS_B * HEAD_DIM
GRID_W = 64
NA_ROWS = 8
NA_COLS = 16

LANES = 128
VMEM_LIMIT_BYTES = 56 * 1024 * 1024

TOKEN_TILE = 512
FFN_TILE = 1024
FF_CHUNK = 512
SEG_COLS = 256
QKV_CHUNK = 512
PAIRS_PER_CHUNK = QKV_CHUNK // LANES
ATT_A_TILE = 512
ATT_A_QB = 128
ATT_A_SIDE = 64
ATT_B_ROWS = 8
ATT_B_HALO = 4
ATT_B_PAIRS = 2
ATT_B_SUB = 2
ATT_B_WIN = NA_ROWS + ATT_B_SUB

BF16 = jnp.bfloat16
F32 = jnp.float32
NT_DIMS = (((1,), (1,)), ((), ()))


def _params(*semantics):
    return pltpu.CompilerParams(dimension_semantics=semantics,
                                vmem_limit_bytes=VMEM_LIMIT_BYTES)


def _rms_rows(x, g):
    ms = jnp.mean(x * x, axis=-1, keepdims=True)
    return (x * lax.rsqrt(ms + RMS_EPS)) * g


def _ffn_kernel(*refs, fused_proj):
    if fused_proj:
        x_ref, a_ref, wa_ref, g_ref, win_ref, wout_ref, o_ref, hn_ref = refs
        o_ref[...] = x_ref[...] + jnp.dot(a_ref[...], wa_ref[...], preferred_element_type=F32)
    else:
        x_ref, g_ref, win_ref, wout_ref, o_ref, hn_ref = refs
        o_ref[...] = x_ref[...]
    hn_ref[...] = _rms_rows(o_ref[...], g_ref[...]).astype(BF16)
    for c0 in range(0, D_FF, FF_CHUNK):
        width = min(FF_CHUNK, D_FF - c0)
        h = hn_ref[...]
        gate = jnp.dot(h, win_ref[:, c0:c0 + width], preferred_element_type=F32)
        up = jnp.dot(h, win_ref[:, D_FF + c0:D_FF + c0 + width], preferred_element_type=F32)
        a = (gate * jax.nn.sigmoid(gate) * up).astype(BF16)
        o_ref[...] += 0.5 * jnp.dot(a, wout_ref[c0:c0 + width, :], preferred_element_type=F32)


def _ffn(x, g, w_in, w_out, attn=None):
    t = x.shape[0]
    const = lambda i: (0, 0)
    resident = pl.Buffered(1)
    row = pl.BlockSpec((FFN_TILE, D_MODEL), lambda i: (i, 0))
    in_specs, args = [row], [x]
    if attn is not None:
        a, w_a = attn
        in_specs += [pl.BlockSpec((FFN_TILE, a.shape[1]), lambda i: (i, 0)),
                     pl.BlockSpec(w_a.shape, const, pipeline_mode=resident)]
        args += [a, w_a]
    in_specs += [pl.BlockSpec((1, D_MODEL), const),
                 pl.BlockSpec((D_MODEL, 2 * D_FF), const, pipeline_mode=resident),
                 pl.BlockSpec((D_FF, D_MODEL), const, pipeline_mode=resident)]
    args += [g.reshape(1, D_MODEL), w_in, w_out]
    return pl.pallas_call(
        functools.partial(_ffn_kernel, fused_proj=attn is not None),
        grid=(t // FFN_TILE,),
        in_specs=in_specs,
        out_specs=row,
        out_shape=jax.ShapeDtypeStruct((t, D_MODEL), F32),
        scratch_shapes=[pltpu.VMEM((FFN_TILE, D_MODEL), BF16)],
        compiler_params=_params("parallel"),
        name="ffn_proj" if attn is not None else "ffn",
    )(*args)


def _project_chunk(hn_ref, w_ref, gain_ref, seg_ref, c, normed):
    cols = slice(c * QKV_CHUNK, (c + 1) * QKV_CHUNK)
    y = jnp.dot(hn_ref[...], w_ref[:, cols], preferred_element_type=F32)
    if normed:
        y2 = (y * y).astype(BF16)
        ms = jnp.concatenate(
            [jnp.dot(y2[:, s0:s0 + SEG_COLS], seg_ref[...], preferred_element_type=F32)
             for s0 in range(0, QKV_CHUNK, SEG_COLS)], axis=-1)
        y = (y * lax.rsqrt(ms + RMS_EPS)) * gain_ref[:, cols]
    return y


def _qkv_a_kernel(x_ref, g_ref, w_ref, gain_ref, seg_ref, cos_ref, sin_ref, *rest):
    n_chunks = 3 * N_GROUPS_A
    out_refs, (hn_ref, ys_ref) = rest[:n_chunks], rest[n_chunks:]
    hn_ref[...] = _rms_rows(x_ref[...], g_ref[...]).astype(BF16)
    cos, sin = cos_ref[...], sin_ref[...]
    for c in range(n_chunks):
        kind, group = divmod(c, N_GROUPS_A)
        dilation = DILATED_GROUPS[group][1]
        normed = kind < 2
        y = _project_chunk(hn_ref, w_ref, gain_ref, seg_ref, c, normed)
        if normed:
            y = jnp.concatenate(
                [y[:, p * LANES:(p + 1) * LANES] * cos
                 + pltpu.roll(y[:, p * LANES:(p + 1) * LANES], LANES // 2, axis=1) * sin
                 for p in range(PAIRS_PER_CHUNK)], axis=-1)
        if dilation == 1:
            out_refs[c][0, 0] = y.astype(BF16)
        else:
            slot = kind * (N_GROUPS_A - 1) + group - 1
            rows = TOKEN_TILE // dilation
            for p in range(PAIRS_PER_CHUNK):
                cols = slice(p * LANES, (p + 1) * LANES)
                ys_ref[slot, p] = y[:, cols]
                for r in range(dilation):
                    out_refs[c][0, r, :, cols] = ys_ref[slot, p, pl.ds(r, rows, stride=dilation), :].astype(BF16)


def _qkv_a(x, g, w, gains, seg, rope_tables, batch, seq_len):
    t = x.shape[0]
    tps = seq_len // TOKEN_TILE
    const = lambda i: (0, 0)
    in_specs = [
        pl.BlockSpec((TOKEN_TILE, D_MODEL), lambda i: (i, 0)),
        pl.BlockSpec((1, D_MODEL), const),
        pl.BlockSpec((D_MODEL, QKV_A), const),
        pl.BlockSpec((1, QKV_A), const),
        pl.BlockSpec((SEG_COLS, SEG_COLS), const),
        pl.BlockSpec((TOKEN_TILE, LANES), lambda i: (i % tps, 0)),
        pl.BlockSpec((TOKEN_TILE, LANES), lambda i: (i % tps, 0)),
    ]
    out_specs, out_shapes = [], []
    for c in range(3 * N_GROUPS_A):
        d = DILATED_GROUPS[c % N_GROUPS_A][1]
        out_specs.append(pl.BlockSpec((1, d, TOKEN_TILE // d, QKV_CHUNK),
                                      lambda i: (i // tps, 0, i % tps, 0)))
        out_shapes.append(jax.ShapeDtypeStruct((batch, d, seq_len // d, QKV_CHUNK), BF16))
    return pl.pallas_call(
        _qkv_a_kernel,
        grid=(t // TOKEN_TILE,),
        in_specs=in_specs,
        out_specs=out_specs,
        out_shape=out_shapes,
        scratch_shapes=[pltpu.VMEM((TOKEN_TILE, D_MODEL), BF16),
                        pltpu.VMEM((3 * (N_GROUPS_A - 1), PAIRS_PER_CHUNK, TOKEN_TILE, LANES), F32)],
        compiler_params=_params("parallel"),
        name="qkv_a",
    )(x, g.reshape(1, D_MODEL), w, gains, seg, *rope_tables)


def _qkv_b_kernel(x_ref, g_ref, w_ref, gain_ref, seg_ref, o_ref, hn_ref):
    hn_ref[...] = _rms_rows(x_ref[...], g_ref[...]).astype(BF16)
    for c in range(QKV_B // QKV_CHUNK):
        y = _project_chunk(hn_ref, w_ref, gain_ref, seg_ref, c, c < 2 * OUT_B // QKV_CHUNK)
        for p in range(PAIRS_PER_CHUNK):
            o_ref[0, c * PAIRS_PER_CHUNK + p] = y[:, p * LANES:(p + 1) * LANES].astype(BF16)


def _qkv_b(x, g, w, gains, seg, batch, seq_len):
    t = x.shape[0]
    tps = seq_len // TOKEN_TILE
    n_pairs = QKV_B // LANES
    const = lambda i: (0, 0)
    return pl.pallas_call(
        _qkv_b_kernel,
        grid=(t // TOKEN_TILE,),
        in_specs=[
            pl.BlockSpec((TOKEN_TILE, D_MODEL), lambda i: (i, 0)),
            pl.BlockSpec((1, D_MODEL), const),
            pl.BlockSpec((D_MODEL, QKV_B), const),
            pl.BlockSpec((1, QKV_B), const),
            pl.BlockSpec((SEG_COLS, SEG_COLS), const),
        ],
        out_specs=pl.BlockSpec((1, n_pairs, TOKEN_TILE, LANES), lambda i: (i // tps, 0, i % tps, 0)),
        out_shape=jax.ShapeDtypeStruct((batch, n_pairs, seq_len, LANES), BF16),
        scratch_shapes=[pltpu.VMEM((TOKEN_TILE, D_MODEL), BF16)],
        compiler_params=_params("parallel"),
        name="qkv_b",
    )(x, g.reshape(1, D_MODEL), w, gains, seg)


def _attn_a_kernel(q_ref, kp_ref, kc_ref, kn_ref, vp_ref, vc_ref, vn_ref,
                   o_ref, lse_ref, kw_ref, vw_ref, s_ref, p_ref, m_ref, *, seq_len):
    side, qb, tq = ATT_A_SIDE, ATT_A_QB, ATT_A_TILE
    kb = qb + 2 * side
    nb = tq // qb
    kw_ref[0:side] = kp_ref[0, 0]
    kw_ref[side:side + tq] = kc_ref[0, 0]
    kw_ref[side + tq:] = kn_ref[0, 0]
    ones = jnp.ones((tq + 2 * side, LANES), BF16)
    for p in range(PAIRS_PER_CHUNK):
        src = slice(p * LANES, (p + 1) * LANES)
        dst = slice(2 * p * LANES, (2 * p + 1) * LANES)
        vw_ref[0:side, dst] = vp_ref[0, 0, :, src]
        vw_ref[side:side + tq, dst] = vc_ref[0, 0, :, src]
        vw_ref[side + tq:, dst] = vn_ref[0, 0, :, src]
        vw_ref[:, (2 * p + 1) * LANES:(2 * p + 2) * LANES] = ones

    q_start = pl.program_id(2) * tq
    row = lax.broadcasted_iota(jnp.int32, (qb, kb), 0)
    col = lax.broadcasted_iota(jnp.int32, (qb, kb), 1)
    band = (col >= row) & (col - row <= 2 * side)
    biases = []
    for b in range(nb):
        kpos = q_start + b * qb + col - side
        bias = jnp.where(band & (kpos >= 0) & (kpos < seq_len), 0.0, NEG_INF).astype(F32)
        biases.append(jnp.concatenate([bias, bias], axis=0))
    bias4 = jnp.stack(biases)

    lane = lax.broadcasted_iota(jnp.int32, (tq, LANES), 1)
    q_head_a = (lane % HEAD_DIM) < (HEAD_DIM // 2)
    v_head_a = lax.broadcasted_iota(jnp.int32, (qb, LANES), 1) < HEAD_DIM
    zero = jnp.zeros((tq, LANES), BF16)

    for p in range(PAIRS_PER_CHUNK):
        cols = slice(p * LANES, (p + 1) * LANES)
        q2 = q_ref[0, 0, :, cols]
        qa = jnp.where(q_head_a, q2, zero)
        qbb = jnp.where(q_head_a, zero, q2)
        for b in range(nb):
            rows = slice(b * qb, (b + 1) * qb)
            lhs = jnp.concatenate([qa[rows], qbb[rows]], axis=0)
            s_ref[p * nb + b] = lax.dot_general(lhs, kw_ref[b * qb:b * qb + kb, cols], NT_DIMS,
                                                preferred_element_type=F32)
        blk = slice(p * nb, (p + 1) * nb)
        s = s_ref[blk] + bias4
        m = jnp.max(s, axis=-1, keepdims=True)
        p_ref[blk] = jnp.exp(s - m).astype(BF16)
        m_ref[blk] = jnp.broadcast_to(m, (nb, 2 * qb, LANES))
        for b in range(nb):
            rows = slice(b * qb, (b + 1) * qb)
            pv = jnp.dot(p_ref[p * nb + b], vw_ref[b * qb:b * qb + kb, 2 * p * LANES:(2 * p + 2) * LANES],
                         preferred_element_type=F32)
            den = pv[:, LANES:]
            o = pv[:, :LANES] / den
            lse = m_ref[p * nb + b] + jnp.log(den)
            o_ref[0, 0, rows, cols] = jnp.where(v_head_a, o[:qb], o[qb:]).astype(BF16)
            lse_ref[0, 0, rows, cols] = jnp.where(v_head_a, lse[:qb], lse[qb:])


def _attn_a_group(q, k, v, dilation, batch, seq_len):
    length = seq_len // dilation
    tq, side = ATT_A_TILE, ATT_A_SIDE
    n_tiles = length // tq
    halo_blocks = length // side
    nb = tq // ATT_A_QB
    kb = ATT_A_QB + 2 * side
    cur = pl.BlockSpec((1, 1, tq, OUT_A), lambda b, r, i: (b, r, i, 0))
    prev = pl.BlockSpec((1, 1, side, OUT_A),
                        lambda b, r, i: (b, r, jnp.maximum(i * (tq // side) - 1, 0), 0))
    nxt = pl.BlockSpec((1, 1, side, OUT_A),
                       lambda b, r, i: (b, r, jnp.minimum((i + 1) * (tq // side), halo_blocks - 1), 0))
    n_blk = PAIRS_PER_CHUNK * nb
    return pl.pallas_call(
        functools.partial(_attn_a_kernel, seq_len=length),
        grid=(batch, dilation, n_tiles),
        in_specs=[cur, prev, cur, nxt, prev, cur, nxt],
        out_specs=[cur, cur],
        out_shape=[jax.ShapeDtypeStruct((batch, dilation, length, OUT_A), BF16),
                   jax.ShapeDtypeStruct((batch, dilation, length, OUT_A), F32)],
        scratch_shapes=[pltpu.VMEM((tq + 2 * side, OUT_A), BF16),
                        pltpu.VMEM((tq + 2 * side, 2 * OUT_A), BF16),
                        pltpu.VMEM((n_blk, 2 * ATT_A_QB, kb), F32),
                        pltpu.VMEM((n_blk, 2 * ATT_A_QB, kb), BF16),
                        pltpu.VMEM((n_blk, 2 * ATT_A_QB, LANES), F32)],
        compiler_params=_params("parallel", "parallel", "arbitrary"),
        name=f"attn_a_d{dilation}",
    )(q, k, k, k, v, v, v)


def _merge_a_kernel(o0_ref, o1_ref, o2_ref, l0_ref, l1_ref, l2_ref, y_ref, ot_ref, lt_ref):
    for slot, (o_ref, l_ref, dilation) in enumerate(
            ((o1_ref, l1_ref, DILATED_GROUPS[1][1]), (o2_ref, l2_ref, DILATED_GROUPS[2][1]))):
        rows = TOKEN_TILE // dilation
        for r in range(dilation):
            for p in range(PAIRS_PER_CHUNK):
                cols = slice(p * LANES, (p + 1) * LANES)
                ot_ref[slot, p, pl.ds(r, rows, stride=dilation), :] = o_ref[0, r, :, cols].astype(F32)
                lt_ref[slot, p, pl.ds(r, rows, stride=dilation), :] = l_ref[0, r, :, cols]
    for p in range(PAIRS_PER_CHUNK):
        cols = slice(p * LANES, (p + 1) * LANES)
        l0, l1, l2 = l0_ref[0, 0, :, cols], lt_ref[0, p], lt_ref[1, p]
        m = jnp.maximum(jnp.maximum(l0, l1), l2)
        e0, e1, e2 = jnp.exp(l0 - m), jnp.exp(l1 - m), jnp.exp(l2 - m)
        inv = 1.0 / (e0 + e1 + e2)
        y_ref[:, cols] = ((e0 * inv) * o0_ref[0, 0, :, cols].astype(F32) + (e1 * inv) * ot_ref[0, p]
                          + (e2 * inv) * ot_ref[1, p]).astype(BF16)


def _merge_a(outs, lses, batch, seq_len):
    t = batch * seq_len
    tps = seq_len // TOKEN_TILE
    grp = [pl.BlockSpec((1, d, TOKEN_TILE // d, OUT_A), lambda i: (i // tps, 0, i % tps, 0))
           for _, d in DILATED_GROUPS]
    return pl.pallas_call(
        _merge_a_kernel,
        grid=(t // TOKEN_TILE,),
        in_specs=grp + grp,
        out_specs=pl.BlockSpec((TOKEN_TILE, OUT_A), lambda i: (i, 0)),
        out_shape=jax.ShapeDtypeStruct((t, OUT_A), BF16),
        scratch_shapes=[pltpu.VMEM((N_GROUPS_A - 1, PAIRS_PER_CHUNK, TOKEN_TILE, LANES), F32),
                        pltpu.VMEM((N_GROUPS_A - 1, PAIRS_PER_CHUNK, TOKEN_TILE, LANES), F32)],
        compiler_params=_params("parallel"),
        name="merge_a",
    )(*outs, *lses)


def _attn_b_kernel(q_ref, kp_ref, kc_ref, kn_ref, vp_ref, vc_ref, vn_ref, bias_ref,
                   o_ref, kw_ref, vw_ref, s_ref, p_ref):
    w, halo, rows = GRID_W, ATT_B_HALO, ATT_B_ROWS
    n_sub = rows // ATT_B_SUB
    sub_q = ATT_B_SUB * w
    win = ATT_B_WIN * w
    ones = jnp.ones(((rows + 2 * halo) * w, LANES), BF16)
    for u in range(ATT_B_PAIRS):
        kw_ref[u, 0:halo * w] = kp_ref[0, u]
        kw_ref[u, halo * w:(halo + rows) * w] = kc_ref[0, u]
        kw_ref[u, (halo + rows) * w:] = kn_ref[0, u]
        vw_ref[u, 0:halo * w, :LANES] = vp_ref[0, u]
        vw_ref[u, halo * w:(halo + rows) * w, :LANES] = vc_ref[0, u]
        vw_ref[u, (halo + rows) * w:, :LANES] = vn_ref[0, u]
        vw_ref[u, :, LANES:] = ones

    i = pl.program_id(2)
    lo_min = jnp.where(i == 0, halo, 0)
    lo_max = jnp.where(i == pl.num_programs(2) - 1, halo, rows)
    starts = [pl.multiple_of(jnp.clip(ATT_B_SUB * sb, lo_min, lo_max) * w, LANES) for sb in range(n_sub)]
    head_a = lax.broadcasted_iota(jnp.int32, (rows * w, LANES), 1) < HEAD_DIM
    out_a = lax.broadcasted_iota(jnp.int32, (sub_q, LANES), 1) < HEAD_DIM
    zero = jnp.zeros((rows * w, LANES), BF16)
    for u in range(ATT_B_PAIRS):
        q2 = q_ref[0, u]
        qa = jnp.where(head_a, q2, zero)
        qb = jnp.where(head_a, zero, q2)
        for sb in range(n_sub):
            rs = slice(sb * sub_q, (sb + 1) * sub_q)
            lhs = jnp.concatenate([qa[rs], qb[rs]], axis=0)
            s_ref[u, sb] = lax.dot_general(lhs, kw_ref[u, pl.ds(starts[sb], win), :], NT_DIMS,
                                           preferred_element_type=F32)
    for u in range(ATT_B_PAIRS):
        s = s_ref[u] + jnp.concatenate([bias_ref[2 * u, 0], bias_ref[2 * u + 1, 0]], axis=1)
        m = jnp.max(s, axis=-1, keepdims=True)
        p_ref[u] = jnp.exp(s - m).astype(BF16)
    for u in range(ATT_B_PAIRS):
        for sb in range(n_sub):
            pv = jnp.dot(p_ref[u, sb], vw_ref[u, pl.ds(starts[sb], win), :], preferred_element_type=F32)
            o = pv[:, :LANES] / pv[:, LANES:]
            o_ref[0, sb * sub_q:(sb + 1) * sub_q, u * LANES:(u + 1) * LANES] = (
                jnp.where(out_a, o[:sub_q], o[sub_q:]).astype(BF16))


def _attn_b(qkv, bias, batch, seq_len):
    n_rows = seq_len // GRID_W
    tile = ATT_B_ROWS * GRID_W
    halo = ATT_B_HALO * GRID_W
    n_tiles = n_rows // ATT_B_ROWS
    assert n_tiles >= 2
    per_tile = tile // halo
    halo_blocks = seq_len // halo
    n_steps = HEADS_B // 2 // ATT_B_PAIRS
    n_sub = ATT_B_ROWS // ATT_B_SUB
    sub_q = ATT_B_SUB * GRID_W
    win = ATT_B_WIN * GRID_W

    def cur(kind):
        return pl.BlockSpec((1, ATT_B_PAIRS, tile, LANES), lambda hp, b, i: (b, kind * n_steps + hp, i, 0))

    def prev(kind):
        return pl.BlockSpec((1, ATT_B_PAIRS, halo, LANES),
                            lambda hp, b, i: (b, kind * n_steps + hp, jnp.maximum(i * per_tile - 1, 0), 0))

    def nxt(kind):
        return pl.BlockSpec((1, ATT_B_PAIRS, halo, LANES),
                            lambda hp, b, i: (b, kind * n_steps + hp,
                                              jnp.minimum((i + 1) * per_tile, halo_blocks - 1), 0))

    def tile_kind(i):
        return jnp.where(i == 0, 0, jnp.where(i == n_tiles - 1, 2, 1))

    return pl.pallas_call(
        _attn_b_kernel,
        grid=(n_steps, batch, n_tiles),
        in_specs=[cur(0), prev(1), cur(1), nxt(1), prev(2), cur(2), nxt(2),
                  pl.BlockSpec((2 * ATT_B_PAIRS, 1, n_sub, sub_q, win),
                               lambda hp, b, i: (hp, tile_kind(i), 0, 0, 0))],
        out_specs=pl.BlockSpec((1, tile, ATT_B_PAIRS * LANES), lambda hp, b, i: (b, i, hp)),
        out_shape=jax.ShapeDtypeStruct((batch, seq_len, OUT_B), BF16),
        scratch_shapes=[pltpu.VMEM((ATT_B_PAIRS, tile + 2 * halo, LANES), BF16),
                        pltpu.VMEM((ATT_B_PAIRS, tile + 2 * halo, 2 * LANES), BF16),
                        pltpu.VMEM((ATT_B_PAIRS, n_sub, 2 * sub_q, win), F32),
                        pltpu.VMEM((ATT_B_PAIRS, n_sub, 2 * sub_q, win), BF16)],
        compiler_params=_params("arbitrary", "arbitrary", "arbitrary"),
        name="attn_b",
    )(qkv, qkv, qkv, qkv, qkv, qkv, qkv, bias)


def _rope_layout():
    n = np.arange(QKV_CHUNK)
    pair, lane = n // LANES, n % LANES
    second, hb, i = lane // HEAD_DIM, (lane % HEAD_DIM) // (HEAD_DIM // 2), lane % (HEAD_DIM // 2)
    head = 2 * pair + hb
    return head * HEAD_DIM + second * (HEAD_DIM // 2) + i, head


def _segment_mean_matrix(head_of_column):
    same = head_of_column[:, None] == head_of_column[None, :]
    return jnp.asarray(np.where(same, 1.0 / HEAD_DIM, 0.0), BF16)


def _rope_tables(seq_len):
    half = HEAD_DIM // 2
    inv_freq = ROPE_THETA ** (-jnp.arange(half, dtype=F32) / half)
    ang = jnp.arange(seq_len, dtype=F32)[:, None] * inv_freq[None, :]
    cos, sin = jnp.cos(ang), jnp.sin(ang)
    return (jnp.concatenate([cos, cos, cos, cos], axis=-1),
            jnp.concatenate([-sin, -sin, sin, sin], axis=-1))


def _prep_a(w_qkv, q_gain, k_gain):
    src, _ = _rope_layout()
    n_qk = 2 * N_GROUPS_A
    cols = np.concatenate([c * QKV_CHUNK + src for c in range(n_qk)]
                          + [np.arange(n_qk * QKV_CHUNK, QKV_A)])
    elem = src % HEAD_DIM
    gains = jnp.concatenate([jnp.tile(q_gain[elem] * (HEAD_DIM ** -0.5), N_GROUPS_A),
                             jnp.tile(k_gain[elem], N_GROUPS_A),
                             jnp.ones((QKV_A - n_qk * QKV_CHUNK,), F32)])
    return w_qkv[:, cols].astype(BF16), gains.reshape(1, QKV_A)


def _prep_b(q_gain, k_gain):
    gains = jnp.concatenate([jnp.tile(q_gain * (HEAD_DIM ** -0.5), HEADS_B),
                             jnp.tile(k_gain, HEADS_B),
                             jnp.ones((OUT_B,), F32)])
    return gains.reshape(1, QKV_B)


def _bias_tile_plan():
    halo, rows, n_sub = ATT_B_HALO, ATT_B_ROWS, ATT_B_ROWS // ATT_B_SUB
    plan = {}
    for kind, (lo_min, lo_max) in enumerate(((halo, rows), (0, rows), (0, halo))):
        for sb in range(n_sub):
            ws = min(max(ATT_B_SUB * sb, lo_min), lo_max)
            for ql in range(ATT_B_SUB):
                rho = ATT_B_SUB * sb + ql
                lo = min(max(rho, lo_min), lo_max)
                for jj in range(ATT_B_WIN // 2):
                    jl = ws + 2 * jj
                    dr = jl - rho + NA_ROWS // 2 - 1
                    plan[kind, sb, ql, jj] = (dr, lo <= jl < lo + NA_ROWS, lo <= jl + 1 < lo + NA_ROWS)
    return plan


def _bias_expand_kernel(c_ref, o_ref):
    w = GRID_W
    left = lax.broadcasted_iota(jnp.int32, (w, 2 * w), 1) < w
    neg = jnp.full((w, 2 * w), NEG_INF, F32)
    for (kind, sb, ql, jj), (dr, ok_a, ok_b) in _bias_tile_plan().items():
        if ok_a or ok_b:
            tile = c_ref[0, dr + 1]
            if not ok_b:
                tile = jnp.where(left, tile, neg)
            elif not ok_a:
                tile = jnp.where(left, neg, tile)
        else:
            tile = neg
        o_ref[0, kind, sb, ql * w:(ql + 1) * w, 2 * jj * w:2 * (jj + 1) * w] = tile


def _bias_table_b(rpb):
    w, n_sub, n_dr = GRID_W, ATT_B_ROWS // ATT_B_SUB, 2 * NA_ROWS - 1
    col = np.arange(w)
    col_start = np.clip(col - NA_COLS // 2, 0, w - NA_COLS)
    col_ok = (col[None, :] >= col_start[:, None]) & (col[None, :] < col_start[:, None] + NA_COLS)
    dc = np.clip(col[None, :] - col[:, None] + NA_COLS - 1, 0, 2 * NA_COLS - 2)
    by_row = jnp.where(col_ok[None, None], rpb[:, :, dc], NEG_INF)
    pad = jnp.full((HEADS_B, 1, w, w), NEG_INF, F32)
    by_row = jnp.concatenate([pad, by_row, pad], axis=1)
    pairs = jnp.concatenate([by_row[:, :-1], by_row[:, 1:]], axis=-1)
    return pl.pallas_call(
        _bias_expand_kernel,
        grid=(HEADS_B,),
        in_specs=[pl.BlockSpec((1, n_dr + 1, w, 2 * w), lambda h: (h, 0, 0, 0))],
        out_specs=pl.BlockSpec((1, 3, n_sub, ATT_B_SUB * w, ATT_B_WIN * w), lambda h: (h, 0, 0, 0, 0)),
        out_shape=jax.ShapeDtypeStruct((HEADS_B, 3, n_sub, ATT_B_SUB * w, ATT_B_WIN * w), F32),
        compiler_params=_params("parallel"),
        name="bias_expand",
    )(pairs)


def _mixer_a(x, batch, seq_len, mix_g, w_qkv, gains, seg, rope_tables):
    qkv = _qkv_a(x, mix_g, w_qkv, gains, seg, rope_tables, batch, seq_len)
    outs, lses = [], []
    for group, (window, dilation) in enumerate(DILATED_GROUPS):
        assert window // (2 * dilation) == ATT_A_SIDE
        o, lse = _attn_a_group(qkv[group], qkv[N_GROUPS_A + group], qkv[2 * N_GROUPS_A + group],
                               dilation, batch, seq_len)
        outs.append(o)
        lses.append(lse)
    return _merge_a(outs, lses, batch, seq_len)


def _mixer_b(x, batch, seq_len, mix_g, w_qkv, gains, bias, seg):
    qkv = _qkv_b(x, mix_g, w_qkv, gains, seg, batch, seq_len)
    return _attn_b(qkv, bias, batch, seq_len).reshape(batch * seq_len, OUT_B)


def kernel(x_prompt, x_sample, ffn1_norm, ffn1_w_in, ffn1_w_out, mix_norm, ffn2_norm, ffn2_w_in, ffn2_w_out, a_w_qkv, a_q_norm, a_k_norm, a_w_o, b_w_qkv, b_q_norm, b_k_norm, b_rpb, b_w_o):
    depth = ffn1_norm.shape[0]
    bf = lambda w: w.astype(BF16)
    ffn1_w_in, ffn1_w_out, ffn2_w_in, ffn2_w_out = map(bf, (ffn1_w_in, ffn1_w_out, ffn2_w_in, ffn2_w_out))
    a_w_o, b_w_qkv, b_w_o = map(bf, (a_w_o, b_w_qkv, b_w_o))
    seg_a = _segment_mean_matrix(_rope_layout()[1][:SEG_COLS])
    seg_b = _segment_mean_matrix(np.arange(SEG_COLS) // HEAD_DIM)
    prep_a = [_prep_a(a_w_qkv[j], a_q_norm[j], a_k_norm[j]) for j in range(a_w_qkv.shape[0])]
    prep_b = [(_prep_b(b_q_norm[j], b_k_norm[j]), _bias_table_b(b_rpb[j])) for j in range(b_w_qkv.shape[0])]

    def trunk(x3):
        batch, seq_len, _ = x3.shape
        x = x3.reshape(batch * seq_len, D_MODEL)
        rope_tables = _rope_tables(seq_len)
        for i in range(depth):
            j = i // 2
            x = _ffn(x, ffn1_norm[i], ffn1_w_in[i], ffn1_w_out[i])
            if i % 2 == 0:
                w_qkv, gains = prep_a[j]
                attn = (_mixer_a(x, batch, seq_len, mix_norm[i], w_qkv, gains, seg_a, rope_tables), a_w_o[j])
            else:
                gains, bias = prep_b[j]
                attn = (_mixer_b(x, batch, seq_len, mix_norm[i], b_w_qkv[j], gains, bias, seg_b), b_w_o[j])
            x = _ffn(x, ffn2_norm[i], ffn2_w_in[i], ffn2_w_out[i], attn)
        return x.reshape(batch, seq_len, D_MODEL)

    return (trunk(x_prompt), trunk(x_sample))
```

```python
import functools

import jax
import jax.numpy as jnp
import numpy as np
from jax import lax
from jax.experimental import pallas as pl
from jax.experimental.pallas import tpu as pltpu

D_MODEL = 1024
HEAD_DIM = 64
D_FF = 2816
RMS_EPS = 1e-6
ROPE_THETA = 10000.0
NEG_INF = -1e30
LOG2E = 1.4426950408889634
DILATED_GROUPS = ((128, 1), (512, 4), (2048, 16))
HEADS_A = 8
N_GROUPS_A = 3
QKV_A = 3 * N_GROUPS_A * HEADS_A * HEAD_DIM
OUT_A = HEADS_A * HEAD_DIM
HEADS_B = 16
QKV_B = 3 * HEADS_B * HEAD_DIM
OUT_B = HEADS_B * HEAD_DIM
GRID_W = 64
NA_ROWS = 8
NA_COLS = 16

LANES = 128
VMEM_LIMIT_BYTES = 56 * 1024 * 1024

TOKEN_TILE = 512
QKV_TILE = 1024
FFN_TILE = 1024
FF_CHUNK = 512
SEG_COLS = 256
QKV_CHUNK = 512
PAIRS_PER_CHUNK = QKV_CHUNK // LANES
ATT_A_TILE = 512
ATT_A_QB = 128
ATT_A_SIDE = 64
ATT_B_ROWS = 8
ATT_B_HALO = 4
ATT_B_PAIRS = 4
ATT_B_SUB = 2
ATT_B_WIN = NA_ROWS + ATT_B_SUB

BF16 = jnp.bfloat16
F32 = jnp.float32
NT_DIMS = (((1,), (1,)), ((), ()))
Q_SCALE = HEAD_DIM ** -0.5 * LOG2E


def _params(*semantics):
    return pltpu.CompilerParams(dimension_semantics=semantics,
                                vmem_limit_bytes=VMEM_LIMIT_BYTES)


def _rms_rows(x, g):
    ms = jnp.mean(x * x, axis=-1, keepdims=True)
    return (x * lax.rsqrt(ms + RMS_EPS)) * g


def _ffn_kernel(*refs, fused_proj):
    if fused_proj:
        x_ref, a_ref, wa_ref, g_ref, win_ref, wout_ref, o_ref, hn_ref = refs
        o_ref[...] = x_ref[...] + jnp.dot(a_ref[...], wa_ref[...], preferred_element_type=F32)
    else:
        x_ref, g_ref, win_ref, wout_ref, o_ref, hn_ref = refs
        o_ref[...] = x_ref[...]
    hn_ref[...] = _rms_rows(o_ref[...], g_ref[...]).astype(BF16)
    for c0 in range(0, D_FF, FF_CHUNK):
        width = min(FF_CHUNK, D_FF - c0)
        h = hn_ref[...]
        gate = jnp.dot(h, win_ref[:, c0:c0 + width], preferred_element_type=F32)
        up = jnp.dot(h, win_ref[:, D_FF + c0:D_FF + c0 + width], preferred_element_type=F32)
        a = (gate * jax.nn.sigmoid(gate) * up).astype(BF16)
        o_ref[...] += 0.5 * jnp.dot(a, wout_ref[c0:c0 + width, :], preferred_element_type=F32)


def _ffn(x, g, w_in, w_out, attn=None):
    t = x.shape[0]
    const = lambda i: (0, 0)
    resident = pl.Buffered(1)
    row = pl.BlockSpec((FFN_TILE, D_MODEL), lambda i: (i, 0))
    in_specs, args = [row], [x]
    if attn is not None:
        a, w_a = attn
        in_specs += [pl.BlockSpec((FFN_TILE, a.shape[1]), lambda i: (i, 0)),
                     pl.BlockSpec(w_a.shape, const, pipeline_mode=resident)]
        args += [a, w_a]
    in_specs += [pl.BlockSpec((1, D_MODEL), const),
                 pl.BlockSpec((D_MODEL, 2 * D_FF), const, pipeline_mode=resident),
                 pl.BlockSpec((D_FF, D_MODEL), const, pipeline_mode=resident)]
    args += [g.reshape(1, D_MODEL), w_in, w_out]
    return pl.pallas_call(
        functools.partial(_ffn_kernel, fused_proj=attn is not None),
        grid=(t // FFN_TILE,),
        in_specs=in_specs,
        out_specs=row,
        out_shape=jax.ShapeDtypeStruct((t, D_MODEL), F32),
        scratch_shapes=[pltpu.VMEM((FFN_TILE, D_MODEL), BF16)],
        compiler_params=_params("parallel"),
        name="ffn_proj" if attn is not None else "ffn",
    )(*args)


def _project_chunk(hn_ref, w_ref, gain_ref, seg_ref, c, normed):
    cols = slice(c * QKV_CHUNK, (c + 1) * QKV_CHUNK)
    y = jnp.dot(hn_ref[...], w_ref[:, cols], preferred_element_type=F32)
    if normed:
        y2 = (y * y).astype(BF16)
        ms = jnp.concatenate(
            [jnp.dot(y2[:, s0:s0 + SEG_COLS], seg_ref[...], preferred_element_type=F32)
             for s0 in range(0, QKV_CHUNK, SEG_COLS)], axis=-1)
        y = (y * lax.rsqrt(ms + RMS_EPS)) * gain_ref[:, cols]
    return y


def _qkv_a_kernel(x_ref, g_ref, w_ref, gain_ref, seg_ref, cos_ref, sin_ref, *rest):
    n_chunks = 3 * N_GROUPS_A
    out_refs, (hn_ref, ys_ref) = rest[:n_chunks], rest[n_chunks:]
    hn_ref[...] = _rms_rows(x_ref[...], g_ref[...]).astype(BF16)
    cos, sin = cos_ref[...], sin_ref[...]
    for c in range(n_chunks):
        kind, group = divmod(c, N_GROUPS_A)
        dilation = DILATED_GROUPS[group][1]
        normed = kind < 2
        y = _project_chunk(hn_ref, w_ref, gain_ref, seg_ref, c, normed)
        if normed:
            y = jnp.concatenate(
                [y[:, p * LANES:(p + 1) * LANES] * cos
                 + pltpu.roll(y[:, p * LANES:(p + 1) * LANES], LANES // 2, axis=1) * sin
                 for p in range(PAIRS_PER_CHUNK)], axis=-1)
        if dilation == 1:
            out_refs[c][0, 0] = y.astype(BF16)
        else:
            slot = kind * (N_GROUPS_A - 1) + group - 1
            rows = QKV_TILE // dilation
            for p in range(PAIRS_PER_CHUNK):
                cols = slice(p * LANES, (p + 1) * LANES)
                ys_ref[slot, p] = y[:, cols]
                for r in range(dilation):
                    out_refs[c][0, r, :, cols] = ys_ref[slot, p, pl.ds(r, rows, stride=dilation), :].astype(BF16)


def _qkv_a(x, g, w, gains, seg, rope_tables, batch, seq_len):
    t = x.shape[0]
    tps = seq_len // QKV_TILE
    const = lambda i: (0, 0)
    in_specs = [
        pl.BlockSpec((QKV_TILE, D_MODEL), lambda i: (i, 0)),
        pl.BlockSpec((1, D_MODEL), const),
        pl.BlockSpec((D_MODEL, QKV_A), const, pipeline_mode=pl.Buffered(1)),
        pl.BlockSpec((1, QKV_A), const),
        pl.BlockSpec((SEG_COLS, SEG_COLS), const),
        pl.BlockSpec((QKV_TILE, LANES), lambda i: (i % tps, 0)),
        pl.BlockSpec((QKV_TILE, LANES), lambda i: (i % tps, 0)),
    ]
    out_specs, out_shapes = [], []
    for c in range(3 * N_GROUPS_A):
        d = DILATED_GROUPS[c % N_GROUPS_A][1]
        out_specs.append(pl.BlockSpec((1, d, QKV_TILE // d, QKV_CHUNK),
                                      lambda i: (i // tps, 0, i % tps, 0)))
        out_shapes.append(jax.ShapeDtypeStruct((batch, d, seq_len // d, QKV_CHUNK), BF16))
    return pl.pallas_call(
        _qkv_a_kernel,
        grid=(t // QKV_TILE,),
        in_specs=in_specs,
        out_specs=out_specs,
        out_shape=out_shapes,
        scratch_shapes=[pltpu.VMEM((QKV_TILE, D_MODEL), BF16),
                        pltpu.VMEM((3 * (N_GROUPS_A - 1), PAIRS_PER_CHUNK, QKV_TILE, LANES), F32)],
        compiler_params=_params("parallel"),
        name="qkv_a",
    )(x, g.reshape(1, D_MODEL), w, gains, seg, *rope_tables)


def _qkv_b_kernel(x_ref, g_ref, w_ref, gain_ref, seg_ref, o_ref, hn_ref):
    hn_ref[...] = _rms_rows(x_ref[...], g_ref[...]).astype(BF16)
    for c in range(QKV_B // QKV_CHUNK):
        y = _project_chunk(hn_ref, w_ref, gain_ref, seg_ref, c, c < 2 * OUT_B // QKV_CHUNK)
        for p in range(PAIRS_PER_CHUNK):
            o_ref[0, c * PAIRS_PER_CHUNK + p] = y[:, p * LANES:(p + 1) * LANES].astype(BF16)


def _qkv_b(x, g, w, gains, seg, batch, seq_len):
    t = x.shape[0]
    tps = seq_len // QKV_TILE
    n_pairs = QKV_B // LANES
    const = lambda i: (0, 0)
    return pl.pallas_call(
        _qkv_b_kernel,
        grid=(t // QKV_TILE,),
        in_specs=[
            pl.BlockSpec((QKV_TILE, D_MODEL), lambda i: (i, 0)),
            pl.BlockSpec((1, D_MODEL), const),
            pl.BlockSpec((D_MODEL, QKV_B), const, pipeline_mode=pl.Buffered(1)),
            pl.BlockSpec((1, QKV_B), const),
            pl.BlockSpec((SEG_COLS, SEG_COLS), const),
        ],
        out_specs=pl.BlockSpec((1, n_pairs, QKV_TILE, LANES), lambda i: (i // tps, 0, i % tps, 0)),
        out_shape=jax.ShapeDtypeStruct((batch, n_pairs, seq_len, LANES), BF16),
        scratch_shapes=[pltpu.VMEM((QKV_TILE, D_MODEL), BF16)],
        compiler_params=_params("parallel"),
        name="qkv_b",
    )(x, g.reshape(1, D_MODEL), w, gains, seg)


def _attn_a_kernel(q_ref, kp_ref, kc_ref, kn_ref, vp_ref, vc_ref, vn_ref,
                   o_ref, lse_ref, kw_ref, vw_ref, s_ref, p_ref, m_ref, *, seq_len):
    side, qb, tq = ATT_A_SIDE, ATT_A_QB, ATT_A_TILE
    kb = qb + 2 * side
    nb = tq // qb
    kw_ref[0:side] = kp_ref[0, 0]
    kw_ref[side:side + tq] = kc_ref[0, 0]
    kw_ref[side + tq:] = kn_ref[0, 0]
    ones = jnp.ones((tq + 2 * side, LANES), BF16)
    for p in range(PAIRS_PER_CHUNK):
        src = slice(p * LANES, (p + 1) * LANES)
        dst = slice(2 * p * LANES, (2 * p + 1) * LANES)
        vw_ref[0:side, dst] = vp_ref[0, 0, :, src]
        vw_ref[side:side + tq, dst] = vc_ref[0, 0, :, src]
        vw_ref[side + tq:, dst] = vn_ref[0, 0, :, src]
        vw_ref[:, (2 * p + 1) * LANES:(2 * p + 2) * LANES] = ones

    q_start = pl.program_id(2) * tq
    row = lax.broadcasted_iota(jnp.int32, (qb, kb), 0)
    col = lax.broadcasted_iota(jnp.int32, (qb, kb), 1)
    band = (col >= row) & (col - row <= 2 * side)
    biases = []
    for b in range(nb):
        kpos = q_start + b * qb + col - side
        bias = jnp.where(band & (kpos >= 0) & (kpos < seq_len), 0.0, NEG_INF).astype(F32)
        biases.append(jnp.concatenate([bias, bias], axis=0))
    bias4 = jnp.stack(biases)

    lane = lax.broadcasted_iota(jnp.int32, (tq, LANES), 1)
    q_head_a = (lane % HEAD_DIM) < (HEAD_DIM // 2)
    v_head_a = lax.broadcasted_iota(jnp.int32, (qb, LANES), 1) < HEAD_DIM
    zero = jnp.zeros((tq, LANES), BF16)

    for p in range(PAIRS_PER_CHUNK):
        cols = slice(p * LANES, (p + 1) * LANES)
        q2 = q_ref[0, 0, :, cols]
        qa = jnp.where(q_head_a, q2, zero)
        qbb = jnp.where(q_head_a, zero, q2)
        for b in range(nb):
            rows = slice(b * qb, (b + 1) * qb)
            lhs = jnp.concatenate([qa[rows], qbb[rows]], axis=0)
            s_ref[p * nb + b] = lax.dot_general(lhs, kw_ref[b * qb:b * qb + kb, cols], NT_DIMS,
                                                preferred_element_type=F32)
        blk = slice(p * nb, (p + 1) * nb)
        s = s_ref[blk] + bias4
        m = jnp.max(s, axis=-1, keepdims=True)
        p_ref[blk] = jnp.exp2(s - m).astype(BF16)
        m_ref[blk] = jnp.broadcast_to(m, (nb, 2 * qb, LANES))
        for b in range(nb):
            rows = slice(b * qb, (b + 1) * qb)
            pv = jnp.dot(p_ref[p * nb + b], vw_ref[b * qb:b * qb + kb, 2 * p * LANES:(2 * p + 2) * LANES],
                         preferred_element_type=F32)
            den = pv[:, LANES:]
            o = pv[:, :LANES] / den
            lse = m_ref[p * nb + b] + jnp.log2(den)
            o_ref[0, 0, rows, cols] = jnp.where(v_head_a, o[:qb], o[qb:]).astype(BF16)
            lse_ref[0, 0, rows, cols] = jnp.where(v_head_a, lse[:qb], lse[qb:])


def _attn_a_group(q, k, v, dilation, batch, seq_len):
    length = seq_len // dilation
    tq, side = ATT_A_TILE, ATT_A_SIDE
    n_tiles = length // tq
    halo_blocks = length // side
    nb = tq // ATT_A_QB
    kb = ATT_A_QB + 2 * side
    cur = pl.BlockSpec((1, 1, tq, OUT_A), lambda b, r, i: (b, r, i, 0))
    prev = pl.BlockSpec((1, 1, side, OUT_A),
                        lambda b, r, i: (b, r, jnp.maximum(i * (tq // side) - 1, 0), 0))
    nxt = pl.BlockSpec((1, 1, side, OUT_A),
                       lambda b, r, i: (b, r, jnp.minimum((i + 1) * (tq // side), halo_blocks - 1), 0))
    n_blk = PAIRS_PER_CHUNK * nb
    return pl.pallas_call(
        functools.partial(_attn_a_kernel, seq_len=length),
        grid=(batch, dilation, n_tiles),
        in_specs=[cur, prev, cur, nxt, prev, cur, nxt],
        out_specs=[cur, cur],
        out_shape=[jax.ShapeDtypeStruct((batch, dilation, length, OUT_A), BF16),
                   jax.ShapeDtypeStruct((batch, dilation, length, OUT_A), F32)],
        scratch_shapes=[pltpu.VMEM((tq + 2 * side, OUT_A), BF16),
                        pltpu.VMEM((tq + 2 * side, 2 * OUT_A), BF16),
                        pltpu.VMEM((n_blk, 2 * ATT_A_QB, kb), F32),
                        pltpu.VMEM((n_blk, 2 * ATT_A_QB, kb), BF16),
                        pltpu.VMEM((n_blk, 2 * ATT_A_QB, LANES), F32)],
        compiler_params=_params("parallel", "parallel", "arbitrary"),
        name=f"attn_a_d{dilation}",
    )(q, k, k, k, v, v, v)


def _merge_a_kernel(o0_ref, o1_ref, o2_ref, l0_ref, l1_ref, l2_ref, y_ref, ot_ref, lt_ref):
    for slot, (o_ref, l_ref, dilation) in enumerate(
            ((o1_ref, l1_ref, DILATED_GROUPS[1][1]), (o2_ref, l2_ref, DILATED_GROUPS[2][1]))):
        rows = TOKEN_TILE // dilation
        for r in range(dilation):
            for p in range(PAIRS_PER_CHUNK):
                cols = slice(p * LANES, (p + 1) * LANES)
                ot_ref[slot, p, pl.ds(r, rows, stride=dilation), :] = o_ref[0, r, :, cols].astype(F32)
                lt_ref[slot, p, pl.ds(r, rows, stride=dilation), :] = l_ref[0, r, :, cols]
    for p in range(PAIRS_PER_CHUNK):
        cols = slice(p * LANES, (p + 1) * LANES)
        l0, l1, l2 = l0_ref[0, 0, :, cols], lt_ref[0, p], lt_ref[1, p]
        m = jnp.maximum(jnp.maximum(l0, l1), l2)
        e0, e1, e2 = jnp.exp2(l0 - m), jnp.exp2(l1 - m), jnp.exp2(l2 - m)
        inv = 1.0 / (e0 + e1 + e2)
        y_ref[:, cols] = ((e0 * inv) * o0_ref[0, 0, :, cols].astype(F32) + (e1 * inv) * ot_ref[0, p]
                          + (e2 * inv) * ot_ref[1, p]).astype(BF16)


def _merge_a(outs, lses, batch, seq_len):
    t = batch * seq_len
    tps = seq_len // TOKEN_TILE
    grp = [pl.BlockSpec((1, d, TOKEN_TILE // d, OUT_A), lambda i: (i // tps, 0, i % tps, 0))
           for _, d in DILATED_GROUPS]
    return pl.pallas_call(
        _merge_a_kernel,
        grid=(t // TOKEN_TILE,),
        in_specs=grp + grp,
        out_specs=pl.BlockSpec((TOKEN_TILE, OUT_A), lambda i: (i, 0)),
        out_shape=jax.ShapeDtypeStruct((t, OUT_A), BF16),
        scratch_shapes=[pltpu.VMEM((N_GROUPS_A - 1, PAIRS_PER_CHUNK, TOKEN_TILE, LANES), F32),
                        pltpu.VMEM((N_GROUPS_A - 1, PAIRS_PER_CHUNK, TOKEN_TILE, LANES), F32)],
        compiler_params=_params("parallel"),
        name="merge_a",
    )(*outs, *lses)


def _attn_b_kernel(q_ref, kp_ref, kc_ref, kn_ref, vp_ref, vc_ref, vn_ref, bias_ref,
                   o_ref, kw_ref, vw_ref, s_ref, p_ref):
    w, halo, rows = GRID_W, ATT_B_HALO, ATT_B_ROWS
    n_sub = rows // ATT_B_SUB
    sub_q = ATT_B_SUB * w
    win = ATT_B_WIN * w
    ones = jnp.ones(((rows + 2 * halo) * w, LANES), BF16)
    for u in range(ATT_B_PAIRS):
        kw_ref[u, 0:halo * w] = kp_ref[0, u]
        kw_ref[u, halo * w:(halo + rows) * w] = kc_ref[0, u]
        kw_ref[u, (halo + rows) * w:] = kn_ref[0, u]
        vw_ref[u, 0:halo * w, :LANES] = vp_ref[0, u]
        vw_ref[u, halo * w:(halo + rows) * w, :LANES] = vc_ref[0, u]
        vw_ref[u, (halo + rows) * w:, :LANES] = vn_ref[0, u]
        vw_ref[u, :, LANES:] = ones

    i = pl.program_id(2)
    lo_min = jnp.where(i == 0, halo, 0)
    lo_max = jnp.where(i == pl.num_programs(2) - 1, halo, rows)
    starts = [pl.multiple_of(jnp.clip(ATT_B_SUB * sb, lo_min, lo_max) * w, LANES) for sb in range(n_sub)]
    head_a = lax.broadcasted_iota(jnp.int32, (rows * w, LANES), 1) < HEAD_DIM
    out_a = lax.broadcasted_iota(jnp.int32, (sub_q, LANES), 1) < HEAD_DIM
    zero = jnp.zeros((rows * w, LANES), BF16)
    for u in range(ATT_B_PAIRS):
        q2 = q_ref[0, u]
        qa = jnp.where(head_a, q2, zero)
        qb = jnp.where(head_a, zero, q2)
        for sb in range(n_sub):
            rs = slice(sb * sub_q, (sb + 1) * sub_q)
            lhs = jnp.concatenate([qa[rs], qb[rs]], axis=0)
            s_ref[u, sb] = lax.dot_general(lhs, kw_ref[u, pl.ds(starts[sb], win), :], NT_DIMS,
                                           preferred_element_type=F32)
    for u in range(ATT_B_PAIRS):
        s = s_ref[u] + jnp.concatenate([bias_ref[2 * u, 0], bias_ref[2 * u + 1, 0]], axis=1)
        m = jnp.max(s, axis=-1, keepdims=True)
        p_ref[u] = jnp.exp2(s - m).astype(BF16)
    for u in range(ATT_B_PAIRS):
        for sb in range(n_sub):
            pv = jnp.dot(p_ref[u, sb], vw_ref[u, pl.ds(starts[sb], win), :], preferred_element_type=F32)
            o = pv[:, :LANES] / pv[:, LANES:]
            o_ref[0, sb * sub_q:(sb + 1) * sub_q, u * LANES:(u + 1) * LANES] = (
                jnp.where(out_a, o[:sub_q], o[sub_q:]).astype(BF16))


def _attn_b(qkv, bias, batch, seq_len):
    n_rows = seq_len // GRID_W
    tile = ATT_B_ROWS * GRID_W
    halo = ATT_B_HALO * GRID_W
    n_tiles = n_rows // ATT_B_ROWS
    assert n_tiles >= 2
    per_tile = tile // halo
    halo_blocks = seq_len // halo
    n_steps = HEADS_B // 2 // ATT_B_PAIRS
    n_sub = ATT_B_ROWS // ATT_B_SUB
    sub_q = ATT_B_SUB * GRID_W
    win = ATT_B_WIN * GRID_W

    def cur(kind):
        return pl.BlockSpec((1, ATT_B_PAIRS, tile, LANES), lambda hp, b, i: (b, kind * n_steps + hp, i, 0))

    def prev(kind):
        return pl.BlockSpec((1, ATT_B_PAIRS, halo, LANES),
                            lambda hp, b, i: (b, kind * n_steps + hp, jnp.maximum(i * per_tile - 1, 0), 0))

    def nxt(kind):
        return pl.BlockSpec((1, ATT_B_PAIRS, halo, LANES),
                            lambda hp, b, i: (b, kind * n_steps + hp,
                                              jnp.minimum((i + 1) * per_tile, halo_blocks - 1), 0))

    def tile_kind(i):
        return jnp.where(i == 0, 0, jnp.where(i == n_tiles - 1, 2, 1))

    return pl.pallas_call(
        _attn_b_kernel,
        grid=(n_steps, batch, n_tiles),
        in_specs=[cur(0), prev(1), cur(1), nxt(1), prev(2), cur(2), nxt(2),
                  pl.BlockSpec((2 * ATT_B_PAIRS, 1, n_sub, sub_q, win),
                               lambda hp, b, i: (hp, tile_kind(i), 0, 0, 0))],
        out_specs=pl.BlockSpec((1, tile, ATT_B_PAIRS * LANES), lambda hp, b, i: (b, i, hp)),
        out_shape=jax.ShapeDtypeStruct((batch, seq_len, OUT_B), BF16),
        scratch_shapes=[pltpu.VMEM((ATT_B_PAIRS, tile + 2 * halo, LANES), BF16),
                        pltpu.VMEM((ATT_B_PAIRS, tile + 2 * halo, 2 * LANES), BF16),
                        pltpu.VMEM((ATT_B_PAIRS, n_sub, 2 * sub_q, win), F32),
                        pltpu.VMEM((ATT_B_PAIRS, n_sub, 2 * sub_q, win), BF16)],
        compiler_params=_params("arbitrary", "arbitrary", "arbitrary"),
        name="attn_b",
    )(qkv, qkv, qkv, qkv, qkv, qkv, qkv, bias)


def _rope_layout():
    n = np.arange(QKV_CHUNK)
    pair, lane = n // LANES, n % LANES
    second, hb, i = lane // HEAD_DIM, (lane % HEAD_DIM) // (HEAD_DIM // 2), lane % (HEAD_DIM // 2)
    head = 2 * pair + hb
    return head * HEAD_DIM + second * (HEAD_DIM // 2) + i, head


def _segment_mean_matrix(head_of_column):
    same = head_of_column[:, None] == head_of_column[None, :]
    return jnp.asarray(np.where(same, 1.0 / HEAD_DIM, 0.0), BF16)


def _rope_tables(seq_len):
    half = HEAD_DIM // 2
    inv_freq = ROPE_THETA ** (-jnp.arange(half, dtype=F32) / half)
    ang = jnp.arange(seq_len, dtype=F32)[:, None] * inv_freq[None, :]
    cos, sin = jnp.cos(ang), jnp.sin(ang)
    return (jnp.concatenate([cos, cos, cos, cos], axis=-1),
            jnp.concatenate([-sin, -sin, sin, sin], axis=-1))


def _prep_a(w_qkv, q_gain, k_gain):
    src, _ = _rope_layout()
    n_qk = 2 * N_GROUPS_A
    cols = np.concatenate([c * QKV_CHUNK + src for c in range(n_qk)]
                          + [np.arange(n_qk * QKV_CHUNK, QKV_A)])
    half = HEAD_DIM // 2

    def pair_lanes(gain):
        return jnp.concatenate([gain[:half], gain[:half], gain[half:], gain[half:]])

    n_pairs = N_GROUPS_A * PAIRS_PER_CHUNK
    gains = jnp.concatenate([jnp.tile(pair_lanes(q_gain) * Q_SCALE, n_pairs),
                             jnp.tile(pair_lanes(k_gain), n_pairs),
                             jnp.ones((QKV_A - n_qk * QKV_CHUNK,), F32)])
    return w_qkv[:, cols].astype(BF16), gains.reshape(1, QKV_A)


def _prep_b(q_gain, k_gain):
    gains = jnp.concatenate([jnp.tile(q_gain * Q_SCALE, HEADS_B),
                             jnp.tile(k_gain, HEADS_B),
                             jnp.ones((OUT_B,), F32)])
    return gains.reshape(1, QKV_B)


def _bias_tile_plan():
    halo, rows, n_sub = ATT_B_HALO, ATT_B_ROWS, ATT_B_ROWS // ATT_B_SUB
    plan = {}
    for kind, (lo_min, lo_max) in enumerate(((halo, rows), (0, rows), (0, halo))):
        for sb in range(n_sub):
            ws = min(max(ATT_B_SUB * sb, lo_min), lo_max)
            for ql in range(ATT_B_SUB):
                rho = ATT_B_SUB * sb + ql
                lo = min(max(rho, lo_min), lo_max)
                for jj in range(ATT_B_WIN // 2):
                    jl = ws + 2 * jj
                    dr = jl - rho + NA_ROWS // 2 - 1
                    plan[kind, sb, ql, jj] = (dr, lo <= jl < lo + NA_ROWS, lo <= jl + 1 < lo + NA_ROWS)
    return plan


def _bias_expand_kernel(c_ref, o_ref):
    w = GRID_W
    left = lax.broadcasted_iota(jnp.int32, (w, 2 * w), 1) < w
    neg = jnp.full((w, 2 * w), NEG_INF, F32)
    for (kind, sb, ql, jj), (dr, ok_a, ok_b) in _bias_tile_plan().items():
        if ok_a or ok_b:
            tile = c_ref[0, dr + 1]
            if not ok_b:
                tile = jnp.where(left, tile, neg)
            elif not ok_a:
                tile = jnp.where(left, neg, tile)
        else:
            tile = neg
        o_ref[0, kind, sb, ql * w:(ql + 1) * w, 2 * jj * w:2 * (jj + 1) * w] = tile


def _bias_table_b(rpb):
    w, n_sub, n_dr = GRID_W, ATT_B_ROWS // ATT_B_SUB, 2 * NA_ROWS - 1
    col = np.arange(w)
    col_start = np.clip(col - NA_COLS // 2, 0, w - NA_COLS)
    col_ok = (col[None, :] >= col_start[:, None]) & (col[None, :] < col_start[:, None] + NA_COLS)
    dc = np.clip(col[None, :] - col[:, None] + NA_COLS - 1, 0, 2 * NA_COLS - 2)
    pick = jnp.asarray(np.arange(2 * NA_COLS - 1)[:, None, None] == dc[None], F32)
    by_row = jnp.einsum("hrc,cqk->hrqk", rpb * LOG2E, pick, precision=lax.Precision.HIGHEST)
    by_row = jnp.where(col_ok[None, None], by_row, NEG_INF)
    pad = jnp.full((HEADS_B, 1, w, w), NEG_INF, F32)
    by_row = jnp.concatenate([pad, by_row, pad], axis=1)
    pairs = jnp.concatenate([by_row[:, :-1], by_row[:, 1:]], axis=-1)
    return pl.pallas_call(
        _bias_expand_kernel,
        grid=(HEADS_B,),
        in_specs=[pl.BlockSpec((1, n_dr + 1, w, 2 * w), lambda h: (h, 0, 0, 0))],
        out_specs=pl.BlockSpec((1, 3, n_sub, ATT_B_SUB * w, ATT_B_WIN * w), lambda h: (h, 0, 0, 0, 0)),
        out_shape=jax.ShapeDtypeStruct((HEADS_B, 3, n_sub, ATT_B_SUB * w, ATT_B_WIN * w), F32),
        compiler_params=_params("parallel"),
        name="bias_expand",
    )(pairs)


def _mixer_a(x, batch, seq_len, mix_g, w_qkv, gains, seg, rope_tables):
    qkv = _qkv_a(x, mix_g, w_qkv, gains, seg, rope_tables, batch, seq_len)
    outs, lses = [], []
    for group, (window, dilation) in enumerate(DILATED_GROUPS):
        assert window // (2 * dilation) == ATT_A_SIDE
        o, lse = _attn_a_group(qkv[group], qkv[N_GROUPS_A + group], qkv[2 * N_GROUPS_A + group],
                               dilation, batch, seq_len)
        outs.append(o)
        lses.append(lse)
    return _merge_a(outs, lses, batch, seq_len)


def _mixer_b(x, batch, seq_len, mix_g, w_qkv, gains, bias, seg):
    qkv = _qkv_b(x, mix_g, w_qkv, gains, seg, batch, seq_len)
    return _attn_b(qkv, bias, batch, seq_len).reshape(batch * seq_len, OUT_B)


def kernel(x_prompt, x_sample, ffn1_norm, ffn1_w_in, ffn1_w_out, mix_norm, ffn2_norm, ffn2_w_in, ffn2_w_out, a_w_qkv, a_q_norm, a_k_norm, a_w_o, b_w_qkv, b_q_norm, b_k_norm, b_rpb, b_w_o):
    depth = ffn1_norm.shape[0]
    bf = lambda w: w.astype(BF16)
    ffn1_w_in, ffn1_w_out, ffn2_w_in, ffn2_w_out = map(bf, (ffn1_w_in, ffn1_w_out, ffn2_w_in, ffn2_w_out))
    a_w_o, b_w_qkv, b_w_o = map(bf, (a_w_o, b_w_qkv, b_w_o))
    seg_a = _segment_mean_matrix(_rope_layout()[1][:SEG_COLS])
    seg_b = _segment_mean_matrix(np.arange(SEG_COLS) // HEAD_DIM)
    prep_a = [_prep_a(a_w_qkv[j], a_q_norm[j], a_k_norm[j]) for j in range(a_w_qkv.shape[0])]
    prep_b = [(_prep_b(b_q_norm[j], b_k_norm[j]), _bias_table_b(b_rpb[j])) for j in range(b_w_qkv.shape[0])]
    rope_tables = _rope_tables(max(x_prompt.shape[1], x_sample.shape[1]))

    def trunk(x3):
        batch, seq_len, _ = x3.shape
        x = x3.reshape(batch * seq_len, D_MODEL)
        for i in range(depth):
            j = i // 2
            x = _ffn(x, ffn1_norm[i], ffn1_w_in[i], ffn1_w_out[i])
            if i % 2 == 0:
                w_qkv, gains = prep_a[j]
                attn = (_mixer_a(x, batch, seq_len, mix_norm[i], w_qkv, gains, seg_a, rope_tables), a_w_o[j])
            else:
                gains, bias = prep_b[j]
                attn = (_mixer_b(x, batch, seq_len, mix_norm[i], b_w_qkv[j], gains, bias, seg_b), b_w_o[j])
            x = _ffn(x, ffn2_norm[i], ffn2_w_in[i], ffn2_w_out[i], attn)
        return x.reshape(batch, seq_len, D_MODEL)

    return (trunk(x_prompt), trunk(x_sample))
```

```python
import functools

import jax
import jax.numpy as jnp
import numpy as np
from jax import lax
from jax.experimental import pallas as pl
from jax.experimental.pallas import tpu as pltpu

D_MODEL = 1024
HEAD_DIM = 64
D_FF = 2816
RMS_EPS = 1e-6
ROPE_THETA = 10000.0
NEG_INF = -1e30
LOG2E = 1.4426950408889634
DILATED_GROUPS = ((128, 1), (512, 4), (2048, 16))
HEADS_A = 8
N_GROUPS_A = 3
QKV_A = 3 * N_GROUPS_A * HEADS_A * HEAD_DIM
OUT_A = HEADS_A * HEAD_DIM
HEADS_B = 16
QKV_B = 3 * HEADS_B * HEAD_DIM
OUT_B = HEADS_B * HEAD_DIM
GRID_W = 64
NA_ROWS = 8
NA_COLS = 16

LANES = 128
VMEM_LIMIT_BYTES = 56 * 1024 * 1024

TOKEN_TILE = 1024
QKV_TILE = 1024
FFN_TILE = 1024
FF_CHUNK = 512
SEG_COLS = 256
QKV_CHUNK = 512
PAIRS_PER_CHUNK = QKV_CHUNK // LANES
ATT_A_TILE = 1024
ATT_A_QB = 128
ATT_A_SIDE = 64
ATT_B_ROWS = 8
ATT_B_HALO = 4
ATT_B_PAIRS = 4
ATT_B_SUB = 2
ATT_B_WIN = NA_ROWS + ATT_B_SUB

BF16 = jnp.bfloat16
F32 = jnp.float32
NT_DIMS = (((1,), (1,)), ((), ()))
Q_SCALE = HEAD_DIM ** -0.5 * LOG2E


def _params(*semantics):
    return pltpu.CompilerParams(dimension_semantics=semantics,
                                vmem_limit_bytes=VMEM_LIMIT_BYTES)


def _rms_rows(x, g):
    ms = jnp.mean(x * x, axis=-1, keepdims=True)
    return (x * lax.rsqrt(ms + RMS_EPS)) * g


def _ffn_kernel(*refs, fused_proj):
    if fused_proj:
        x_ref, a_ref, wa_ref, g_ref, win_ref, wout_ref, o_ref, hn_ref = refs
        o_ref[...] = x_ref[...] + jnp.dot(a_ref[...], wa_ref[...], preferred_element_type=F32)
    else:
        x_ref, g_ref, win_ref, wout_ref, o_ref, hn_ref = refs
        o_ref[...] = x_ref[...]
    hn_ref[...] = _rms_rows(o_ref[...], g_ref[...]).astype(BF16)
    for c0 in range(0, D_FF, FF_CHUNK):
        width = min(FF_CHUNK, D_FF - c0)
        h = hn_ref[...]
        gate = jnp.dot(h, win_ref[:, c0:c0 + width], preferred_element_type=F32)
        up = jnp.dot(h, win_ref[:, D_FF + c0:D_FF + c0 + width], preferred_element_type=F32)
        a = (gate * jax.nn.sigmoid(gate) * up).astype(BF16)
        o_ref[...] += 0.5 * jnp.dot(a, wout_ref[c0:c0 + width, :], preferred_element_type=F32)


def _ffn(x, g, w_in, w_out, attn=None):
    t = x.shape[0]
    const = lambda i: (0, 0)
    resident = pl.Buffered(1)
    row = pl.BlockSpec((FFN_TILE, D_MODEL), lambda i: (i, 0))
    in_specs, args = [row], [x]
    if attn is not None:
        a, w_a = attn
        in_specs += [pl.BlockSpec((FFN_TILE, a.shape[1]), lambda i: (i, 0)),
                     pl.BlockSpec(w_a.shape, const, pipeline_mode=resident)]
        args += [a, w_a]
    in_specs += [pl.BlockSpec((1, D_MODEL), const),
                 pl.BlockSpec((D_MODEL, 2 * D_FF), const, pipeline_mode=resident),
                 pl.BlockSpec((D_FF, D_MODEL), const, pipeline_mode=resident)]
    args += [g.reshape(1, D_MODEL), w_in, w_out]
    return pl.pallas_call(
        functools.partial(_ffn_kernel, fused_proj=attn is not None),
        grid=(t // FFN_TILE,),
        in_specs=in_specs,
        out_specs=row,
        out_shape=jax.ShapeDtypeStruct((t, D_MODEL), F32),
        scratch_shapes=[pltpu.VMEM((FFN_TILE, D_MODEL), BF16)],
        compiler_params=_params("parallel"),
        name="ffn_proj" if attn is not None else "ffn",
    )(*args)


def _project_chunk(hn_ref, w_ref, gain_ref, seg_ref, c, normed):
    cols = slice(c * QKV_CHUNK, (c + 1) * QKV_CHUNK)
    y = jnp.dot(hn_ref[...], w_ref[:, cols], preferred_element_type=F32)
    if normed:
        y2 = (y * y).astype(BF16)
        ms = jnp.concatenate(
            [jnp.dot(y2[:, s0:s0 + SEG_COLS], seg_ref[...], preferred_element_type=F32)
             for s0 in range(0, QKV_CHUNK, SEG_COLS)], axis=-1)
        y = (y * lax.rsqrt(ms + RMS_EPS)) * gain_ref[:, cols]
    return y


def _qkv_a_kernel(x_ref, g_ref, w_ref, gain_ref, seg_ref, cos_ref, sin_ref, *rest):
    n_chunks = 3 * N_GROUPS_A
    out_refs, (hn_ref, ys_ref) = rest[:n_chunks], rest[n_chunks:]
    hn_ref[...] = _rms_rows(x_ref[...], g_ref[...]).astype(BF16)
    cos, sin = cos_ref[...], sin_ref[...]
    for c in range(n_chunks):
        kind, group = divmod(c, N_GROUPS_A)
        dilation = DILATED_GROUPS[group][1]
        normed = kind < 2
        y = _project_chunk(hn_ref, w_ref, gain_ref, seg_ref, c, normed)
        if normed:
            y = jnp.concatenate(
                [y[:, p * LANES:(p + 1) * LANES] * cos
                 + pltpu.roll(y[:, p * LANES:(p + 1) * LANES], LANES // 2, axis=1) * sin
                 for p in range(PAIRS_PER_CHUNK)], axis=-1)
        if dilation == 1:
            out_refs[c][0, 0] = y.astype(BF16)
        else:
            slot = kind * (N_GROUPS_A - 1) + group - 1
            rows = QKV_TILE // dilation
            for p in range(PAIRS_PER_CHUNK):
                cols = slice(p * LANES, (p + 1) * LANES)
                ys_ref[slot, p] = y[:, cols]
                for r in range(dilation):
                    out_refs[c][0, r, :, cols] = ys_ref[slot, p, pl.ds(r, rows, stride=dilation), :].astype(BF16)


def _qkv_a(x, g, w, gains, seg, rope_tables, batch, seq_len):
    t = x.shape[0]
    tps = seq_len // QKV_TILE
    const = lambda i: (0, 0)
    in_specs = [
        pl.BlockSpec((QKV_TILE, D_MODEL), lambda i: (i, 0)),
        pl.BlockSpec((1, D_MODEL), const),
        pl.BlockSpec((D_MODEL, QKV_A), const, pipeline_mode=pl.Buffered(1)),
        pl.BlockSpec((1, QKV_A), const),
        pl.BlockSpec((SEG_COLS, SEG_COLS), const),
        pl.BlockSpec((QKV_TILE, LANES), lambda i: (i % tps, 0)),
        pl.BlockSpec((QKV_TILE, LANES), lambda i: (i % tps, 0)),
    ]
    out_specs, out_shapes = [], []
    for c in range(3 * N_GROUPS_A):
        d = DILATED_GROUPS[c % N_GROUPS_A][1]
        out_specs.append(pl.BlockSpec((1, d, QKV_TILE // d, QKV_CHUNK),
                                      lambda i: (i // tps, 0, i % tps, 0)))
        out_shapes.append(jax.ShapeDtypeStruct((batch, d, seq_len // d, QKV_CHUNK), BF16))
    return pl.pallas_call(
        _qkv_a_kernel,
        grid=(t // QKV_TILE,),
        in_specs=in_specs,
        out_specs=out_specs,
        out_shape=out_shapes,
        scratch_shapes=[pltpu.VMEM((QKV_TILE, D_MODEL), BF16),
                        pltpu.VMEM((3 * (N_GROUPS_A - 1), PAIRS_PER_CHUNK, QKV_TILE, LANES), F32)],
        compiler_params=_params("parallel"),
        name="qkv_a",
    )(x, g.reshape(1, D_MODEL), w, gains, seg, *rope_tables)


def _qkv_b_kernel(x_ref, g_ref, w_ref, gain_ref, seg_ref, o_ref, hn_ref):
    hn_ref[...] = _rms_rows(x_ref[...], g_ref[...]).astype(BF16)
    for c in range(QKV_B // QKV_CHUNK):
        y = _project_chunk(hn_ref, w_ref, gain_ref, seg_ref, c, c < 2 * OUT_B // QKV_CHUNK)
        for p in range(PAIRS_PER_CHUNK):
            o_ref[0, c * PAIRS_PER_CHUNK + p] = y[:, p * LANES:(p + 1) * LANES].astype(BF16)


def _qkv_b(x, g, w, gains, seg, batch, seq_len):
    t = x.shape[0]
    tps = seq_len // QKV_TILE
    n_pairs = QKV_B // LANES
    const = lambda i: (0, 0)
    return pl.pallas_call(
        _qkv_b_kernel,
        grid=(t // QKV_TILE,),
        in_specs=[
            pl.BlockSpec((QKV_TILE, D_MODEL), lambda i: (i, 0)),
            pl.BlockSpec((1, D_MODEL), const),
            pl.BlockSpec((D_MODEL, QKV_B), const, pipeline_mode=pl.Buffered(1)),
            pl.BlockSpec((1, QKV_B), const),
            pl.BlockSpec((SEG_COLS, SEG_COLS), const),
        ],
        out_specs=pl.BlockSpec((1, n_pairs, QKV_TILE, LANES), lambda i: (i // tps, 0, i % tps, 0)),
        out_shape=jax.ShapeDtypeStruct((batch, n_pairs, seq_len, LANES), BF16),
        scratch_shapes=[pltpu.VMEM((QKV_TILE, D_MODEL), BF16)],
        compiler_params=_params("parallel"),
        name="qkv_b",
    )(x, g.reshape(1, D_MODEL), w, gains, seg)


def _attn_a_kernel(q_ref, kp_ref, kc_ref, kn_ref, vp_ref, vc_ref, vn_ref,
                   o_ref, lse_ref, kw_ref, vw_ref, s_ref, p_ref, m_ref, *, seq_len, tq):
    side, qb = ATT_A_SIDE, ATT_A_QB
    kb = qb + 2 * side
    nb = tq // qb
    kw_ref[0:side] = kp_ref[0, 0]
    kw_ref[side:side + tq] = kc_ref[0, 0]
    kw_ref[side + tq:] = kn_ref[0, 0]
    ones = jnp.ones((tq + 2 * side, LANES), BF16)
    for p in range(PAIRS_PER_CHUNK):
        src = slice(p * LANES, (p + 1) * LANES)
        dst = slice(2 * p * LANES, (2 * p + 1) * LANES)
        vw_ref[0:side, dst] = vp_ref[0, 0, :, src]
        vw_ref[side:side + tq, dst] = vc_ref[0, 0, :, src]
        vw_ref[side + tq:, dst] = vn_ref[0, 0, :, src]
        vw_ref[:, (2 * p + 1) * LANES:(2 * p + 2) * LANES] = ones

    q_start = pl.program_id(2) * tq
    row = lax.broadcasted_iota(jnp.int32, (qb, kb), 0)
    col = lax.broadcasted_iota(jnp.int32, (qb, kb), 1)
    band = (col >= row) & (col - row <= 2 * side)
    biases = []
    for b in range(nb):
        kpos = q_start + b * qb + col - side
        bias = jnp.where(band & (kpos >= 0) & (kpos < seq_len), 0.0, NEG_INF).astype(F32)
        biases.append(jnp.concatenate([bias, bias], axis=0))
    bias4 = jnp.stack(biases)

    lane = lax.broadcasted_iota(jnp.int32, (tq, LANES), 1)
    q_head_a = (lane % HEAD_DIM) < (HEAD_DIM // 2)
    v_head_a = lax.broadcasted_iota(jnp.int32, (qb, LANES), 1) < HEAD_DIM
    zero = jnp.zeros((tq, LANES), BF16)

    for p in range(PAIRS_PER_CHUNK):
        cols = slice(p * LANES, (p + 1) * LANES)
        q2 = q_ref[0, 0, :, cols]
        qa = jnp.where(q_head_a, q2, zero)
        qbb = jnp.where(q_head_a, zero, q2)
        for b in range(nb):
            rows = slice(b * qb, (b + 1) * qb)
            lhs = jnp.concatenate([qa[rows], qbb[rows]], axis=0)
            s_ref[p * nb + b] = lax.dot_general(lhs, kw_ref[b * qb:b * qb + kb, cols], NT_DIMS,
                                                preferred_element_type=F32)
        blk = slice(p * nb, (p + 1) * nb)
        s = s_ref[blk] + bias4
        m = jnp.max(s, axis=-1, keepdims=True)
        p_ref[blk] = jnp.exp2(s - m).astype(BF16)
        m_ref[blk] = jnp.broadcast_to(m, (nb, 2 * qb, LANES))
        for b in range(nb):
            rows = slice(b * qb, (b + 1) * qb)
            pv = jnp.dot(p_ref[p * nb + b], vw_ref[b * qb:b * qb + kb, 2 * p * LANES:(2 * p + 2) * LANES],
                         preferred_element_type=F32)
            den = pv[:, LANES:]
            o = pv[:, :LANES] / den
            lse = m_ref[p * nb + b] + jnp.log2(den)
            o_ref[0, 0, rows, cols] = jnp.where(v_head_a, o[:qb], o[qb:]).astype(BF16)
            lse_ref[0, 0, rows, cols] = jnp.where(v_head_a, lse[:qb], lse[qb:])


def _attn_a_group(q, k, v, dilation, batch, seq_len):
    length = seq_len // dilation
    tq, side = min(ATT_A_TILE, length), ATT_A_SIDE
    n_tiles = length // tq
    halo_blocks = length // side
    nb = tq // ATT_A_QB
    kb = ATT_A_QB + 2 * side
    cur = pl.BlockSpec((1, 1, tq, OUT_A), lambda b, r, i: (b, r, i, 0))
    prev = pl.BlockSpec((1, 1, side, OUT_A),
                        lambda b, r, i: (b, r, jnp.maximum(i * (tq // side) - 1, 0), 0))
    nxt = pl.BlockSpec((1, 1, side, OUT_A),
                       lambda b, r, i: (b, r, jnp.minimum((i + 1) * (tq // side), halo_blocks - 1), 0))
    n_blk = PAIRS_PER_CHUNK * nb
    return pl.pallas_call(
        functools.partial(_attn_a_kernel, seq_len=length, tq=tq),
        grid=(batch, dilation, n_tiles),
        in_specs=[cur, prev, cur, nxt, prev, cur, nxt],
        out_specs=[cur, cur],
        out_shape=[jax.ShapeDtypeStruct((batch, dilation, length, OUT_A), BF16),
                   jax.ShapeDtypeStruct((batch, dilation, length, OUT_A), F32)],
        scratch_shapes=[pltpu.VMEM((tq + 2 * side, OUT_A), BF16),
                        pltpu.VMEM((tq + 2 * side, 2 * OUT_A), BF16),
                        pltpu.VMEM((n_blk, 2 * ATT_A_QB, kb), F32),
                        pltpu.VMEM((n_blk, 2 * ATT_A_QB, kb), BF16),
                        pltpu.VMEM((n_blk, 2 * ATT_A_QB, LANES), F32)],
        compiler_params=_params("parallel", "parallel", "arbitrary"),
        name=f"attn_a_d{dilation}",
    )(q, k, k, k, v, v, v)


def _merge_a_kernel(o0_ref, o1_ref, o2_ref, l0_ref, l1_ref, l2_ref, y_ref, ot_ref, lt_ref):
    for slot, (o_ref, l_ref, dilation) in enumerate(
            ((o1_ref, l1_ref, DILATED_GROUPS[1][1]), (o2_ref, l2_ref, DILATED_GROUPS[2][1]))):
        rows = TOKEN_TILE // dilation
        for r in range(dilation):
            for p in range(PAIRS_PER_CHUNK):
                cols = slice(p * LANES, (p + 1) * LANES)
                ot_ref[slot, p, pl.ds(r, rows, stride=dilation), :] = o_ref[0, r, :, cols].astype(F32)
                lt_ref[slot, p, pl.ds(r, rows, stride=dilation), :] = l_ref[0, r, :, cols]
    for p in range(PAIRS_PER_CHUNK):
        cols = slice(p * LANES, (p + 1) * LANES)
        l0, l1, l2 = l0_ref[0, 0, :, cols], lt_ref[0, p], lt_ref[1, p]
        m = jnp.maximum(jnp.maximum(l0, l1), l2)
        e0, e1, e2 = jnp.exp2(l0 - m), jnp.exp2(l1 - m), jnp.exp2(l2 - m)
        inv = 1.0 / (e0 + e1 + e2)
        y_ref[:, cols] = ((e0 * inv) * o0_ref[0, 0, :, cols].astype(F32) + (e1 * inv) * ot_ref[0, p]
                          + (e2 * inv) * ot_ref[1, p]).astype(BF16)


def _merge_a(outs, lses, batch, seq_len):
    t = batch * seq_len
    tps = seq_len // TOKEN_TILE
    grp = [pl.BlockSpec((1, d, TOKEN_TILE // d, OUT_A), lambda i: (i // tps, 0, i % tps, 0))
           for _, d in DILATED_GROUPS]
    return pl.pallas_call(
        _merge_a_kernel,
        grid=(t // TOKEN_TILE,),
        in_specs=grp + grp,
        out_specs=pl.BlockSpec((TOKEN_TILE, OUT_A), lambda i: (i, 0)),
        out_shape=jax.ShapeDtypeStruct((t, OUT_A), BF16),
        scratch_shapes=[pltpu.VMEM((N_GROUPS_A - 1, PAIRS_PER_CHUNK, TOKEN_TILE, LANES), F32),
                        pltpu.VMEM((N_GROUPS_A - 1, PAIRS_PER_CHUNK, TOKEN_TILE, LANES), F32)],
        compiler_params=_params("parallel"),
        name="merge_a",
    )(*outs, *lses)


def _attn_b_kernel(q_ref, kp_ref, kc_ref, kn_ref, vp_ref, vc_ref, vn_ref, bias_ref,
                   o_ref, kw_ref, vw_ref, s_ref, p_ref):
    w, halo, rows = GRID_W, ATT_B_HALO, ATT_B_ROWS
    n_sub = rows // ATT_B_SUB
    sub_q = ATT_B_SUB * w
    win = ATT_B_WIN * w
    ones = jnp.ones(((rows + 2 * halo) * w, LANES), BF16)
    for u in range(ATT_B_PAIRS):
        kw_ref[u, 0:halo * w] = kp_ref[0, u]
        kw_ref[u, halo * w:(halo + rows) * w] = kc_ref[0, u]
        kw_ref[u, (halo + rows) * w:] = kn_ref[0, u]
        vw_ref[u, 0:halo * w, :LANES] = vp_ref[0, u]
        vw_ref[u, halo * w:(halo + rows) * w, :LANES] = vc_ref[0, u]
        vw_ref[u, (halo + rows) * w:, :LANES] = vn_ref[0, u]
        vw_ref[u, :, LANES:] = ones

    i = pl.program_id(2)
    lo_min = jnp.where(i == 0, halo, 0)
    lo_max = jnp.where(i == pl.num_programs(2) - 1, halo, rows)
    starts = [pl.multiple_of(jnp.clip(ATT_B_SUB * sb, lo_min, lo_max) * w, LANES) for sb in range(n_sub)]
    head_a = lax.broadcasted_iota(jnp.int32, (rows * w, LANES), 1) < HEAD_DIM
    out_a = lax.broadcasted_iota(jnp.int32, (sub_q, LANES), 1) < HEAD_DIM
    zero = jnp.zeros((rows * w, LANES), BF16)
    for u in range(ATT_B_PAIRS):
        q2 = q_ref[0, u]
        qa = jnp.where(head_a, q2, zero)
        qb = jnp.where(head_a, zero, q2)
        for sb in range(n_sub):
            rs = slice(sb * sub_q, (sb + 1) * sub_q)
            lhs = jnp.concatenate([qa[rs], qb[rs]], axis=0)
            s_ref[u, sb] = lax.dot_general(lhs, kw_ref[u, pl.ds(starts[sb], win), :], NT_DIMS,
                                           preferred_element_type=F32)
    for u in range(ATT_B_PAIRS):
        s = s_ref[u] + jnp.concatenate([bias_ref[2 * u, 0], bias_ref[2 * u + 1, 0]], axis=1)
        m = jnp.max(s, axis=-1, keepdims=True)
        p_ref[u] = jnp.exp2(s - m).astype(BF16)
    for u in range(ATT_B_PAIRS):
        for sb in range(n_sub):
            pv = jnp.dot(p_ref[u, sb], vw_ref[u, pl.ds(starts[sb], win), :], preferred_element_type=F32)
            o = pv[:, :LANES] / pv[:, LANES:]
            o_ref[0, sb * sub_q:(sb + 1) * sub_q, u * LANES:(u + 1) * LANES] = (
                jnp.where(out_a, o[:sub_q], o[sub_q:]).astype(BF16))


def _attn_b(qkv, bias, batch, seq_len):
    n_rows = seq_len // GRID_W
    tile = ATT_B_ROWS * GRID_W
    halo = ATT_B_HALO * GRID_W
    n_tiles = n_rows // ATT_B_ROWS
    assert n_tiles >= 2
    per_tile = tile // halo
    halo_blocks = seq_len // halo
    n_steps = HEADS_B // 2 // ATT_B_PAIRS
    n_sub = ATT_B_ROWS // ATT_B_SUB
    sub_q = ATT_B_SUB * GRID_W
    win = ATT_B_WIN * GRID_W

    def cur(kind):
        return pl.BlockSpec((1, ATT_B_PAIRS, tile, LANES), lambda hp, b, i: (b, kind * n_steps + hp, i, 0))

    def prev(kind):
        return pl.BlockSpec((1, ATT_B_PAIRS, halo, LANES),
                            lambda hp, b, i: (b, kind * n_steps + hp, jnp.maximum(i * per_tile - 1, 0), 0))

    def nxt(kind):
        return pl.BlockSpec((1, ATT_B_PAIRS, halo, LANES),
                            lambda hp, b, i: (b, kind * n_steps + hp,
                                              jnp.minimum((i + 1) * per_tile, halo_blocks - 1), 0))

    def tile_kind(i):
        return jnp.where(i == 0, 0, jnp.where(i == n_tiles - 1, 2, 1))

    return pl.pallas_call(
        _attn_b_kernel,
        grid=(n_steps, batch, n_tiles),
        in_specs=[cur(0), prev(1), cur(1), nxt(1), prev(2), cur(2), nxt(2),
                  pl.BlockSpec((2 * ATT_B_PAIRS, 1, n_sub, sub_q, win),
                               lambda hp, b, i: (hp, tile_kind(i), 0, 0, 0))],
        out_specs=pl.BlockSpec((1, tile, ATT_B_PAIRS * LANES), lambda hp, b, i: (b, i, hp)),
        out_shape=jax.ShapeDtypeStruct((batch, seq_len, OUT_B), BF16),
        scratch_shapes=[pltpu.VMEM((ATT_B_PAIRS, tile + 2 * halo, LANES), BF16),
                        pltpu.VMEM((ATT_B_PAIRS, tile + 2 * halo, 2 * LANES), BF16),
                        pltpu.VMEM((ATT_B_PAIRS, n_sub, 2 * sub_q, win), F32),
                        pltpu.VMEM((ATT_B_PAIRS, n_sub, 2 * sub_q, win), BF16)],
        compiler_params=_params("arbitrary", "arbitrary", "arbitrary"),
        name="attn_b",
    )(qkv, qkv, qkv, qkv, qkv, qkv, qkv, bias)


def _rope_layout():
    n = np.arange(QKV_CHUNK)
    pair, lane = n // LANES, n % LANES
    second, hb, i = lane // HEAD_DIM, (lane % HEAD_DIM) // (HEAD_DIM // 2), lane % (HEAD_DIM // 2)
    head = 2 * pair + hb
    return head * HEAD_DIM + second * (HEAD_DIM // 2) + i, head


def _segment_mean_matrix(head_of_column):
    same = head_of_column[:, None] == head_of_column[None, :]
    return jnp.asarray(np.where(same, 1.0 / HEAD_DIM, 0.0), BF16)


def _rope_tables(seq_len):
    half = HEAD_DIM // 2
    inv_freq = np.float32(ROPE_THETA) ** (-np.arange(half, dtype=np.float32) / np.float32(half))
    ang = (np.arange(seq_len, dtype=np.float32)[:, None] * inv_freq[None, :]).astype(np.float64)
    cos, sin = jnp.asarray(np.cos(ang), F32), jnp.asarray(np.sin(ang), F32)
    return (jnp.concatenate([cos, cos, cos, cos], axis=-1),
            jnp.concatenate([-sin, -sin, sin, sin], axis=-1))


def _prep_a(w_qkv, q_gain, k_gain):
    src, _ = _rope_layout()
    n_qk = 2 * N_GROUPS_A
    cols = np.concatenate([c * QKV_CHUNK + src for c in range(n_qk)]
                          + [np.arange(n_qk * QKV_CHUNK, QKV_A)])
    half = HEAD_DIM // 2

    def pair_lanes(gain):
        return jnp.concatenate([gain[:half], gain[:half], gain[half:], gain[half:]])

    n_pairs = N_GROUPS_A * PAIRS_PER_CHUNK
    gains = jnp.concatenate([jnp.tile(pair_lanes(q_gain) * Q_SCALE, n_pairs),
                             jnp.tile(pair_lanes(k_gain), n_pairs),
                             jnp.ones((QKV_A - n_qk * QKV_CHUNK,), F32)])
    return w_qkv[:, cols].astype(BF16), gains.reshape(1, QKV_A)


def _prep_b(q_gain, k_gain):
    gains = jnp.concatenate([jnp.tile(q_gain * Q_SCALE, HEADS_B),
                             jnp.tile(k_gain, HEADS_B),
                             jnp.ones((OUT_B,), F32)])
    return gains.reshape(1, QKV_B)


def _bias_tile_plan():
    halo, rows, n_sub = ATT_B_HALO, ATT_B_ROWS, ATT_B_ROWS // ATT_B_SUB
    plan = {}
    for kind, (lo_min, lo_max) in enumerate(((halo, rows), (0, rows), (0, halo))):
        for sb in range(n_sub):
            ws = min(max(ATT_B_SUB * sb, lo_min), lo_max)
            for ql in range(ATT_B_SUB):
                rho = ATT_B_SUB * sb + ql
                lo = min(max(rho, lo_min), lo_max)
                for jj in range(ATT_B_WIN // 2):
                    jl = ws + 2 * jj
                    dr = jl - rho + NA_ROWS // 2 - 1
                    plan[kind, sb, ql, jj] = (dr, lo <= jl < lo + NA_ROWS, lo <= jl + 1 < lo + NA_ROWS)
    return plan


def _bias_expand_kernel(c_ref, o_ref):
    w = GRID_W
    left = lax.broadcasted_iota(jnp.int32, (w, 2 * w), 1) < w
    neg = jnp.full((w, 2 * w), NEG_INF, F32)
    for (kind, sb, ql, jj), (dr, ok_a, ok_b) in _bias_tile_plan().items():
        if ok_a or ok_b:
            tile = c_ref[0, dr + 1]
            if not ok_b:
                tile = jnp.where(left, tile, neg)
            elif not ok_a:
                tile = jnp.where(left, neg, tile)
        else:
            tile = neg
        o_ref[0, kind, sb, ql * w:(ql + 1) * w, 2 * jj * w:2 * (jj + 1) * w] = tile


def _bias_table_b(rpb):
    w, n_sub, n_dr = GRID_W, ATT_B_ROWS // ATT_B_SUB, 2 * NA_ROWS - 1
    col = np.arange(w)
    col_start = np.clip(col - NA_COLS // 2, 0, w - NA_COLS)
    col_ok = (col[None, :] >= col_start[:, None]) & (col[None, :] < col_start[:, None] + NA_COLS)
    dc = np.clip(col[None, :] - col[:, None] + NA_COLS - 1, 0, 2 * NA_COLS - 2)
    pick = jnp.asarray(np.arange(2 * NA_COLS - 1)[:, None, None] == dc[None], F32)
    by_row = jnp.einsum("hrc,cqk->hrqk", rpb * LOG2E, pick, precision=lax.Precision.HIGHEST)
    by_row = jnp.where(col_ok[None, None], by_row, NEG_INF)
    pad = jnp.full((HEADS_B, 1, w, w), NEG_INF, F32)
    by_row = jnp.concatenate([pad, by_row, pad], axis=1)
    pairs = jnp.concatenate([by_row[:, :-1], by_row[:, 1:]], axis=-1)
    return pl.pallas_call(
        _bias_expand_kernel,
        grid=(HEADS_B,),
        in_specs=[pl.BlockSpec((1, n_dr + 1, w, 2 * w), lambda h: (h, 0, 0, 0))],
        out_specs=pl.BlockSpec((1, 3, n_sub, ATT_B_SUB * w, ATT_B_WIN * w), lambda h: (h, 0, 0, 0, 0)),
        out_shape=jax.ShapeDtypeStruct((HEADS_B, 3, n_sub, ATT_B_SUB * w, ATT_B_WIN * w), F32),
        compiler_params=_params("parallel"),
        name="bias_expand",
    )(pairs)


def _mixer_a(x, batch, seq_len, mix_g, w_qkv, gains, seg, rope_tables):
    qkv = _qkv_a(x, mix_g, w_qkv, gains, seg, rope_tables, batch, seq_len)
    outs, lses = [], []
    for group, (window, dilation) in enumerate(DILATED_GROUPS):
        assert window // (2 * dilation) == ATT_A_SIDE
        o, lse = _attn_a_group(qkv[group], qkv[N_GROUPS_A + group], qkv[2 * N_GROUPS_A + group],
                               dilation, batch, seq_len)
        outs.append(o)
        lses.append(lse)
    return _merge_a(outs, lses, batch, seq_len)


def _mixer_b(x, batch, seq_len, mix_g, w_qkv, gains, bias, seg):
    qkv = _qkv_b(x, mix_g, w_qkv, gains, seg, batch, seq_len)
    return _attn_b(qkv, bias, batch, seq_len).reshape(batch * seq_len, OUT_B)


def kernel(x_prompt, x_sample, ffn1_norm, ffn1_w_in, ffn1_w_out, mix_norm, ffn2_norm, ffn2_w_in, ffn2_w_out, a_w_qkv, a_q_norm, a_k_norm, a_w_o, b_w_qkv, b_q_norm, b_k_norm, b_rpb, b_w_o):
    depth = ffn1_norm.shape[0]
    bf = lambda w: w.astype(BF16)
    ffn1_w_in, ffn1_w_out, ffn2_w_in, ffn2_w_out = map(bf, (ffn1_w_in, ffn1_w_out, ffn2_w_in, ffn2_w_out))
    a_w_o, b_w_qkv, b_w_o = map(bf, (a_w_o, b_w_qkv, b_w_o))
    seg_a = _segment_mean_matrix(_rope_layout()[1][:SEG_COLS])
    seg_b = _segment_mean_matrix(np.arange(SEG_COLS) // HEAD_DIM)
    prep_a = [_prep_a(a_w_qkv[j], a_q_norm[j], a_k_norm[j]) for j in range(a_w_qkv.shape[0])]
    prep_b = [(_prep_b(b_q_norm[j], b_k_norm[j]), _bias_table_b(b_rpb[j])) for j in range(b_w_qkv.shape[0])]
    rope_tables = _rope_tables(max(x_prompt.shape[1], x_sample.shape[1]))

    def trunk(x3):
        batch, seq_len, _ = x3.shape
        x = x3.reshape(batch * seq_len, D_MODEL)
        for i in range(depth):
            j = i // 2
            x = _ffn(x, ffn1_norm[i], ffn1_w_in[i], ffn1_w_out[i])
            if i % 2 == 0:
                w_qkv, gains = prep_a[j]
                attn = (_mixer_a(x, batch, seq_len, mix_norm[i], w_qkv, gains, seg_a, rope_tables), a_w_o[j])
            else:
                gains, bias = prep_b[j]
                attn = (_mixer_b(x, batch, seq_len, mix_norm[i], b_w_qkv[j], gains, bias, seg_b), b_w_o[j])
            x = _ffn(x, ffn2_norm[i], ffn2_w_in[i], ffn2_w_out[i], attn)
        return x.reshape(batch, seq_len, D_MODEL)

    return (trunk(x_prompt), trunk(x_sample))
```

```python
import functools

import jax
import jax.numpy as jnp
import numpy as np
from jax import lax
from jax.experimental import pallas as pl
from jax.experimental.pallas import tpu as pltpu

D_MODEL = 1024
HEAD_DIM = 64
D_FF = 2816
RMS_EPS = 1e-6
ROPE_THETA = 10000.0
NEG_INF = -1e30
LOG2E = 1.4426950408889634
DILATED_GROUPS = ((128, 1), (512, 4), (2048, 16))
HEADS_A = 8
N_GROUPS_A = 3
QKV_A = 3 * N_GROUPS_A * HEADS_A * HEAD_DIM
OUT_A = HEADS_A * HEAD_DIM
HEADS_B = 16
QKV_B = 3 * HEADS_B * HEAD_DIM
OUT_B = HEADS_B * HEAD_DIM
GRID_W = 64
NA_ROWS = 8
NA_COLS = 16

LANES = 128
VMEM_LIMIT_BYTES = 56 * 1024 * 1024

TOKEN_TILE = 1024
QKV_TILE = 1024
FFN_TILE = 1024
FF_CHUNK = 512
SEG_COLS = 256
QKV_CHUNK = 512
PAIRS_PER_CHUNK = QKV_CHUNK // LANES
ATT_A_TILE = 1024
ATT_A_QB = 128
ATT_A_SIDE = 64
LSE_REP = LANES // HEADS_A
ATT_B_ROWS = 8
ATT_B_HALO = 4
ATT_B_PAIRS = 4
ATT_B_SUB = 2
ATT_B_WIN = NA_ROWS + ATT_B_SUB

BF16 = jnp.bfloat16
F32 = jnp.float32
NT_DIMS = (((1,), (1,)), ((), ()))
Q_SCALE = HEAD_DIM ** -0.5 * LOG2E


def _params(*semantics):
    return pltpu.CompilerParams(dimension_semantics=semantics,
                                vmem_limit_bytes=VMEM_LIMIT_BYTES)


def _rms_rows(x, g):
    ms = jnp.mean(x * x, axis=-1, keepdims=True)
    return (x * lax.rsqrt(ms + RMS_EPS)) * g


def _ffn_kernel(*refs, fused_proj):
    if fused_proj:
        x_ref, a_ref, wa_ref, g_ref, win_ref, wout_ref, o_ref, hn_ref = refs
        o_ref[...] = x_ref[...] + jnp.dot(a_ref[...], wa_ref[...], preferred_element_type=F32)
    else:
        x_ref, g_ref, win_ref, wout_ref, o_ref, hn_ref = refs
        o_ref[...] = x_ref[...]
    hn_ref[...] = _rms_rows(o_ref[...], g_ref[...]).astype(BF16)
    for c0 in range(0, D_FF, FF_CHUNK):
        width = min(FF_CHUNK, D_FF - c0)
        h = hn_ref[...]
        gate = jnp.dot(h, win_ref[:, c0:c0 + width], preferred_element_type=F32)
        up = jnp.dot(h, win_ref[:, D_FF + c0:D_FF + c0 + width], preferred_element_type=F32)
        a = (gate * jax.nn.sigmoid(gate) * up).astype(BF16)
        o_ref[...] += 0.5 * jnp.dot(a, wout_ref[c0:c0 + width, :], preferred_element_type=F32)


def _ffn(x, g, w_in, w_out, attn=None):
    t = x.shape[0]
    const = lambda i: (0, 0)
    resident = pl.Buffered(1)
    row = pl.BlockSpec((FFN_TILE, D_MODEL), lambda i: (i, 0))
    in_specs, args = [row], [x]
    if attn is not None:
        a, w_a = attn
        in_specs += [pl.BlockSpec((FFN_TILE, a.shape[1]), lambda i: (i, 0)),
                     pl.BlockSpec(w_a.shape, const, pipeline_mode=resident)]
        args += [a, w_a]
    in_specs += [pl.BlockSpec((1, D_MODEL), const),
                 pl.BlockSpec((D_MODEL, 2 * D_FF), const, pipeline_mode=resident),
                 pl.BlockSpec((D_FF, D_MODEL), const, pipeline_mode=resident)]
    args += [g.reshape(1, D_MODEL), w_in, w_out]
    return pl.pallas_call(
        functools.partial(_ffn_kernel, fused_proj=attn is not None),
        grid=(t // FFN_TILE,),
        in_specs=in_specs,
        out_specs=row,
        out_shape=jax.ShapeDtypeStruct((t, D_MODEL), F32),
        scratch_shapes=[pltpu.VMEM((FFN_TILE, D_MODEL), BF16)],
        compiler_params=_params("parallel"),
        name="ffn_proj" if attn is not None else "ffn",
    )(*args)


def _project_chunk(hn_ref, w_ref, gain_ref, seg_ref, c, normed):
    cols = slice(c * QKV_CHUNK, (c + 1) * QKV_CHUNK)
    y = jnp.dot(hn_ref[...], w_ref[:, cols], preferred_element_type=F32)
    if normed:
        y2 = (y * y).astype(BF16)
        ms = jnp.concatenate(
            [jnp.dot(y2[:, s0:s0 + SEG_COLS], seg_ref[...], preferred_element_type=F32)
             for s0 in range(0, QKV_CHUNK, SEG_COLS)], axis=-1)
        y = (y * lax.rsqrt(ms + RMS_EPS)) * gain_ref[:, cols]
    return y


def _qkv_a_kernel(x_ref, g_ref, w_ref, gain_ref, seg_ref, cos_ref, sin_ref, *rest):
    n_chunks = 3 * N_GROUPS_A
    out_refs, (hn_ref, ys_ref) = rest[:n_chunks], rest[n_chunks:]
    hn_ref[...] = _rms_rows(x_ref[...], g_ref[...]).astype(BF16)
    cos, sin = cos_ref[...], sin_ref[...]
    for c in range(n_chunks):
        kind, group = divmod(c, N_GROUPS_A)
        dilation = DILATED_GROUPS[group][1]
        normed = kind < 2
        y = _project_chunk(hn_ref, w_ref, gain_ref, seg_ref, c, normed)
        if normed:
            y = jnp.concatenate(
                [y[:, p * LANES:(p + 1) * LANES] * cos
                 + pltpu.roll(y[:, p * LANES:(p + 1) * LANES], LANES // 2, axis=1) * sin
                 for p in range(PAIRS_PER_CHUNK)], axis=-1)
        if dilation == 1:
            out_refs[c][0, 0] = y.astype(BF16)
        else:
            slot = kind * (N_GROUPS_A - 1) + group - 1
            rows = QKV_TILE // dilation
            for p in range(PAIRS_PER_CHUNK):
                cols = slice(p * LANES, (p + 1) * LANES)
                ys_ref[slot, p] = y[:, cols]
                for r in range(dilation):
                    out_refs[c][0, r, :, cols] = ys_ref[slot, p, pl.ds(r, rows, stride=dilation), :].astype(BF16)


def _qkv_a(x, g, w, gains, seg, rope_tables, batch, seq_len):
    t = x.shape[0]
    tps = seq_len // QKV_TILE
    const = lambda i: (0, 0)
    in_specs = [
        pl.BlockSpec((QKV_TILE, D_MODEL), lambda i: (i, 0)),
        pl.BlockSpec((1, D_MODEL), const),
        pl.BlockSpec((D_MODEL, QKV_A), const, pipeline_mode=pl.Buffered(1)),
        pl.BlockSpec((1, QKV_A), const),
        pl.BlockSpec((SEG_COLS, SEG_COLS), const),
        pl.BlockSpec((QKV_TILE, LANES), lambda i: (i % tps, 0)),
        pl.BlockSpec((QKV_TILE, LANES), lambda i: (i % tps, 0)),
    ]
    out_specs, out_shapes = [], []
    for c in range(3 * N_GROUPS_A):
        d = DILATED_GROUPS[c % N_GROUPS_A][1]
        out_specs.append(pl.BlockSpec((1, d, QKV_TILE // d, QKV_CHUNK),
                                      lambda i: (i // tps, 0, i % tps, 0)))
        out_shapes.append(jax.ShapeDtypeStruct((batch, d, seq_len // d, QKV_CHUNK), BF16))
    return pl.pallas_call(
        _qkv_a_kernel,
        grid=(t // QKV_TILE,),
        in_specs=in_specs,
        out_specs=out_specs,
        out_shape=out_shapes,
        scratch_shapes=[pltpu.VMEM((QKV_TILE, D_MODEL), BF16),
                        pltpu.VMEM((3 * (N_GROUPS_A - 1), PAIRS_PER_CHUNK, QKV_TILE, LANES), F32)],
        compiler_params=_params("parallel"),
        name="qkv_a",
    )(x, g.reshape(1, D_MODEL), w, gains, seg, *rope_tables)


def _qkv_b_kernel(x_ref, g_ref, w_ref, gain_ref, seg_ref, o_ref, hn_ref):
    hn_ref[...] = _rms_rows(x_ref[...], g_ref[...]).astype(BF16)
    for c in range(QKV_B // QKV_CHUNK):
        y = _project_chunk(hn_ref, w_ref, gain_ref, seg_ref, c, c < 2 * OUT_B // QKV_CHUNK)
        for p in range(PAIRS_PER_CHUNK):
            o_ref[0, c * PAIRS_PER_CHUNK + p] = y[:, p * LANES:(p + 1) * LANES].astype(BF16)


def _qkv_b(x, g, w, gains, seg, batch, seq_len):
    t = x.shape[0]
    tps = seq_len // QKV_TILE
    n_pairs = QKV_B // LANES
    const = lambda i: (0, 0)
    return pl.pallas_call(
        _qkv_b_kernel,
        grid=(t // QKV_TILE,),
        in_specs=[
            pl.BlockSpec((QKV_TILE, D_MODEL), lambda i: (i, 0)),
            pl.BlockSpec((1, D_MODEL), const),
            pl.BlockSpec((D_MODEL, QKV_B), const, pipeline_mode=pl.Buffered(1)),
            pl.BlockSpec((1, QKV_B), const),
            pl.BlockSpec((SEG_COLS, SEG_COLS), const),
        ],
        out_specs=pl.BlockSpec((1, n_pairs, QKV_TILE, LANES), lambda i: (i // tps, 0, i % tps, 0)),
        out_shape=jax.ShapeDtypeStruct((batch, n_pairs, seq_len, LANES), BF16),
        scratch_shapes=[pltpu.VMEM((QKV_TILE, D_MODEL), BF16)],
        compiler_params=_params("parallel"),
        name="qkv_b",
    )(x, g.reshape(1, D_MODEL), w, gains, seg)


def _attn_a_kernel(q_ref, kp_ref, kc_ref, kn_ref, vp_ref, vc_ref, vn_ref,
                   o_ref, lse_ref, kw_ref, vw_ref, s_ref, p_ref, m_ref, *, seq_len, tq):
    side, qb = ATT_A_SIDE, ATT_A_QB
    kb = qb + 2 * side
    nb = tq // qb
    kw_ref[0:side] = kp_ref[0, 0]
    kw_ref[side:side + tq] = kc_ref[0, 0]
    kw_ref[side + tq:] = kn_ref[0, 0]
    ones = jnp.ones((tq + 2 * side, LANES), BF16)
    for p in range(PAIRS_PER_CHUNK):
        src = slice(p * LANES, (p + 1) * LANES)
        dst = slice(2 * p * LANES, (2 * p + 1) * LANES)
        vw_ref[0:side, dst] = vp_ref[0, 0, :, src]
        vw_ref[side:side + tq, dst] = vc_ref[0, 0, :, src]
        vw_ref[side + tq:, dst] = vn_ref[0, 0, :, src]
        vw_ref[:, (2 * p + 1) * LANES:(2 * p + 2) * LANES] = ones

    q_start = pl.program_id(2) * tq
    row = lax.broadcasted_iota(jnp.int32, (qb, kb), 0)
    col = lax.broadcasted_iota(jnp.int32, (qb, kb), 1)
    band = (col >= row) & (col - row <= 2 * side)
    biases = []
    for b in range(nb):
        kpos = q_start + b * qb + col - side
        bias = jnp.where(band & (kpos >= 0) & (kpos < seq_len), 0.0, NEG_INF).astype(F32)
        biases.append(jnp.concatenate([bias, bias], axis=0))
    bias4 = jnp.stack(biases)

    lane = lax.broadcasted_iota(jnp.int32, (tq, LANES), 1)
    q_head_a = (lane % HEAD_DIM) < (HEAD_DIM // 2)
    v_head_a = lax.broadcasted_iota(jnp.int32, (qb, LANES), 1) < HEAD_DIM
    lse_lane = lax.broadcasted_iota(jnp.int32, (qb, LANES), 1) // LSE_REP
    zero = jnp.zeros((tq, LANES), BF16)

    for p in range(PAIRS_PER_CHUNK):
        cols = slice(p * LANES, (p + 1) * LANES)
        q2 = q_ref[0, 0, :, cols]
        qa = jnp.where(q_head_a, q2, zero)
        qbb = jnp.where(q_head_a, zero, q2)
        for b in range(nb):
            rows = slice(b * qb, (b + 1) * qb)
            lhs = jnp.concatenate([qa[rows], qbb[rows]], axis=0)
            s_ref[p * nb + b] = lax.dot_general(lhs, kw_ref[b * qb:b * qb + kb, cols], NT_DIMS,
                                                preferred_element_type=F32)
        blk = slice(p * nb, (p + 1) * nb)
        s = s_ref[blk] + bias4
        m = jnp.max(s, axis=-1, keepdims=True)
        p_ref[blk] = jnp.exp2(s - m).astype(BF16)
        m_ref[blk] = jnp.broadcast_to(m, (nb, 2 * qb, LANES))
        for b in range(nb):
            rows = slice(b * qb, (b + 1) * qb)
            pv = jnp.dot(p_ref[p * nb + b], vw_ref[b * qb:b * qb + kb, 2 * p * LANES:(2 * p + 2) * LANES],
                         preferred_element_type=F32)
            den = pv[:, LANES:]
            o = pv[:, :LANES] / den
            lse = m_ref[p * nb + b] + jnp.log2(den)
            o_ref[0, 0, rows, cols] = jnp.where(v_head_a, o[:qb], o[qb:]).astype(BF16)
            prev = lse_ref[0, 0, rows, :] if p else jnp.zeros((qb, LANES), F32)
            lse_ref[0, 0, rows, :] = jnp.where(lse_lane == 2 * p, lse[:qb],
                                               jnp.where(lse_lane == 2 * p + 1, lse[qb:], prev))


def _attn_a_group(q, k, v, dilation, batch, seq_len):
    length = seq_len // dilation
    tq, side = min(ATT_A_TILE, length), ATT_A_SIDE
    n_tiles = length // tq
    halo_blocks = length // side
    nb = tq // ATT_A_QB
    kb = ATT_A_QB + 2 * side
    cur = pl.BlockSpec((1, 1, tq, OUT_A), lambda b, r, i: (b, r, i, 0))
    prev = pl.BlockSpec((1, 1, side, OUT_A),
                        lambda b, r, i: (b, r, jnp.maximum(i * (tq // side) - 1, 0), 0))
    nxt = pl.BlockSpec((1, 1, side, OUT_A),
                       lambda b, r, i: (b, r, jnp.minimum((i + 1) * (tq // side), halo_blocks - 1), 0))
    n_blk = PAIRS_PER_CHUNK * nb
    lse_spec = pl.BlockSpec((1, 1, tq, LANES), lambda b, r, i: (b, r, i, 0))
    return pl.pallas_call(
        functools.partial(_attn_a_kernel, seq_len=length, tq=tq),
        grid=(batch, dilation, n_tiles),
        in_specs=[cur, prev, cur, nxt, prev, cur, nxt],
        out_specs=[cur, lse_spec],
        out_shape=[jax.ShapeDtypeStruct((batch, dilation, length, OUT_A), BF16),
                   jax.ShapeDtypeStruct((batch, dilation, length, LANES), F32)],
        scratch_shapes=[pltpu.VMEM((tq + 2 * side, OUT_A), BF16),
                        pltpu.VMEM((tq + 2 * side, 2 * OUT_A), BF16),
                        pltpu.VMEM((n_blk, 2 * ATT_A_QB, kb), F32),
                        pltpu.VMEM((n_blk, 2 * ATT_A_QB, kb), BF16),
                        pltpu.VMEM((n_blk, 2 * ATT_A_QB, LANES), F32)],
        compiler_params=_params("parallel", "parallel", "arbitrary"),
        name=f"attn_a_d{dilation}",
    )(q, k, k, k, v, v, v)


def _merge_a_kernel(o0_ref, o1_ref, o2_ref, l0_ref, l1_ref, l2_ref, e_ref, y_ref, ot_ref, lt_ref):
    for slot, (o_ref, l_ref, dilation) in enumerate(
            ((o1_ref, l1_ref, DILATED_GROUPS[1][1]), (o2_ref, l2_ref, DILATED_GROUPS[2][1]))):
        rows = TOKEN_TILE // dilation
        for r in range(dilation):
            lt_ref[slot, pl.ds(r, rows, stride=dilation), :] = l_ref[0, r]
            for p in range(PAIRS_PER_CHUNK):
                cols = slice(p * LANES, (p + 1) * LANES)
                ot_ref[slot, p, pl.ds(r, rows, stride=dilation), :] = o_ref[0, r, :, cols].astype(F32)
    l0, l1, l2 = l0_ref[0, 0], lt_ref[0], lt_ref[1]
    m = jnp.maximum(jnp.maximum(l0, l1), l2)
    e0, e1, e2 = jnp.exp2(l0 - m), jnp.exp2(l1 - m), jnp.exp2(l2 - m)
    inv = 1.0 / (e0 + e1 + e2)

    def spread(w):
        hi = w.astype(BF16)
        lo = (w - hi.astype(F32)).astype(BF16)
        return (jnp.dot(hi, e_ref[...], preferred_element_type=F32)
                + jnp.dot(lo, e_ref[...], preferred_element_type=F32))

    w0, w1, w2 = spread(e0 * inv), spread(e1 * inv), spread(e2 * inv)
    for p in range(PAIRS_PER_CHUNK):
        cols = slice(p * LANES, (p + 1) * LANES)
        y_ref[:, cols] = (w0[:, cols] * o0_ref[0, 0, :, cols].astype(F32) + w1[:, cols] * ot_ref[0, p]
                          + w2[:, cols] * ot_ref[1, p]).astype(BF16)


def _merge_a(outs, lses, batch, seq_len):
    t = batch * seq_len
    tps = seq_len // TOKEN_TILE
    grp = [pl.BlockSpec((1, d, TOKEN_TILE // d, OUT_A), lambda i: (i // tps, 0, i % tps, 0))
           for _, d in DILATED_GROUPS]
    grp_lse = [pl.BlockSpec((1, d, TOKEN_TILE // d, LANES), lambda i: (i // tps, 0, i % tps, 0))
               for _, d in DILATED_GROUPS]
    spread = jnp.asarray(np.arange(LANES)[:, None] == LSE_REP * (np.arange(OUT_A)[None, :] // HEAD_DIM), BF16)
    return pl.pallas_call(
        _merge_a_kernel,
        grid=(t // TOKEN_TILE,),
        in_specs=grp + grp_lse + [pl.BlockSpec((LANES, OUT_A), lambda i: (0, 0))],
        out_specs=pl.BlockSpec((TOKEN_TILE, OUT_A), lambda i: (i, 0)),
        out_shape=jax.ShapeDtypeStruct((t, OUT_A), BF16),
        scratch_shapes=[pltpu.VMEM((N_GROUPS_A - 1, PAIRS_PER_CHUNK, TOKEN_TILE, LANES), F32),
                        pltpu.VMEM((N_GROUPS_A - 1, TOKEN_TILE, LANES), F32)],
        compiler_params=_params("parallel"),
        name="merge_a",
    )(*outs, *lses, spread)


def _attn_b_kernel(q_ref, kp_ref, kc_ref, kn_ref, vp_ref, vc_ref, vn_ref, bias_ref,
                   o_ref, kw_ref, vw_ref, s_ref, p_ref):
    w, halo, rows = GRID_W, ATT_B_HALO, ATT_B_ROWS
    n_sub = rows // ATT_B_SUB
    sub_q = ATT_B_SUB * w
    win = ATT_B_WIN * w
    ones = jnp.ones(((rows + 2 * halo) * w, LANES), BF16)
    for u in range(ATT_B_PAIRS):
        kw_ref[u, 0:halo * w] = kp_ref[0, u]
        kw_ref[u, halo * w:(halo + rows) * w] = kc_ref[0, u]
        kw_ref[u, (halo + rows) * w:] = kn_ref[0, u]
        vw_ref[u, 0:halo * w, :LANES] = vp_ref[0, u]
        vw_ref[u, halo * w:(halo + rows) * w, :LANES] = vc_ref[0, u]
        vw_ref[u, (halo + rows) * w:, :LANES] = vn_ref[0, u]
        vw_ref[u, :, LANES:] = ones

    i = pl.program_id(2)
    lo_min = jnp.where(i == 0, halo, 0)
    lo_max = jnp.where(i == pl.num_programs(2) - 1, halo, rows)
    starts = [pl.multiple_of(jnp.clip(ATT_B_SUB * sb, lo_min, lo_max) * w, LANES) for sb in range(n_sub)]
    head_a = lax.broadcasted_iota(jnp.int32, (rows * w, LANES), 1) < HEAD_DIM
    out_a = lax.broadcasted_iota(jnp.int32, (sub_q, LANES), 1) < HEAD_DIM
    zero = jnp.zeros((rows * w, LANES), BF16)
    for u in range(ATT_B_PAIRS):
        q2 = q_ref[0, u]
        qa = jnp.where(head_a, q2, zero)
        qb = jnp.where(head_a, zero, q2)
        for sb in range(n_sub):
            rs = slice(sb * sub_q, (sb + 1) * sub_q)
            lhs = jnp.concatenate([qa[rs], qb[rs]], axis=0)
            s_ref[u, sb] = lax.dot_general(lhs, kw_ref[u, pl.ds(starts[sb], win), :], NT_DIMS,
                                           preferred_element_type=F32)
    for u in range(ATT_B_PAIRS):
        s = s_ref[u] + jnp.concatenate([bias_ref[2 * u, 0], bias_ref[2 * u + 1, 0]], axis=1)
        m = jnp.max(s, axis=-1, keepdims=True)
        p_ref[u] = jnp.exp2(s - m).astype(BF16)
    for u in range(ATT_B_PAIRS):
        for sb in range(n_sub):
            pv = jnp.dot(p_ref[u, sb], vw_ref[u, pl.ds(starts[sb], win), :], preferred_element_type=F32)
            o = pv[:, :LANES] / pv[:, LANES:]
            o_ref[0, sb * sub_q:(sb + 1) * sub_q, u * LANES:(u + 1) * LANES] = (
                jnp.where(out_a, o[:sub_q], o[sub_q:]).astype(BF16))


def _attn_b(qkv, bias, batch, seq_len):
    n_rows = seq_len // GRID_W
    tile = ATT_B_ROWS * GRID_W
    halo = ATT_B_HALO * GRID_W
    n_tiles = n_rows // ATT_B_ROWS
    assert n_tiles >= 2
    per_tile = tile // halo
    halo_blocks = seq_len // halo
    n_steps = HEADS_B // 2 // ATT_B_PAIRS
    n_sub = ATT_B_ROWS // ATT_B_SUB
    sub_q = ATT_B_SUB * GRID_W
    win = ATT_B_WIN * GRID_W

    def cur(kind):
        return pl.BlockSpec((1, ATT_B_PAIRS, tile, LANES), lambda hp, b, i: (b, kind * n_steps + hp, i, 0))

    def prev(kind):
        return pl.BlockSpec((1, ATT_B_PAIRS, halo, LANES),
                            lambda hp, b, i: (b, kind * n_steps + hp, jnp.maximum(i * per_tile - 1, 0), 0))

    def nxt(kind):
        return pl.BlockSpec((1, ATT_B_PAIRS, halo, LANES),
                            lambda hp, b, i: (b, kind * n_steps + hp,
                                              jnp.minimum((i + 1) * per_tile, halo_blocks - 1), 0))

    def tile_kind(i):
        return jnp.where(i == 0, 0, jnp.where(i == n_tiles - 1, 2, 1))

    return pl.pallas_call(
        _attn_b_kernel,
        grid=(n_steps, batch, n_tiles),
        in_specs=[cur(0), prev(1), cur(1), nxt(1), prev(2), cur(2), nxt(2),
                  pl.BlockSpec((2 * ATT_B_PAIRS, 1, n_sub, sub_q, win),
                               lambda hp, b, i: (hp, tile_kind(i), 0, 0, 0))],
        out_specs=pl.BlockSpec((1, tile, ATT_B_PAIRS * LANES), lambda hp, b, i: (b, i, hp)),
        out_shape=jax.ShapeDtypeStruct((batch, seq_len, OUT_B), BF16),
        scratch_shapes=[pltpu.VMEM((ATT_B_PAIRS, tile + 2 * halo, LANES), BF16),
                        pltpu.VMEM((ATT_B_PAIRS, tile + 2 * halo, 2 * LANES), BF16),
                        pltpu.VMEM((ATT_B_PAIRS, n_sub, 2 * sub_q, win), F32),
                        pltpu.VMEM((ATT_B_PAIRS, n_sub, 2 * sub_q, win), BF16)],
        compiler_params=_params("arbitrary", "arbitrary", "arbitrary"),
        name="attn_b",
    )(qkv, qkv, qkv, qkv, qkv, qkv, qkv, bias)


def _rope_layout():
    n = np.arange(QKV_CHUNK)
    pair, lane = n // LANES, n % LANES
    second, hb, i = lane // HEAD_DIM, (lane % HEAD_DIM) // (HEAD_DIM // 2), lane % (HEAD_DIM // 2)
    head = 2 * pair + hb
    return head * HEAD_DIM + second * (HEAD_DIM // 2) + i, head


def _segment_mean_matrix(head_of_column):
    same = head_of_column[:, None] == head_of_column[None, :]
    return jnp.asarray(np.where(same, 1.0 / HEAD_DIM, 0.0), BF16)


def _rope_tables(seq_len):
    half = HEAD_DIM // 2
    inv_freq = np.float32(ROPE_THETA) ** (-np.arange(half, dtype=np.float32) / np.float32(half))
    ang = (np.arange(seq_len, dtype=np.float32)[:, None] * inv_freq[None, :]).astype(np.float64)
    cos, sin = jnp.asarray(np.cos(ang), F32), jnp.asarray(np.sin(ang), F32)
    return (jnp.concatenate([cos, cos, cos, cos], axis=-1),
            jnp.concatenate([-sin, -sin, sin, sin], axis=-1))


def _prep_a(w_qkv, q_gain, k_gain):
    src, _ = _rope_layout()
    n_qk = 2 * N_GROUPS_A
    cols = np.concatenate([c * QKV_CHUNK + src for c in range(n_qk)]
                          + [np.arange(n_qk * QKV_CHUNK, QKV_A)])
    half = HEAD_DIM // 2

    def pair_lanes(gain):
        return jnp.concatenate([gain[:half], gain[:half], gain[half:], gain[half:]])

    n_pairs = N_GROUPS_A * PAIRS_PER_CHUNK
    gains = jnp.concatenate([jnp.tile(pair_lanes(q_gain) * Q_SCALE, n_pairs),
                             jnp.tile(pair_lanes(k_gain), n_pairs),
                             jnp.ones((QKV_A - n_qk * QKV_CHUNK,), F32)])
    return w_qkv[:, cols].astype(BF16), gains.reshape(1, QKV_A)


def _prep_b(q_gain, k_gain):
    gains = jnp.concatenate([jnp.tile(q_gain * Q_SCALE, HEADS_B),
                             jnp.tile(k_gain, HEADS_B),
                             jnp.ones((OUT_B,), F32)])
    return gains.reshape(1, QKV_B)


def _bias_tile_plan():
    halo, rows, n_sub = ATT_B_HALO, ATT_B_ROWS, ATT_B_ROWS // ATT_B_SUB
    plan = {}
    for kind, (lo_min, lo_max) in enumerate(((halo, rows), (0, rows), (0, halo))):
        for sb in range(n_sub):
            ws = min(max(ATT_B_SUB * sb, lo_min), lo_max)
            for ql in range(ATT_B_SUB):
                rho = ATT_B_SUB * sb + ql
                lo = min(max(rho, lo_min), lo_max)
                for jj in range(ATT_B_WIN // 2):
                    jl = ws + 2 * jj
                    dr = jl - rho + NA_ROWS // 2 - 1
                    plan[kind, sb, ql, jj] = (dr, lo <= jl < lo + NA_ROWS, lo <= jl + 1 < lo + NA_ROWS)
    return plan


def _bias_expand_kernel(c_ref, o_ref):
    w = GRID_W
    left = lax.broadcasted_iota(jnp.int32, (w, 2 * w), 1) < w
    neg = jnp.full((w, 2 * w), NEG_INF, F32)
    for (kind, sb, ql, jj), (dr, ok_a, ok_b) in _bias_tile_plan().items():
        if ok_a or ok_b:
            tile = c_ref[0, dr + 1]
            if not ok_b:
                tile = jnp.where(left, tile, neg)
            elif not ok_a:
                tile = jnp.where(left, neg, tile)
        else:
            tile = neg
        o_ref[0, kind, sb, ql * w:(ql + 1) * w, 2 * jj * w:2 * (jj + 1) * w] = tile


def _bias_table_b(rpb):
    w, n_sub, n_dr = GRID_W, ATT_B_ROWS // ATT_B_SUB, 2 * NA_ROWS - 1
    col = np.arange(w)
    col_start = np.clip(col - NA_COLS // 2, 0, w - NA_COLS)
    col_ok = (col[None, :] >= col_start[:, None]) & (col[None, :] < col_start[:, None] + NA_COLS)
    dc = np.clip(col[None, :] - col[:, None] + NA_COLS - 1, 0, 2 * NA_COLS - 2)
    pick = jnp.asarray(np.arange(2 * NA_COLS - 1)[:, None, None] == dc[None], F32)
    by_row = jnp.einsum("hrc,cqk->hrqk", rpb * LOG2E, pick, precision=lax.Precision.HIGHEST)
    by_row = jnp.where(col_ok[None, None], by_row, NEG_INF)
    pad = jnp.full((HEADS_B, 1, w, w), NEG_INF, F32)
    by_row = jnp.concatenate([pad, by_row, pad], axis=1)
    pairs = jnp.concatenate([by_row[:, :-1], by_row[:, 1:]], axis=-1)
    return pl.pallas_call(
        _bias_expand_kernel,
        grid=(HEADS_B,),
        in_specs=[pl.BlockSpec((1, n_dr + 1, w, 2 * w), lambda h: (h, 0, 0, 0))],
        out_specs=pl.BlockSpec((1, 3, n_sub, ATT_B_SUB * w, ATT_B_WIN * w), lambda h: (h, 0, 0, 0, 0)),
        out_shape=jax.ShapeDtypeStruct((HEADS_B, 3, n_sub, ATT_B_SUB * w, ATT_B_WIN * w), F32),
        compiler_params=_params("parallel"),
        name="bias_expand",
    )(pairs)


def _mixer_a(x, batch, seq_len, mix_g, w_qkv, gains, seg, rope_tables):
    qkv = _qkv_a(x, mix_g, w_qkv, gains, seg, rope_tables, batch, seq_len)
    outs, lses = [], []
    for group, (window, dilation) in enumerate(DILATED_GROUPS):
        assert window // (2 * dilation) == ATT_A_SIDE
        o, lse = _attn_a_group(qkv[group], qkv[N_GROUPS_A + group], qkv[2 * N_GROUPS_A + group],
                               dilation, batch, seq_len)
        outs.append(o)
        lses.append(lse)
    return _merge_a(outs, lses, batch, seq_len)


def _mixer_b(x, batch, seq_len, mix_g, w_qkv, gains, bias, seg):
    qkv = _qkv_b(x, mix_g, w_qkv, gains, seg, batch, seq_len)
    return _attn_b(qkv, bias, batch, seq_len).reshape(batch * seq_len, OUT_B)


def kernel(x_prompt, x_sample, ffn1_norm, ffn1_w_in, ffn1_w_out, mix_norm, ffn2_norm, ffn2_w_in, ffn2_w_out, a_w_qkv, a_q_norm, a_k_norm, a_w_o, b_w_qkv, b_q_norm, b_k_norm, b_rpb, b_w_o):
    depth = ffn1_norm.shape[0]
    bf = lambda w: w.astype(BF16)
    ffn1_w_in, ffn1_w_out, ffn2_w_in, ffn2_w_out = map(bf, (ffn1_w_in, ffn1_w_out, ffn2_w_in, ffn2_w_out))
    a_w_o, b_w_qkv, b_w_o = map(bf, (a_w_o, b_w_qkv, b_w_o))
    seg_a = _segment_mean_matrix(_rope_layout()[1][:SEG_COLS])
    seg_b = _segment_mean_matrix(np.arange(SEG_COLS) // HEAD_DIM)
    prep_a = [_prep_a(a_w_qkv[j], a_q_norm[j], a_k_norm[j]) for j in range(a_w_qkv.shape[0])]
    prep_b = [(_prep_b(b_q_norm[j], b_k_norm[j]), _bias_table_b(b_rpb[j])) for j in range(b_w_qkv.shape[0])]
    rope_tables = _rope_tables(max(x_prompt.shape[1], x_sample.shape[1]))

    def trunk(x3):
        batch, seq_len, _ = x3.shape
        x = x3.reshape(batch * seq_len, D_MODEL)
        for i in range(depth):
            j = i // 2
            x = _ffn(x, ffn1_norm[i], ffn1_w_in[i], ffn1_w_out[i])
            if i % 2 == 0:
                w_qkv, gains = prep_a[j]
                attn = (_mixer_a(x, batch, seq_len, mix_norm[i], w_qkv, gains, seg_a, rope_tables), a_w_o[j])
            else:
                gains, bias = prep_b[j]
                attn = (_mixer_b(x, batch, seq_len, mix_norm[i], b_w_qkv[j], gains, bias, seg_b), b_w_o[j])
            x = _ffn(x, ffn2_norm[i], ffn2_w_in[i], ffn2_w_out[i], attn)
        return x.reshape(batch, seq_len, D_MODEL)

    return (trunk(x_prompt), trunk(x_sample))
```

```python
import functools

import jax
import jax.numpy as jnp
import numpy as np
from jax import lax
from jax.experimental import pallas as pl
from jax.experimental.pallas import tpu as pltpu

D_MODEL = 1024
HEAD_DIM = 64
D_FF = 2816
RMS_EPS = 1e-6
ROPE_THETA = 10000.0
NEG_INF = -1e30
LOG2E = 1.4426950408889634
DILATED_GROUPS = ((128, 1), (512, 4), (2048, 16))
HEADS_A = 8
N_GROUPS_A = 3
QKV_A = 3 * N_GROUPS_A * HEADS_A * HEAD_DIM
OUT_A = HEADS_A * HEAD_DIM
HEADS_B = 16
QKV_B = 3 * HEADS_B * HEAD_DIM
OUT_B = HEADS_B * HEAD_DIM
GRID_W = 64
NA_ROWS = 8
NA_COLS = 16

LANES = 128
VMEM_LIMIT_BYTES = 56 * 1024 * 1024

TOKEN_TILE = 1024
QKV_TILE = 1024
FFN_TILE = 1024
FF_CHUNK = 512
SEG_COLS = 256
QKV_CHUNK = 512
PAIRS_PER_CHUNK = QKV_CHUNK // LANES
ATT_A_TILE = 512
ATT_A_QB = 128
ATT_A_SIDE = 64
LSE_REP = LANES // HEADS_A
ATT_B_ROWS = 8
ATT_B_HALO = 4
ATT_B_PAIRS = 4
ATT_B_SUB = 2
ATT_B_WIN = NA_ROWS + ATT_B_SUB

BF16 = jnp.bfloat16
F32 = jnp.float32
NT_DIMS = (((1,), (1,)), ((), ()))
Q_SCALE = HEAD_DIM ** -0.5 * LOG2E


def _params(*semantics):
    return pltpu.CompilerParams(dimension_semantics=semantics,
                                vmem_limit_bytes=VMEM_LIMIT_BYTES)


def _rms_rows(x, g):
    ms = jnp.mean(x * x, axis=-1, keepdims=True)
    return (x * lax.rsqrt(ms + RMS_EPS)) * g


def _ffn_kernel(*refs, fused_proj):
    if fused_proj:
        x_ref, a_ref, wa_ref, g_ref, win_ref, wout_ref, o_ref, hn_ref = refs
        o_ref[...] = x_ref[...] + jnp.dot(a_ref[...], wa_ref[...], preferred_element_type=F32)
    else:
        x_ref, g_ref, win_ref, wout_ref, o_ref, hn_ref = refs
        o_ref[...] = x_ref[...]
    hn_ref[...] = _rms_rows(o_ref[...], g_ref[...]).astype(BF16)
    for c0 in range(0, D_FF, FF_CHUNK):
        width = min(FF_CHUNK, D_FF - c0)
        h = hn_ref[...]
        gate = jnp.dot(h, win_ref[:, c0:c0 + width], preferred_element_type=F32)
        up = jnp.dot(h, win_ref[:, D_FF + c0:D_FF + c0 + width], preferred_element_type=F32)
        a = (gate * jax.nn.sigmoid(gate) * up).astype(BF16)
        o_ref[...] += 0.5 * jnp.dot(a, wout_ref[c0:c0 + width, :], preferred_element_type=F32)


def _ffn(x, g, w_in, w_out, layer, attn=None):
    t = x.shape[0]
    const = lambda i: (0, 0)
    this_layer = lambda i: (layer, 0, 0)
    resident = pl.Buffered(1)
    row = pl.BlockSpec((FFN_TILE, D_MODEL), lambda i: (i, 0))
    in_specs, args = [row], [x]
    if attn is not None:
        a, w_a = attn
        in_specs += [pl.BlockSpec((FFN_TILE, a.shape[1]), lambda i: (i, 0)),
                     pl.BlockSpec(w_a.shape, const, pipeline_mode=resident)]
        args += [a, w_a]
    in_specs += [pl.BlockSpec((1, D_MODEL), const),
                 pl.BlockSpec((None, D_MODEL, 2 * D_FF), this_layer, pipeline_mode=resident),
                 pl.BlockSpec((None, D_FF, D_MODEL), this_layer, pipeline_mode=resident)]
    args += [g.reshape(1, D_MODEL), w_in, w_out]
    return pl.pallas_call(
        functools.partial(_ffn_kernel, fused_proj=attn is not None),
        grid=(t // FFN_TILE,),
        in_specs=in_specs,
        out_specs=row,
        out_shape=jax.ShapeDtypeStruct((t, D_MODEL), F32),
        scratch_shapes=[pltpu.VMEM((FFN_TILE, D_MODEL), BF16)],
        compiler_params=_params("parallel"),
        name="ffn_proj" if attn is not None else "ffn",
    )(*args)


def _project_chunk(hn_ref, w_ref, gain_ref, seg_ref, c, normed):
    cols = slice(c * QKV_CHUNK, (c + 1) * QKV_CHUNK)
    y = jnp.dot(hn_ref[...], w_ref[:, cols], preferred_element_type=F32)
    if normed:
        y2 = (y * y).astype(BF16)
        ms = jnp.concatenate(
            [jnp.dot(y2[:, s0:s0 + SEG_COLS], seg_ref[...], preferred_element_type=F32)
             for s0 in range(0, QKV_CHUNK, SEG_COLS)], axis=-1)
        y = (y * lax.rsqrt(ms + RMS_EPS)) * gain_ref[:, cols]
    return y


def _qkv_a_kernel(x_ref, g_ref, w_ref, gain_ref, seg_ref, cos_ref, sin_ref, *rest):
    n_chunks = 3 * N_GROUPS_A
    out_refs, (hn_ref, ys_ref) = rest[:n_chunks], rest[n_chunks:]
    hn_ref[...] = _rms_rows(x_ref[...], g_ref[...]).astype(BF16)
    cos, sin = cos_ref[...], sin_ref[...]
    for c in range(n_chunks):
        kind, group = divmod(c, N_GROUPS_A)
        dilation = DILATED_GROUPS[group][1]
        normed = kind < 2
        y = _project_chunk(hn_ref, w_ref, gain_ref, seg_ref, c, normed)
        if normed:
            y = jnp.concatenate(
                [y[:, p * LANES:(p + 1) * LANES] * cos
                 + pltpu.roll(y[:, p * LANES:(p + 1) * LANES], LANES // 2, axis=1) * sin
                 for p in range(PAIRS_PER_CHUNK)], axis=-1)
        if dilation == 1:
            out_refs[c][0, 0] = y.astype(BF16)
        else:
            slot = kind * (N_GROUPS_A - 1) + group - 1
            rows = QKV_TILE // dilation
            for p in range(PAIRS_PER_CHUNK):
                cols = slice(p * LANES, (p + 1) * LANES)
                ys_ref[slot, p] = y[:, cols]
                for r in range(dilation):
                    out_refs[c][0, r, :, cols] = ys_ref[slot, p, pl.ds(r, rows, stride=dilation), :].astype(BF16)


def _qkv_a(x, g, w, gains, seg, rope_tables, batch, seq_len):
    t = x.shape[0]
    tps = seq_len // QKV_TILE
    const = lambda i: (0, 0)
    in_specs = [
        pl.BlockSpec((QKV_TILE, D_MODEL), lambda i: (i, 0)),
        pl.BlockSpec((1, D_MODEL), const),
        pl.BlockSpec((D_MODEL, QKV_A), const, pipeline_mode=pl.Buffered(1)),
        pl.BlockSpec((1, QKV_A), const),
        pl.BlockSpec((SEG_COLS, SEG_COLS), const),
        pl.BlockSpec((QKV_TILE, LANES), lambda i: (i % tps, 0)),
        pl.BlockSpec((QKV_TILE, LANES), lambda i: (i % tps, 0)),
    ]
    out_specs, out_shapes = [], []
    for c in range(3 * N_GROUPS_A):
        d = DILATED_GROUPS[c % N_GROUPS_A][1]
        out_specs.append(pl.BlockSpec((1, d, QKV_TILE // d, QKV_CHUNK),
                                      lambda i: (i // tps, 0, i % tps, 0)))
        out_shapes.append(jax.ShapeDtypeStruct((batch, d, seq_len // d, QKV_CHUNK), BF16))
    return pl.pallas_call(
        _qkv_a_kernel,
        grid=(t // QKV_TILE,),
        in_specs=in_specs,
        out_specs=out_specs,
        out_shape=out_shapes,
        scratch_shapes=[pltpu.VMEM((QKV_TILE, D_MODEL), BF16),
                        pltpu.VMEM((3 * (N_GROUPS_A - 1), PAIRS_PER_CHUNK, QKV_TILE, LANES), F32)],
        compiler_params=_params("parallel"),
        name="qkv_a",
    )(x, g.reshape(1, D_MODEL), w, gains, seg, *rope_tables)


def _qkv_b_kernel(x_ref, g_ref, w_ref, gain_ref, seg_ref, o_ref, hn_ref):
    hn_ref[...] = _rms_rows(x_ref[...], g_ref[...]).astype(BF16)
    for c in range(QKV_B // QKV_CHUNK):
        y = _project_chunk(hn_ref, w_ref, gain_ref, seg_ref, c, c < 2 * OUT_B // QKV_CHUNK)
        for p in range(PAIRS_PER_CHUNK):
            o_ref[0, c * PAIRS_PER_CHUNK + p] = y[:, p * LANES:(p + 1) * LANES].astype(BF16)


def _qkv_b(x, g, w, gains, seg, batch, seq_len):
    t = x.shape[0]
    tps = seq_len // QKV_TILE
    n_pairs = QKV_B // LANES
    const = lambda i: (0, 0)
    return pl.pallas_call(
        _qkv_b_kernel,
        grid=(t // QKV_TILE,),
        in_specs=[
            pl.BlockSpec((QKV_TILE, D_MODEL), lambda i: (i, 0)),
            pl.BlockSpec((1, D_MODEL), const),
            pl.BlockSpec((D_MODEL, QKV_B), const, pipeline_mode=pl.Buffered(1)),
            pl.BlockSpec((1, QKV_B), const),
            pl.BlockSpec((SEG_COLS, SEG_COLS), const),
        ],
        out_specs=pl.BlockSpec((1, n_pairs, QKV_TILE, LANES), lambda i: (i // tps, 0, i % tps, 0)),
        out_shape=jax.ShapeDtypeStruct((batch, n_pairs, seq_len, LANES), BF16),
        scratch_shapes=[pltpu.VMEM((QKV_TILE, D_MODEL), BF16)],
        compiler_params=_params("parallel"),
        name="qkv_b",
    )(x, g.reshape(1, D_MODEL), w, gains, seg)


def _attn_a_kernel(q_ref, kp_ref, kc_ref, kn_ref, vp_ref, vc_ref, vn_ref,
                   o_ref, lse_ref, kw_ref, vw_ref, s_ref, p_ref, m_ref, *, seq_len, tq):
    side, qb = ATT_A_SIDE, ATT_A_QB
    kb = qb + 2 * side
    nb = tq // qb
    kw_ref[0:side] = kp_ref[0, 0]
    kw_ref[side:side + tq] = kc_ref[0, 0]
    kw_ref[side + tq:] = kn_ref[0, 0]
    ones = jnp.ones((tq + 2 * side, LANES), BF16)
    for p in range(PAIRS_PER_CHUNK):
        src = slice(p * LANES, (p + 1) * LANES)
        dst = slice(2 * p * LANES, (2 * p + 1) * LANES)
        vw_ref[0:side, dst] = vp_ref[0, 0, :, src]
        vw_ref[side:side + tq, dst] = vc_ref[0, 0, :, src]
        vw_ref[side + tq:, dst] = vn_ref[0, 0, :, src]
        vw_ref[:, (2 * p + 1) * LANES:(2 * p + 2) * LANES] = ones

    q_start = pl.program_id(2) * tq
    row = lax.broadcasted_iota(jnp.int32, (qb, kb), 0)
    col = lax.broadcasted_iota(jnp.int32, (qb, kb), 1)
    band = (col >= row) & (col - row <= 2 * side)
    biases = []
    for b in range(nb):
        kpos = q_start + b * qb + col - side
        bias = jnp.where(band & (kpos >= 0) & (kpos < seq_len), 0.0, NEG_INF).astype(F32)
        biases.append(jnp.concatenate([bias, bias], axis=0))
    bias4 = jnp.stack(biases)

    lane = lax.broadcasted_iota(jnp.int32, (tq, LANES), 1)
    q_head_a = (lane % HEAD_DIM) < (HEAD_DIM // 2)
    v_head_a = lax.broadcasted_iota(jnp.int32, (qb, LANES), 1) < HEAD_DIM
    lse_lane = lax.broadcasted_iota(jnp.int32, (qb, LANES), 1) // LSE_REP
    zero = jnp.zeros((tq, LANES), BF16)

    for p in range(PAIRS_PER_CHUNK):
        cols = slice(p * LANES, (p + 1) * LANES)
        q2 = q_ref[0, 0, :, cols]
        qa = jnp.where(q_head_a, q2, zero)
        qbb = jnp.where(q_head_a, zero, q2)
        for b in range(nb):
            rows = slice(b * qb, (b + 1) * qb)
            lhs = jnp.concatenate([qa[rows], qbb[rows]], axis=0)
            s_ref[p * nb + b] = lax.dot_general(lhs, kw_ref[b * qb:b * qb + kb, cols], NT_DIMS,
                                                preferred_element_type=F32)
        blk = slice(p * nb, (p + 1) * nb)
        s = s_ref[blk] + bias4
        m = jnp.max(s, axis=-1, keepdims=True)
        p_ref[blk] = jnp.exp2(s - m).astype(BF16)
        m_ref[blk] = jnp.broadcast_to(m, (nb, 2 * qb, LANES))
        for b in range(nb):
            rows = slice(b * qb, (b + 1) * qb)
            pv = jnp.dot(p_ref[p * nb + b], vw_ref[b * qb:b * qb + kb, 2 * p * LANES:(2 * p + 2) * LANES],
                         preferred_element_type=F32)
            den = pv[:, LANES:]
            o = pv[:, :LANES] / den
            lse = m_ref[p * nb + b] + jnp.log2(den)
            o_ref[0, 0, rows, cols] = jnp.where(v_head_a, o[:qb], o[qb:]).astype(BF16)
            prev = lse_ref[0, 0, rows, :] if p else jnp.zeros((qb, LANES), F32)
            lse_ref[0, 0, rows, :] = jnp.where(lse_lane == 2 * p, lse[:qb],
                                               jnp.where(lse_lane == 2 * p + 1, lse[qb:], prev))


def _attn_a_group(q, k, v, dilation, batch, seq_len):
    length = seq_len // dilation
    tq, side = min(ATT_A_TILE, length), ATT_A_SIDE
    n_tiles = length // tq
    halo_blocks = length // side
    nb = tq // ATT_A_QB
    kb = ATT_A_QB + 2 * side
    cur = pl.BlockSpec((1, 1, tq, OUT_A), lambda b, r, i: (b, r, i, 0))
    prev = pl.BlockSpec((1, 1, side, OUT_A),
                        lambda b, r, i: (b, r, jnp.maximum(i * (tq // side) - 1, 0), 0))
    nxt = pl.BlockSpec((1, 1, side, OUT_A),
                       lambda b, r, i: (b, r, jnp.minimum((i + 1) * (tq // side), halo_blocks - 1), 0))
    n_blk = PAIRS_PER_CHUNK * nb
    lse_spec = pl.BlockSpec((1, 1, tq, LANES), lambda b, r, i: (b, r, i, 0))
    return pl.pallas_call(
        functools.partial(_attn_a_kernel, seq_len=length, tq=tq),
        grid=(batch, dilation, n_tiles),
        in_specs=[cur, prev, cur, nxt, prev, cur, nxt],
        out_specs=[cur, lse_spec],
        out_shape=[jax.ShapeDtypeStruct((batch, dilation, length, OUT_A), BF16),
                   jax.ShapeDtypeStruct((batch, dilation, length, LANES), F32)],
        scratch_shapes=[pltpu.VMEM((tq + 2 * side, OUT_A), BF16),
                        pltpu.VMEM((tq + 2 * side, 2 * OUT_A), BF16),
                        pltpu.VMEM((n_blk, 2 * ATT_A_QB, kb), F32),
                        pltpu.VMEM((n_blk, 2 * ATT_A_QB, kb), BF16),
                        pltpu.VMEM((n_blk, 2 * ATT_A_QB, LANES), F32)],
        compiler_params=_params("parallel", "parallel", "arbitrary"),
        name=f"attn_a_d{dilation}",
    )(q, k, k, k, v, v, v)


def _merge_a_kernel(o0_ref, o1_ref, o2_ref, l0_ref, l1_ref, l2_ref, e_ref, y_ref, ot_ref, lt_ref):
    for slot, (o_ref, l_ref, dilation) in enumerate(
            ((o1_ref, l1_ref, DILATED_GROUPS[1][1]), (o2_ref, l2_ref, DILATED_GROUPS[2][1]))):
        rows = TOKEN_TILE // dilation
        for r in range(dilation):
            lt_ref[slot, pl.ds(r, rows, stride=dilation), :] = l_ref[0, r]
            for p in range(PAIRS_PER_CHUNK):
                cols = slice(p * LANES, (p + 1) * LANES)
                ot_ref[slot, p, pl.ds(r, rows, stride=dilation), :] = o_ref[0, r, :, cols].astype(F32)
    l0, l1, l2 = l0_ref[0, 0], lt_ref[0], lt_ref[1]
    m = jnp.maximum(jnp.maximum(l0, l1), l2)
    e0, e1, e2 = jnp.exp2(l0 - m), jnp.exp2(l1 - m), jnp.exp2(l2 - m)
    inv = 1.0 / (e0 + e1 + e2)

    def spread(w):
        hi = w.astype(BF16)
        lo = (w - hi.astype(F32)).astype(BF16)
        return (jnp.dot(hi, e_ref[...], preferred_element_type=F32)
                + jnp.dot(lo, e_ref[...], preferred_element_type=F32))

    w0, w1, w2 = spread(e0 * inv), spread(e1 * inv), spread(e2 * inv)
    for p in range(PAIRS_PER_CHUNK):
        cols = slice(p * LANES, (p + 1) * LANES)
        y_ref[:, cols] = (w0[:, cols] * o0_ref[0, 0, :, cols].astype(F32) + w1[:, cols] * ot_ref[0, p]
                          + w2[:, cols] * ot_ref[1, p]).astype(BF16)


def _merge_a(outs, lses, batch, seq_len):
    t = batch * seq_len
    tps = seq_len // TOKEN_TILE
    grp = [pl.BlockSpec((1, d, TOKEN_TILE // d, OUT_A), lambda i: (i // tps, 0, i % tps, 0))
           for _, d in DILATED_GROUPS]
    grp_lse = [pl.BlockSpec((1, d, TOKEN_TILE // d, LANES), lambda i: (i // tps, 0, i % tps, 0))
               for _, d in DILATED_GROUPS]
    spread = jnp.asarray(np.arange(LANES)[:, None] == LSE_REP * (np.arange(OUT_A)[None, :] // HEAD_DIM), BF16)
    return pl.pallas_call(
        _merge_a_kernel,
        grid=(t // TOKEN_TILE,),
        in_specs=grp + grp_lse + [pl.BlockSpec((LANES, OUT_A), lambda i: (0, 0))],
        out_specs=pl.BlockSpec((TOKEN_TILE, OUT_A), lambda i: (i, 0)),
        out_shape=jax.ShapeDtypeStruct((t, OUT_A), BF16),
        scratch_shapes=[pltpu.VMEM((N_GROUPS_A - 1, PAIRS_PER_CHUNK, TOKEN_TILE, LANES), F32),
                        pltpu.VMEM((N_GROUPS_A - 1, TOKEN_TILE, LANES), F32)],
        compiler_params=_params("parallel"),
        name="merge_a",
    )(*outs, *lses, spread)


def _attn_b_kernel(q_ref, kp_ref, kc_ref, kn_ref, vp_ref, vc_ref, vn_ref, bias_ref,
                   o_ref, kw_ref, vw_ref, s_ref, p_ref):
    w, halo, rows = GRID_W, ATT_B_HALO, ATT_B_ROWS
    n_sub = rows // ATT_B_SUB
    sub_q = ATT_B_SUB * w
    win = ATT_B_WIN * w
    ones = jnp.ones(((rows + 2 * halo) * w, LANES), BF16)
    for u in range(ATT_B_PAIRS):
        kw_ref[u, 0:halo * w] = kp_ref[0, u]
        kw_ref[u, halo * w:(halo + rows) * w] = kc_ref[0, u]
        kw_ref[u, (halo + rows) * w:] = kn_ref[0, u]
        vw_ref[u, 0:halo * w, :LANES] = vp_ref[0, u]
        vw_ref[u, halo * w:(halo + rows) * w, :LANES] = vc_ref[0, u]
        vw_ref[u, (halo + rows) * w:, :LANES] = vn_ref[0, u]
        vw_ref[u, :, LANES:] = ones

    i = pl.program_id(2)
    lo_min = jnp.where(i == 0, halo, 0)
    lo_max = jnp.where(i == pl.num_programs(2) - 1, halo, rows)
    starts = [pl.multiple_of(jnp.clip(ATT_B_SUB * sb, lo_min, lo_max) * w, LANES) for sb in range(n_sub)]
    head_a = lax.broadcasted_iota(jnp.int32, (rows * w, LANES), 1) < HEAD_DIM
    out_a = lax.broadcasted_iota(jnp.int32, (sub_q, LANES), 1) < HEAD_DIM
    zero = jnp.zeros((rows * w, LANES), BF16)
    for u in range(ATT_B_PAIRS):
        q2 = q_ref[0, u]
        qa = jnp.where(head_a, q2, zero)
        qb = jnp.where(head_a, zero, q2)
        for sb in range(n_sub):
            rs = slice(sb * sub_q, (sb + 1) * sub_q)
            lhs = jnp.concatenate([qa[rs], qb[rs]], axis=0)
            s_ref[u, sb] = lax.dot_general(lhs, kw_ref[u, pl.ds(starts[sb], win), :], NT_DIMS,
                                           preferred_element_type=F32)
    for u in range(ATT_B_PAIRS):
        s = s_ref[u] + jnp.concatenate([bias_ref[2 * u, 0], bias_ref[2 * u + 1, 0]], axis=1)
        m = jnp.max(s, axis=-1, keepdims=True)
        p_ref[u] = jnp.exp2(s - m).astype(BF16)
    for u in range(ATT_B_PAIRS):
        for sb in range(n_sub):
            pv = jnp.dot(p_ref[u, sb], vw_ref[u, pl.ds(starts[sb], win), :], preferred_element_type=F32)
            o = pv[:, :LANES] / pv[:, LANES:]
            o_ref[0, sb * sub_q:(sb + 1) * sub_q, u * LANES:(u + 1) * LANES] = (
                jnp.where(out_a, o[:sub_q], o[sub_q:]).astype(BF16))


def _attn_b(qkv, bias, batch, seq_len):
    n_rows = seq_len // GRID_W
    tile = ATT_B_ROWS * GRID_W
    halo = ATT_B_HALO * GRID_W
    n_tiles = n_rows // ATT_B_ROWS
    assert n_tiles >= 2
    per_tile = tile // halo
    halo_blocks = seq_len // halo
    n_steps = HEADS_B // 2 // ATT_B_PAIRS
    n_sub = ATT_B_ROWS // ATT_B_SUB
    sub_q = ATT_B_SUB * GRID_W
    win = ATT_B_WIN * GRID_W

    def cur(kind):
        return pl.BlockSpec((1, ATT_B_PAIRS, tile, LANES), lambda hp, b, i: (b, kind * n_steps + hp, i, 0))

    def prev(kind):
        return pl.BlockSpec((1, ATT_B_PAIRS, halo, LANES),
                            lambda hp, b, i: (b, kind * n_steps + hp, jnp.maximum(i * per_tile - 1, 0), 0))

    def nxt(kind):
        return pl.BlockSpec((1, ATT_B_PAIRS, halo, LANES),
                            lambda hp, b, i: (b, kind * n_steps + hp,
                                              jnp.minimum((i + 1) * per_tile, halo_blocks - 1), 0))

    def tile_kind(i):
        return jnp.where(i == 0, 0, jnp.where(i == n_tiles - 1, 2, 1))

    return pl.pallas_call(
        _attn_b_kernel,
        grid=(n_steps, batch, n_tiles),
        in_specs=[cur(0), prev(1), cur(1), nxt(1), prev(2), cur(2), nxt(2),
                  pl.BlockSpec((2 * ATT_B_PAIRS, 1, n_sub, sub_q, win),
                               lambda hp, b, i: (hp, tile_kind(i), 0, 0, 0))],
        out_specs=pl.BlockSpec((1, tile, ATT_B_PAIRS * LANES), lambda hp, b, i: (b, i, hp)),
        out_shape=jax.ShapeDtypeStruct((batch, seq_len, OUT_B), BF16),
        scratch_shapes=[pltpu.VMEM((ATT_B_PAIRS, tile + 2 * halo, LANES), BF16),
                        pltpu.VMEM((ATT_B_PAIRS, tile + 2 * halo, 2 * LANES), BF16),
                        pltpu.VMEM((ATT_B_PAIRS, n_sub, 2 * sub_q, win), F32),
                        pltpu.VMEM((ATT_B_PAIRS, n_sub, 2 * sub_q, win), BF16)],
        compiler_params=_params("arbitrary", "arbitrary", "arbitrary"),
        name="attn_b",
    )(qkv, qkv, qkv, qkv, qkv, qkv, qkv, bias)


def _rope_layout():
    n = np.arange(QKV_CHUNK)
    pair, lane = n // LANES, n % LANES
    second, hb, i = lane // HEAD_DIM, (lane % HEAD_DIM) // (HEAD_DIM // 2), lane % (HEAD_DIM // 2)
    head = 2 * pair + hb
    return head * HEAD_DIM + second * (HEAD_DIM // 2) + i, head


def _segment_mean_matrix(head_of_column):
    same = head_of_column[:, None] == head_of_column[None, :]
    return jnp.asarray(np.where(same, 1.0 / HEAD_DIM, 0.0), BF16)


def _rope_tables(seq_len):
    half = HEAD_DIM // 2
    inv_freq = np.float32(ROPE_THETA) ** (-np.arange(half, dtype=np.float32) / np.float32(half))
    ang = (np.arange(seq_len, dtype=np.float32)[:, None] * inv_freq[None, :]).astype(np.float64)
    cos, sin = jnp.asarray(np.cos(ang), F32), jnp.asarray(np.sin(ang), F32)
    return (jnp.concatenate([cos, cos, cos, cos], axis=-1),
            jnp.concatenate([-sin, -sin, sin, sin], axis=-1))


def _prep_a(w_qkv, q_gain, k_gain):
    src, _ = _rope_layout()
    n_qk = 2 * N_GROUPS_A
    cols = np.concatenate([c * QKV_CHUNK + src for c in range(n_qk)]
                          + [np.arange(n_qk * QKV_CHUNK, QKV_A)])
    half = HEAD_DIM // 2

    def pair_lanes(gain):
        return jnp.concatenate([gain[:half], gain[:half], gain[half:], gain[half:]])

    n_pairs = N_GROUPS_A * PAIRS_PER_CHUNK
    gains = jnp.concatenate([jnp.tile(pair_lanes(q_gain) * Q_SCALE, n_pairs),
                             jnp.tile(pair_lanes(k_gain), n_pairs),
                             jnp.ones((QKV_A - n_qk * QKV_CHUNK,), F32)])
    return w_qkv[:, cols].astype(BF16), gains.reshape(1, QKV_A)


def _prep_b(q_gain, k_gain):
    gains = jnp.concatenate([jnp.tile(q_gain * Q_SCALE, HEADS_B),
                             jnp.tile(k_gain, HEADS_B),
                             jnp.ones((OUT_B,), F32)])
    return gains.reshape(1, QKV_B)


def _bias_tile_plan():
    halo, rows, n_sub = ATT_B_HALO, ATT_B_ROWS, ATT_B_ROWS // ATT_B_SUB
    plan = {}
    for kind, (lo_min, lo_max) in enumerate(((halo, rows), (0, rows), (0, halo))):
        for sb in range(n_sub):
            ws = min(max(ATT_B_SUB * sb, lo_min), lo_max)
            for ql in range(ATT_B_SUB):
                rho = ATT_B_SUB * sb + ql
                lo = min(max(rho, lo_min), lo_max)
                for jj in range(ATT_B_WIN // 2):
                    jl = ws + 2 * jj
                    dr = jl - rho + NA_ROWS // 2 - 1
                    plan[kind, sb, ql, jj] = (dr, lo <= jl < lo + NA_ROWS, lo <= jl + 1 < lo + NA_ROWS)
    return plan


def _bias_expand_kernel(c_ref, o_ref):
    w = GRID_W
    left = lax.broadcasted_iota(jnp.int32, (w, 2 * w), 1) < w
    neg = jnp.full((w, 2 * w), NEG_INF, F32)
    for (kind, sb, ql, jj), (dr, ok_a, ok_b) in _bias_tile_plan().items():
        if ok_a or ok_b:
            tile = c_ref[0, dr + 1]
            if not ok_b:
                tile = jnp.where(left, tile, neg)
            elif not ok_a:
                tile = jnp.where(left, neg, tile)
        else:
            tile = neg
        o_ref[0, kind, sb, ql * w:(ql + 1) * w, 2 * jj * w:2 * (jj + 1) * w] = tile


def _bias_table_b(rpb):
    w, n_sub, n_dr = GRID_W, ATT_B_ROWS // ATT_B_SUB, 2 * NA_ROWS - 1
    col = np.arange(w)
    col_start = np.clip(col - NA_COLS // 2, 0, w - NA_COLS)
    col_ok = (col[None, :] >= col_start[:, None]) & (col[None, :] < col_start[:, None] + NA_COLS)
    dc = np.clip(col[None, :] - col[:, None] + NA_COLS - 1, 0, 2 * NA_COLS - 2)
    pick = jnp.asarray(np.arange(2 * NA_COLS - 1)[:, None, None] == dc[None], F32)
    by_row = jnp.einsum("hrc,cqk->hrqk", rpb * LOG2E, pick, precision=lax.Precision.HIGHEST)
    by_row = jnp.where(col_ok[None, None], by_row, NEG_INF)
    pad = jnp.full((HEADS_B, 1, w, w), NEG_INF, F32)
    by_row = jnp.concatenate([pad, by_row, pad], axis=1)
    pairs = jnp.concatenate([by_row[:, :-1], by_row[:, 1:]], axis=-1)
    return pl.pallas_call(
        _bias_expand_kernel,
        grid=(HEADS_B,),
        in_specs=[pl.BlockSpec((1, n_dr + 1, w, 2 * w), lambda h: (h, 0, 0, 0))],
        out_specs=pl.BlockSpec((1, 3, n_sub, ATT_B_SUB * w, ATT_B_WIN * w), lambda h: (h, 0, 0, 0, 0)),
        out_shape=jax.ShapeDtypeStruct((HEADS_B, 3, n_sub, ATT_B_SUB * w, ATT_B_WIN * w), F32),
        compiler_params=_params("parallel"),
        name="bias_expand",
    )(pairs)


def _mixer_a(x, batch, seq_len, mix_g, w_qkv, gains, seg, rope_tables):
    qkv = _qkv_a(x, mix_g, w_qkv, gains, seg, rope_tables, batch, seq_len)
    outs, lses = [], []
    for group, (window, dilation) in enumerate(DILATED_GROUPS):
        assert window // (2 * dilation) == ATT_A_SIDE
        o, lse = _attn_a_group(qkv[group], qkv[N_GROUPS_A + group], qkv[2 * N_GROUPS_A + group],
                               dilation, batch, seq_len)
        outs.append(o)
        lses.append(lse)
    return _merge_a(outs, lses, batch, seq_len)


def _mixer_b(x, batch, seq_len, mix_g, w_qkv, gains, bias, seg):
    qkv = _qkv_b(x, mix_g, w_qkv, gains, seg, batch, seq_len)
    return _attn_b(qkv, bias, batch, seq_len).reshape(batch * seq_len, OUT_B)


def kernel(x_prompt, x_sample, ffn1_norm, ffn1_w_in, ffn1_w_out, mix_norm, ffn2_norm, ffn2_w_in, ffn2_w_out, a_w_qkv, a_q_norm, a_k_norm, a_w_o, b_w_qkv, b_q_norm, b_k_norm, b_rpb, b_w_o):
    depth = ffn1_norm.shape[0]
    bf = lambda w: w.astype(BF16)
    ffn1_w_in, ffn1_w_out, ffn2_w_in, ffn2_w_out = map(bf, (ffn1_w_in, ffn1_w_out, ffn2_w_in, ffn2_w_out))
    a_w_o, b_w_qkv, b_w_o = map(bf, (a_w_o, b_w_qkv, b_w_o))
    seg_a = _segment_mean_matrix(_rope_layout()[1][:SEG_COLS])
    seg_b = _segment_mean_matrix(np.arange(SEG_COLS) // HEAD_DIM)
    prep_a = [_prep_a(a_w_qkv[j], a_q_norm[j], a_k_norm[j]) for j in range(a_w_qkv.shape[0])]
    prep_b = [(_prep_b(b_q_norm[j], b_k_norm[j]), _bias_table_b(b_rpb[j])) for j in range(b_w_qkv.shape[0])]
    rope_tables = _rope_tables(max(x_prompt.shape[1], x_sample.shape[1]))

    def trunk(x3):
        batch, seq_len, _ = x3.shape
        x = x3.reshape(batch * seq_len, D_MODEL)
        for i in range(depth):
            j = i // 2
            x = _ffn(x, ffn1_norm[i], ffn1_w_in, ffn1_w_out, i)
            if i % 2 == 0:
                w_qkv, gains = prep_a[j]
                attn = (_mixer_a(x, batch, seq_len, mix_norm[i], w_qkv, gains, seg_a, rope_tables), a_w_o[j])
            else:
                gains, bias = prep_b[j]
                attn = (_mixer_b(x, batch, seq_len, mix_norm[i], b_w_qkv[j], gains, bias, seg_b), b_w_o[j])
            x = _ffn(x, ffn2_norm[i], ffn2_w_in, ffn2_w_out, i, attn)
        return x.reshape(batch, seq_len, D_MODEL)

    return (trunk(x_prompt), trunk(x_sample))
```

```python
import functools

import jax
import jax.numpy as jnp
import numpy as np
from jax import lax
from jax.experimental import pallas as pl
from jax.experimental.pallas import tpu as pltpu

D_MODEL = 1024
HEAD_DIM = 64
D_FF = 2816
RMS_EPS = 1e-6
ROPE_THETA = 10000.0
NEG_INF = -1e30
LOG2E = 1.4426950408889634
DILATED_GROUPS = ((128, 1), (512, 4), (2048, 16))
HEADS_A = 8
N_GROUPS_A = 3
QKV_A = 3 * N_GROUPS_A * HEADS_A * HEAD_DIM
OUT_A = HEADS_A * HEAD_DIM
HEADS_B = 16
QKV_B = 3 * HEADS_B * HEAD_DIM
OUT_B = HEADS_B * HEAD_DIM
GRID_W = 64
NA_ROWS = 8
NA_COLS = 16

LANES = 128
VMEM_LIMIT_BYTES = 56 * 1024 * 1024

TOKEN_TILE = 1024
QKV_TILE = 1024
FFN_TILE = 1024
FF_CHUNK = 256
SEG_COLS = 256
QKV_CHUNK = 512
PAIRS_PER_CHUNK = QKV_CHUNK // LANES
ATT_A_TILE = 512
ATT_A_QB = 128
ATT_A_SIDE = 64
LSE_REP = LANES // HEADS_A
ATT_B_ROWS = 8
ATT_B_HALO = 4
ATT_B_PAIRS = 4
ATT_B_SUB = 2
ATT_B_WIN = NA_ROWS + ATT_B_SUB

BF16 = jnp.bfloat16
F32 = jnp.float32
NT_DIMS = (((1,), (1,)), ((), ()))
Q_SCALE = HEAD_DIM ** -0.5 * LOG2E


def _params(*semantics):
    return pltpu.CompilerParams(dimension_semantics=semantics,
                                vmem_limit_bytes=VMEM_LIMIT_BYTES)


def _rms_rows(x, g):
    ms = jnp.mean(x * x, axis=-1, keepdims=True)
    return (x * lax.rsqrt(ms + RMS_EPS)) * g


def _ffn_kernel(*refs, fused_proj):
    if fused_proj:
        x_ref, a_ref, wa_ref, g_ref, win_ref, wout_ref, o_ref, hn_ref = refs
        o_ref[...] = x_ref[...] + jnp.dot(a_ref[...], wa_ref[...], preferred_element_type=F32)
    else:
        x_ref, g_ref, win_ref, wout_ref, o_ref, hn_ref = refs
        o_ref[...] = x_ref[...]
    hn_ref[...] = _rms_rows(o_ref[...], g_ref[...]).astype(BF16)
    for c0 in range(0, D_FF, FF_CHUNK):
        width = min(FF_CHUNK, D_FF - c0)
        h = hn_ref[...]
        gate = jnp.dot(h, win_ref[:, c0:c0 + width], preferred_element_type=F32)
        up = jnp.dot(h, win_ref[:, D_FF + c0:D_FF + c0 + width], preferred_element_type=F32)
        a = (gate * jax.nn.sigmoid(gate) * up).astype(BF16)
        o_ref[...] += 0.5 * jnp.dot(a, wout_ref[c0:c0 + width, :], preferred_element_type=F32)


def _ffn(x, g, w_in, w_out, layer, attn=None):
    t = x.shape[0]
    const = lambda i: (0, 0)
    this_layer = lambda i: (layer, 0, 0)
    resident = pl.Buffered(1)
    row = pl.BlockSpec((FFN_TILE, D_MODEL), lambda i: (i, 0))
    in_specs, args = [row], [x]
    if attn is not None:
        a, w_a = attn
        in_specs += [pl.BlockSpec((FFN_TILE, a.shape[1]), lambda i: (i, 0)),
                     pl.BlockSpec(w_a.shape, const, pipeline_mode=resident)]
        args += [a, w_a]
    in_specs += [pl.BlockSpec((1, D_MODEL), const),
                 pl.BlockSpec((None, D_MODEL, 2 * D_FF), this_layer, pipeline_mode=resident),
                 pl.BlockSpec((None, D_FF, D_MODEL), this_layer, pipeline_mode=resident)]
    args += [g.reshape(1, D_MODEL), w_in, w_out]
    return pl.pallas_call(
        functools.partial(_ffn_kernel, fused_proj=attn is not None),
        grid=(t // FFN_TILE,),
        in_specs=in_specs,
        out_specs=row,
        out_shape=jax.ShapeDtypeStruct((t, D_MODEL), F32),
        scratch_shapes=[pltpu.VMEM((FFN_TILE, D_MODEL), BF16)],
        compiler_params=_params("parallel"),
        name="ffn_proj" if attn is not None else "ffn",
    )(*args)


def _project_chunk(hn_ref, w_ref, gain_ref, seg_ref, c, normed):
    cols = slice(c * QKV_CHUNK, (c + 1) * QKV_CHUNK)
    y = jnp.dot(hn_ref[...], w_ref[:, cols], preferred_element_type=F32)
    if normed:
        y2 = (y * y).astype(BF16)
        ms = jnp.concatenate(
            [jnp.dot(y2[:, s0:s0 + SEG_COLS], seg_ref[...], preferred_element_type=F32)
             for s0 in range(0, QKV_CHUNK, SEG_COLS)], axis=-1)
        y = (y * lax.rsqrt(ms + RMS_EPS)) * gain_ref[:, cols]
    return y


def _qkv_a_kernel(x_ref, g_ref, w_ref, gain_ref, seg_ref, cos_ref, sin_ref, *rest):
    n_chunks = 3 * N_GROUPS_A
    out_refs, (hn_ref, ys_ref) = rest[:n_chunks], rest[n_chunks:]
    hn_ref[...] = _rms_rows(x_ref[...], g_ref[...]).astype(BF16)
    cos, sin = cos_ref[...], sin_ref[...]
    for c in range(n_chunks):
        kind, group = divmod(c, N_GROUPS_A)
        dilation = DILATED_GROUPS[group][1]
        normed = kind < 2
        y = _project_chunk(hn_ref, w_ref, gain_ref, seg_ref, c, normed)
        if normed:
            y = jnp.concatenate(
                [y[:, p * LANES:(p + 1) * LANES] * cos
                 + pltpu.roll(y[:, p * LANES:(p + 1) * LANES], LANES // 2, axis=1) * sin
                 for p in range(PAIRS_PER_CHUNK)], axis=-1)
        if dilation == 1:
            out_refs[c][0, 0] = y.astype(BF16)
        else:
            slot = kind * (N_GROUPS_A - 1) + group - 1
            rows = QKV_TILE // dilation
            for p in range(PAIRS_PER_CHUNK):
                cols = slice(p * LANES, (p + 1) * LANES)
                ys_ref[slot, p] = y[:, cols]
                for r in range(dilation):
                    out_refs[c][0, r, :, cols] = ys_ref[slot, p, pl.ds(r, rows, stride=dilation), :].astype(BF16)


def _qkv_a(x, g, w, gains, seg, rope_tables, batch, seq_len):
    t = x.shape[0]
    tps = seq_len // QKV_TILE
    const = lambda i: (0, 0)
    in_specs = [
        pl.BlockSpec((QKV_TILE, D_MODEL), lambda i: (i, 0)),
        pl.BlockSpec((1, D_MODEL), const),
        pl.BlockSpec((D_MODEL, QKV_A), const, pipeline_mode=pl.Buffered(1)),
        pl.BlockSpec((1, QKV_A), const),
        pl.BlockSpec((SEG_COLS, SEG_COLS), const),
        pl.BlockSpec((QKV_TILE, LANES), lambda i: (i % tps, 0)),
        pl.BlockSpec((QKV_TILE, LANES), lambda i: (i % tps, 0)),
    ]
    out_specs, out_shapes = [], []
    for c in range(3 * N_GROUPS_A):
        d = DILATED_GROUPS[c % N_GROUPS_A][1]
        out_specs.append(pl.BlockSpec((1, d, QKV_TILE // d, QKV_CHUNK),
                                      lambda i: (i // tps, 0, i % tps, 0)))
        out_shapes.append(jax.ShapeDtypeStruct((batch, d, seq_len // d, QKV_CHUNK), BF16))
    return pl.pallas_call(
        _qkv_a_kernel,
        grid=(t // QKV_TILE,),
        in_specs=in_specs,
        out_specs=out_specs,
        out_shape=out_shapes,
        scratch_shapes=[pltpu.VMEM((QKV_TILE, D_MODEL), BF16),
                        pltpu.VMEM((3 * (N_GROUPS_A - 1), PAIRS_PER_CHUNK, QKV_TILE, LANES), F32)],
        compiler_params=_params("parallel"),
        name="qkv_a",
    )(x, g.reshape(1, D_MODEL), w, gains, seg, *rope_tables)


def _qkv_b_kernel(x_ref, g_ref, w_ref, gain_ref, seg_ref, o_ref, hn_ref):
    hn_ref[...] = _rms_rows(x_ref[...], g_ref[...]).astype(BF16)
    for c in range(QKV_B // QKV_CHUNK):
        y = _project_chunk(hn_ref, w_ref, gain_ref, seg_ref, c, c < 2 * OUT_B // QKV_CHUNK)
        for p in range(PAIRS_PER_CHUNK):
            o_ref[0, c * PAIRS_PER_CHUNK + p] = y[:, p * LANES:(p + 1) * LANES].astype(BF16)


def _qkv_b(x, g, w, gains, seg, batch, seq_len):
    t = x.shape[0]
    tps = seq_len // QKV_TILE
    n_pairs = QKV_B // LANES
    const = lambda i: (0, 0)
    return pl.pallas_call(
        _qkv_b_kernel,
        grid=(t // QKV_TILE,),
        in_specs=[
            pl.BlockSpec((QKV_TILE, D_MODEL), lambda i: (i, 0)),
            pl.BlockSpec((1, D_MODEL), const),
            pl.BlockSpec((D_MODEL, QKV_B), const, pipeline_mode=pl.Buffered(1)),
            pl.BlockSpec((1, QKV_B), const),
            pl.BlockSpec((SEG_COLS, SEG_COLS), const),
        ],
        out_specs=pl.BlockSpec((1, n_pairs, QKV_TILE, LANES), lambda i: (i // tps, 0, i % tps, 0)),
        out_shape=jax.ShapeDtypeStruct((batch, n_pairs, seq_len, LANES), BF16),
        scratch_shapes=[pltpu.VMEM((QKV_TILE, D_MODEL), BF16)],
        compiler_params=_params("parallel"),
        name="qkv_b",
    )(x, g.reshape(1, D_MODEL), w, gains, seg)


def _attn_a_kernel(q_ref, kp_ref, kc_ref, kn_ref, vp_ref, vc_ref, vn_ref,
                   o_ref, lse_ref, kw_ref, vw_ref, s_ref, p_ref, m_ref, *, seq_len, tq):
    side, qb = ATT_A_SIDE, ATT_A_QB
    kb = qb + 2 * side
    nb = tq // qb
    kw_ref[0:side] = kp_ref[0, 0]
    kw_ref[side:side + tq] = kc_ref[0, 0]
    kw_ref[side + tq:] = kn_ref[0, 0]
    ones = jnp.ones((tq + 2 * side, LANES), BF16)
    for p in range(PAIRS_PER_CHUNK):
        src = slice(p * LANES, (p + 1) * LANES)
        dst = slice(2 * p * LANES, (2 * p + 1) * LANES)
        vw_ref[0:side, dst] = vp_ref[0, 0, :, src]
        vw_ref[side:side + tq, dst] = vc_ref[0, 0, :, src]
        vw_ref[side + tq:, dst] = vn_ref[0, 0, :, src]
        vw_ref[:, (2 * p + 1) * LANES:(2 * p + 2) * LANES] = ones

    q_start = pl.program_id(2) * tq
    row = lax.broadcasted_iota(jnp.int32, (qb, kb), 0)
    col = lax.broadcasted_iota(jnp.int32, (qb, kb), 1)
    band = (col >= row) & (col - row <= 2 * side)
    biases = []
    for b in range(nb):
        kpos = q_start + b * qb + col - side
        bias = jnp.where(band & (kpos >= 0) & (kpos < seq_len), 0.0, NEG_INF).astype(F32)
        biases.append(jnp.concatenate([bias, bias], axis=0))
    bias4 = jnp.stack(biases)

    lane = lax.broadcasted_iota(jnp.int32, (tq, LANES), 1)
    q_head_a = (lane % HEAD_DIM) < (HEAD_DIM // 2)
    v_head_a = lax.broadcasted_iota(jnp.int32, (qb, LANES), 1) < HEAD_DIM
    lse_lane = lax.broadcasted_iota(jnp.int32, (qb, LANES), 1) // LSE_REP
    zero = jnp.zeros((tq, LANES), BF16)

    for p in range(PAIRS_PER_CHUNK):
        cols = slice(p * LANES, (p + 1) * LANES)
        q2 = q_ref[0, 0, :, cols]
        qa = jnp.where(q_head_a, q2, zero)
        qbb = jnp.where(q_head_a, zero, q2)
        for b in range(nb):
            rows = slice(b * qb, (b + 1) * qb)
            lhs = jnp.concatenate([qa[rows], qbb[rows]], axis=0)
            s_ref[p * nb + b] = lax.dot_general(lhs, kw_ref[b * qb:b * qb + kb, cols], NT_DIMS,
                                                preferred_element_type=F32)
        blk = slice(p * nb, (p + 1) * nb)
        s = s_ref[blk] + bias4
        m = jnp.max(s, axis=-1, keepdims=True)
        p_ref[blk] = jnp.exp2(s - m).astype(BF16)
        m_ref[blk] = jnp.broadcast_to(m, (nb, 2 * qb, LANES))
        for b in range(nb):
            rows = slice(b * qb, (b + 1) * qb)
            pv = jnp.dot(p_ref[p * nb + b], vw_ref[b * qb:b * qb + kb, 2 * p * LANES:(2 * p + 2) * LANES],
                         preferred_element_type=F32)
            den = pv[:, LANES:]
            o = pv[:, :LANES] / den
            lse = m_ref[p * nb + b] + jnp.log2(den)
            o_ref[0, 0, rows, cols] = jnp.where(v_head_a, o[:qb], o[qb:]).astype(BF16)
            prev = lse_ref[0, 0, rows, :] if p else jnp.zeros((qb, LANES), F32)
            lse_ref[0, 0, rows, :] = jnp.where(lse_lane == 2 * p, lse[:qb],
                                               jnp.where(lse_lane == 2 * p + 1, lse[qb:], prev))


def _attn_a_group(q, k, v, dilation, batch, seq_len):
    length = seq_len // dilation
    tq, side = min(ATT_A_TILE, length), ATT_A_SIDE
    n_tiles = length // tq
    halo_blocks = length // side
    nb = tq // ATT_A_QB
    kb = ATT_A_QB + 2 * side
    cur = pl.BlockSpec((1, 1, tq, OUT_A), lambda b, r, i: (b, r, i, 0))
    prev = pl.BlockSpec((1, 1, side, OUT_A),
                        lambda b, r, i: (b, r, jnp.maximum(i * (tq // side) - 1, 0), 0))
    nxt = pl.BlockSpec((1, 1, side, OUT_A),
                       lambda b, r, i: (b, r, jnp.minimum((i + 1) * (tq // side), halo_blocks - 1), 0))
    n_blk = PAIRS_PER_CHUNK * nb
    lse_spec = pl.BlockSpec((1, 1, tq, LANES), lambda b, r, i: (b, r, i, 0))
    return pl.pallas_call(
        functools.partial(_attn_a_kernel, seq_len=length, tq=tq),
        grid=(batch, dilation, n_tiles),
        in_specs=[cur, prev, cur, nxt, prev, cur, nxt],
        out_specs=[cur, lse_spec],
        out_shape=[jax.ShapeDtypeStruct((batch, dilation, length, OUT_A), BF16),
                   jax.ShapeDtypeStruct((batch, dilation, length, LANES), F32)],
        scratch_shapes=[pltpu.VMEM((tq + 2 * side, OUT_A), BF16),
                        pltpu.VMEM((tq + 2 * side, 2 * OUT_A), BF16),
                        pltpu.VMEM((n_blk, 2 * ATT_A_QB, kb), F32),
                        pltpu.VMEM((n_blk, 2 * ATT_A_QB, kb), BF16),
                        pltpu.VMEM((n_blk, 2 * ATT_A_QB, LANES), F32)],
        compiler_params=_params("parallel", "parallel", "arbitrary"),
        name=f"attn_a_d{dilation}",
    )(q, k, k, k, v, v, v)


def _merge_a_kernel(o0_ref, o1_ref, o2_ref, l0_ref, l1_ref, l2_ref, e_ref, y_ref, ot_ref, lt_ref):
    for slot, (o_ref, l_ref, dilation) in enumerate(
            ((o1_ref, l1_ref, DILATED_GROUPS[1][1]), (o2_ref, l2_ref, DILATED_GROUPS[2][1]))):
        rows = TOKEN_TILE // dilation
        for r in range(dilation):
            lt_ref[slot, pl.ds(r, rows, stride=dilation), :] = l_ref[0, r]
            for p in range(PAIRS_PER_CHUNK):
                cols = slice(p * LANES, (p + 1) * LANES)
                ot_ref[slot, p, pl.ds(r, rows, stride=dilation), :] = o_ref[0, r, :, cols].astype(F32)
    l0, l1, l2 = l0_ref[0, 0], lt_ref[0], lt_ref[1]
    m = jnp.maximum(jnp.maximum(l0, l1), l2)
    e0, e1, e2 = jnp.exp2(l0 - m), jnp.exp2(l1 - m), jnp.exp2(l2 - m)
    inv = 1.0 / (e0 + e1 + e2)

    def spread(w):
        hi = w.astype(BF16)
        lo = (w - hi.astype(F32)).astype(BF16)
        return (jnp.dot(hi, e_ref[...], preferred_element_type=F32)
                + jnp.dot(lo, e_ref[...], preferred_element_type=F32))

    w0, w1, w2 = spread(e0 * inv), spread(e1 * inv), spread(e2 * inv)
    for p in range(PAIRS_PER_CHUNK):
        cols = slice(p * LANES, (p + 1) * LANES)
        y_ref[:, cols] = (w0[:, cols] * o0_ref[0, 0, :, cols].astype(F32) + w1[:, cols] * ot_ref[0, p]
                          + w2[:, cols] * ot_ref[1, p]).astype(BF16)


def _merge_a(outs, lses, batch, seq_len):
    t = batch * seq_len
    tps = seq_len // TOKEN_TILE
    grp = [pl.BlockSpec((1, d, TOKEN_TILE // d, OUT_A), lambda i: (i // tps, 0, i % tps, 0))
           for _, d in DILATED_GROUPS]
    grp_lse = [pl.BlockSpec((1, d, TOKEN_TILE // d, LANES), lambda i: (i // tps, 0, i % tps, 0))
               for _, d in DILATED_GROUPS]
    spread = jnp.asarray(np.arange(LANES)[:, None] == LSE_REP * (np.arange(OUT_A)[None, :] // HEAD_DIM), BF16)
    return pl.pallas_call(
        _merge_a_kernel,
        grid=(t // TOKEN_TILE,),
        in_specs=grp + grp_lse + [pl.BlockSpec((LANES, OUT_A), lambda i: (0, 0))],
        out_specs=pl.BlockSpec((TOKEN_TILE, OUT_A), lambda i: (i, 0)),
        out_shape=jax.ShapeDtypeStruct((t, OUT_A), BF16),
        scratch_shapes=[pltpu.VMEM((N_GROUPS_A - 1, PAIRS_PER_CHUNK, TOKEN_TILE, LANES), F32),
                        pltpu.VMEM((N_GROUPS_A - 1, TOKEN_TILE, LANES), F32)],
        compiler_params=_params("parallel"),
        name="merge_a",
    )(*outs, *lses, spread)


def _attn_b_kernel(q_ref, kp_ref, kc_ref, kn_ref, vp_ref, vc_ref, vn_ref, bias_ref,
                   o_ref, kw_ref, vw_ref, s_ref, p_ref):
    w, halo, rows = GRID_W, ATT_B_HALO, ATT_B_ROWS
    n_sub = rows // ATT_B_SUB
    sub_q = ATT_B_SUB * w
    win = ATT_B_WIN * w
    ones = jnp.ones(((rows + 2 * halo) * w, LANES), BF16)
    for u in range(ATT_B_PAIRS):
        kw_ref[u, 0:halo * w] = kp_ref[0, u]
        kw_ref[u, halo * w:(halo + rows) * w] = kc_ref[0, u]
        kw_ref[u, (halo + rows) * w:] = kn_ref[0, u]
        vw_ref[u, 0:halo * w, :LANES] = vp_ref[0, u]
        vw_ref[u, halo * w:(halo + rows) * w, :LANES] = vc_ref[0, u]
        vw_ref[u, (halo + rows) * w:, :LANES] = vn_ref[0, u]
        vw_ref[u, :, LANES:] = ones

    i = pl.program_id(2)
    lo_min = jnp.where(i == 0, halo, 0)
    lo_max = jnp.where(i == pl.num_programs(2) - 1, halo, rows)
    starts = [pl.multiple_of(jnp.clip(ATT_B_SUB * sb, lo_min, lo_max) * w, LANES) for sb in range(n_sub)]
    head_a = lax.broadcasted_iota(jnp.int32, (rows * w, LANES), 1) < HEAD_DIM
    out_a = lax.broadcasted_iota(jnp.int32, (sub_q, LANES), 1) < HEAD_DIM
    zero = jnp.zeros((rows * w, LANES), BF16)
    for u in range(ATT_B_PAIRS):
        q2 = q_ref[0, u]
        qa = jnp.where(head_a, q2, zero)
        qb = jnp.where(head_a, zero, q2)
        for sb in range(n_sub):
            rs = slice(sb * sub_q, (sb + 1) * sub_q)
            lhs = jnp.concatenate([qa[rs], qb[rs]], axis=0)
            s_ref[u, sb] = lax.dot_general(lhs, kw_ref[u, pl.ds(starts[sb], win), :], NT_DIMS,
                                           preferred_element_type=F32)
    for u in range(ATT_B_PAIRS):
        s = s_ref[u] + jnp.concatenate([bias_ref[2 * u, 0], bias_ref[2 * u + 1, 0]], axis=1)
        m = jnp.max(s, axis=-1, keepdims=True)
        p_ref[u] = jnp.exp2(s - m).astype(BF16)
    for u in range(ATT_B_PAIRS):
        for sb in range(n_sub):
            pv = jnp.dot(p_ref[u, sb], vw_ref[u, pl.ds(starts[sb], win), :], preferred_element_type=F32)
            o = pv[:, :LANES] / pv[:, LANES:]
            o_ref[0, sb * sub_q:(sb + 1) * sub_q, u * LANES:(u + 1) * LANES] = (
                jnp.where(out_a, o[:sub_q], o[sub_q:]).astype(BF16))


def _attn_b(qkv, bias, batch, seq_len):
    n_rows = seq_len // GRID_W
    tile = ATT_B_ROWS * GRID_W
    halo = ATT_B_HALO * GRID_W
    n_tiles = n_rows // ATT_B_ROWS
    assert n_tiles >= 2
    per_tile = tile // halo
    halo_blocks = seq_len // halo
    n_steps = HEADS_B // 2 // ATT_B_PAIRS
    n_sub = ATT_B_ROWS // ATT_B_SUB
    sub_q = ATT_B_SUB * GRID_W
    win = ATT_B_WIN * GRID_W

    def cur(kind):
        return pl.BlockSpec((1, ATT_B_PAIRS, tile, LANES), lambda hp, b, i: (b, kind * n_steps + hp, i, 0))

    def prev(kind):
        return pl.BlockSpec((1, ATT_B_PAIRS, halo, LANES),
                            lambda hp, b, i: (b, kind * n_steps + hp, jnp.maximum(i * per_tile - 1, 0), 0))

    def nxt(kind):
        return pl.BlockSpec((1, ATT_B_PAIRS, halo, LANES),
                            lambda hp, b, i: (b, kind * n_steps + hp,
                                              jnp.minimum((i + 1) * per_tile, halo_blocks - 1), 0))

    def tile_kind(i):
        return jnp.where(i == 0, 0, jnp.where(i == n_tiles - 1, 2, 1))

    return pl.pallas_call(
        _attn_b_kernel,
        grid=(n_steps, batch, n_tiles),
        in_specs=[cur(0), prev(1), cur(1), nxt(1), prev(2), cur(2), nxt(2),
                  pl.BlockSpec((2 * ATT_B_PAIRS, 1, n_sub, sub_q, win),
                               lambda hp, b, i: (hp, tile_kind(i), 0, 0, 0))],
        out_specs=pl.BlockSpec((1, tile, ATT_B_PAIRS * LANES), lambda hp, b, i: (b, i, hp)),
        out_shape=jax.ShapeDtypeStruct((batch, seq_len, OUT_B), BF16),
        scratch_shapes=[pltpu.VMEM((ATT_B_PAIRS, tile + 2 * halo, LANES), BF16),
                        pltpu.VMEM((ATT_B_PAIRS, tile + 2 * halo, 2 * LANES), BF16),
                        pltpu.VMEM((ATT_B_PAIRS, n_sub, 2 * sub_q, win), F32),
                        pltpu.VMEM((ATT_B_PAIRS, n_sub, 2 * sub_q, win), BF16)],
        compiler_params=_params("arbitrary", "arbitrary", "arbitrary"),
        name="attn_b",
    )(qkv, qkv, qkv, qkv, qkv, qkv, qkv, bias)


def _rope_layout():
    n = np.arange(QKV_CHUNK)
    pair, lane = n // LANES, n % LANES
    second, hb, i = lane // HEAD_DIM, (lane % HEAD_DIM) // (HEAD_DIM // 2), lane % (HEAD_DIM // 2)
    head = 2 * pair + hb
    return head * HEAD_DIM + second * (HEAD_DIM // 2) + i, head


def _segment_mean_matrix(head_of_column):
    same = head_of_column[:, None] == head_of_column[None, :]
    return jnp.asarray(np.where(same, 1.0 / HEAD_DIM, 0.0), BF16)


def _rope_tables(seq_len):
    half = HEAD_DIM // 2
    inv_freq = np.float32(ROPE_THETA) ** (-np.arange(half, dtype=np.float32) / np.float32(half))
    ang = (np.arange(seq_len, dtype=np.float32)[:, None] * inv_freq[None, :]).astype(np.float64)
    cos, sin = jnp.asarray(np.cos(ang), F32), jnp.asarray(np.sin(ang), F32)
    return (jnp.concatenate([cos, cos, cos, cos], axis=-1),
            jnp.concatenate([-sin, -sin, sin, sin], axis=-1))


def _prep_a(w_qkv, q_gain, k_gain):
    src, _ = _rope_layout()
    n_qk = 2 * N_GROUPS_A
    cols = np.concatenate([c * QKV_CHUNK + src for c in range(n_qk)]
                          + [np.arange(n_qk * QKV_CHUNK, QKV_A)])
    half = HEAD_DIM // 2

    def pair_lanes(gain):
        return jnp.concatenate([gain[:half], gain[:half], gain[half:], gain[half:]])

    n_pairs = N_GROUPS_A * PAIRS_PER_CHUNK
    gains = jnp.concatenate([jnp.tile(pair_lanes(q_gain) * Q_SCALE, n_pairs),
                             jnp.tile(pair_lanes(k_gain), n_pairs),
                             jnp.ones((QKV_A - n_qk * QKV_CHUNK,), F32)])
    return w_qkv[:, cols].astype(BF16), gains.reshape(1, QKV_A)


def _prep_b(q_gain, k_gain):
    gains = jnp.concatenate([jnp.tile(q_gain * Q_SCALE, HEADS_B),
                             jnp.tile(k_gain, HEADS_B),
                             jnp.ones((OUT_B,), F32)])
    return gains.reshape(1, QKV_B)


def _bias_tile_plan():
    halo, rows, n_sub = ATT_B_HALO, ATT_B_ROWS, ATT_B_ROWS // ATT_B_SUB
    plan = {}
    for kind, (lo_min, lo_max) in enumerate(((halo, rows), (0, rows), (0, halo))):
        for sb in range(n_sub):
            ws = min(max(ATT_B_SUB * sb, lo_min), lo_max)
            for ql in range(ATT_B_SUB):
                rho = ATT_B_SUB * sb + ql
                lo = min(max(rho, lo_min), lo_max)
                for jj in range(ATT_B_WIN // 2):
                    jl = ws + 2 * jj
                    dr = jl - rho + NA_ROWS // 2 - 1
                    plan[kind, sb, ql, jj] = (dr, lo <= jl < lo + NA_ROWS, lo <= jl + 1 < lo + NA_ROWS)
    return plan


def _bias_expand_kernel(c_ref, o_ref):
    w = GRID_W
    left = lax.broadcasted_iota(jnp.int32, (w, 2 * w), 1) < w
    neg = jnp.full((w, 2 * w), NEG_INF, F32)
    for (kind, sb, ql, jj), (dr, ok_a, ok_b) in _bias_tile_plan().items():
        if ok_a or ok_b:
            tile = c_ref[0, dr + 1]
            if not ok_b:
                tile = jnp.where(left, tile, neg)
            elif not ok_a:
                tile = jnp.where(left, neg, tile)
        else:
            tile = neg
        o_ref[0, kind, sb, ql * w:(ql + 1) * w, 2 * jj * w:2 * (jj + 1) * w] = tile


def _bias_table_b(rpb):
    w, n_sub, n_dr = GRID_W, ATT_B_ROWS // ATT_B_SUB, 2 * NA_ROWS - 1
    col = np.arange(w)
    col_start = np.clip(col - NA_COLS // 2, 0, w - NA_COLS)
    col_ok = (col[None, :] >= col_start[:, None]) & (col[None, :] < col_start[:, None] + NA_COLS)
    dc = np.clip(col[None, :] - col[:, None] + NA_COLS - 1, 0, 2 * NA_COLS - 2)
    pick = jnp.asarray(np.arange(2 * NA_COLS - 1)[:, None, None] == dc[None], F32)
    by_row = jnp.einsum("hrc,cqk->hrqk", rpb * LOG2E, pick, precision=lax.Precision.HIGHEST)
    by_row = jnp.where(col_ok[None, None], by_row, NEG_INF)
    pad = jnp.full((HEADS_B, 1, w, w), NEG_INF, F32)
    by_row = jnp.concatenate([pad, by_row, pad], axis=1)
    pairs = jnp.concatenate([by_row[:, :-1], by_row[:, 1:]], axis=-1)
    return pl.pallas_call(
        _bias_expand_kernel,
        grid=(HEADS_B,),
        in_specs=[pl.BlockSpec((1, n_dr + 1, w, 2 * w), lambda h: (h, 0, 0, 0))],
        out_specs=pl.BlockSpec((1, 3, n_sub, ATT_B_SUB * w, ATT_B_WIN * w), lambda h: (h, 0, 0, 0, 0)),
        out_shape=jax.ShapeDtypeStruct((HEADS_B, 3, n_sub, ATT_B_SUB * w, ATT_B_WIN * w), F32),
        compiler_params=_params("parallel"),
        name="bias_expand",
    )(pairs)


def _mixer_a(x, batch, seq_len, mix_g, w_qkv, gains, seg, rope_tables):
    qkv = _qkv_a(x, mix_g, w_qkv, gains, seg, rope_tables, batch, seq_len)
    outs, lses = [], []
    for group, (window, dilation) in enumerate(DILATED_GROUPS):
        assert window // (2 * dilation) == ATT_A_SIDE
        o, lse = _attn_a_group(qkv[group], qkv[N_GROUPS_A + group], qkv[2 * N_GROUPS_A + group],
                               dilation, batch, seq_len)
        outs.append(o)
        lses.append(lse)
    return _merge_a(outs, lses, batch, seq_len)


def _mixer_b(x, batch, seq_len, mix_g, w_qkv, gains, bias, seg):
    qkv = _qkv_b(x, mix_g, w_qkv, gains, seg, batch, seq_len)
    return _attn_b(qkv, bias, batch, seq_len).reshape(batch * seq_len, OUT_B)


def kernel(x_prompt, x_sample, ffn1_norm, ffn1_w_in, ffn1_w_out, mix_norm, ffn2_norm, ffn2_w_in, ffn2_w_out, a_w_qkv, a_q_norm, a_k_norm, a_w_o, b_w_qkv, b_q_norm, b_k_norm, b_rpb, b_w_o):
    depth = ffn1_norm.shape[0]
    bf = lambda w: w.astype(BF16)
    ffn1_w_in, ffn1_w_out, ffn2_w_in, ffn2_w_out = map(bf, (ffn1_w_in, ffn1_w_out, ffn2_w_in, ffn2_w_out))
    a_w_o, b_w_qkv, b_w_o = map(bf, (a_w_o, b_w_qkv, b_w_o))
    seg_a = _segment_mean_matrix(_rope_layout()[1][:SEG_COLS])
    seg_b = _segment_mean_matrix(np.arange(SEG_COLS) // HEAD_DIM)
    prep_a = [_prep_a(a_w_qkv[j], a_q_norm[j], a_k_norm[j]) for j in range(a_w_qkv.shape[0])]
    prep_b = [(_prep_b(b_q_norm[j], b_k_norm[j]), _bias_table_b(b_rpb[j])) for j in range(b_w_qkv.shape[0])]
    rope_tables = _rope_tables(max(x_prompt.shape[1], x_sample.shape[1]))

    def trunk(x3):
        batch, seq_len, _ = x3.shape
        x = x3.reshape(batch * seq_len, D_MODEL)
        for i in range(depth):
            j = i // 2
            x = _ffn(x, ffn1_norm[i], ffn1_w_in, ffn1_w_out, i)
            if i % 2 == 0:
                w_qkv, gains = prep_a[j]
                attn = (_mixer_a(x, batch, seq_len, mix_norm[i], w_qkv, gains, seg_a, rope_tables), a_w_o[j])
            else:
                gains, bias = prep_b[j]
                attn = (_mixer_b(x, batch, seq_len, mix_norm[i], b_w_qkv[j], gains, bias, seg_b), b_w_o[j])
            x = _ffn(x, ffn2_norm[i], ffn2_w_in, ffn2_w_out, i, attn)
        return x.reshape(batch, seq_len, D_MODEL)

    return (trunk(x_prompt), trunk(x_sample))
```

```python
import functools

import jax
import jax.numpy as jnp
import numpy as np
from jax import lax
from jax.experimental import pallas as pl
from jax.experimental.pallas import tpu as pltpu

D_MODEL = 1024
HEAD_DIM = 64
D_FF = 2816
RMS_EPS = 1e-6
ROPE_THETA = 10000.0
NEG_INF = -1e30
LOG2E = 1.4426950408889634
DILATED_GROUPS = ((128, 1), (512, 4), (2048, 16))
HEADS_A = 8
N_GROUPS_A = 3
QKV_A = 3 * N_GROUPS_A * HEADS_A * HEAD_DIM
OUT_A = HEADS_A * HEAD_DIM
HEADS_B = 16
QKV_B = 3 * HEADS_B * HEAD_DIM
OUT_B = HEADS_B * HEAD_DIM
GRID_W = 64
NA_ROWS = 8
NA_COLS = 16

LANES = 128
VMEM_LIMIT_BYTES = 56 * 1024 * 1024

TOKEN_TILE = 1024
QKV_TILE = 1024
FFN_TILE = 1024
FF_CHUNK = 256
SEG_COLS = 256
QKV_CHUNK = 512
PAIRS_PER_CHUNK = QKV_CHUNK // LANES
ATT_A_TILE = 512
ATT_A_QB = 128
KVQ_OFFSET = (2 * QKV_CHUNK, 0, QKV_CHUNK)
ATT_A_SIDE = 64
LSE_REP = LANES // HEADS_A
ATT_B_ROWS = 8
ATT_B_HALO = 4
ATT_B_PAIRS = 4
ATT_B_SUB = 2
ATT_B_WIN = NA_ROWS + ATT_B_SUB

BF16 = jnp.bfloat16
F32 = jnp.float32
NT_DIMS = (((1,), (1,)), ((), ()))
Q_SCALE = HEAD_DIM ** -0.5 * LOG2E


def _params(*semantics):
    return pltpu.CompilerParams(dimension_semantics=semantics,
                                vmem_limit_bytes=VMEM_LIMIT_BYTES)


def _rms_rows(x, g):
    ms = jnp.mean(x * x, axis=-1, keepdims=True)
    return (x * lax.rsqrt(ms + RMS_EPS)) * g


def _ffn_kernel(*refs, fused_proj):
    if fused_proj:
        x_ref, a_ref, wa_ref, g_ref, win_ref, wout_ref, o_ref, hn_ref = refs
        o_ref[...] = x_ref[...] + jnp.dot(a_ref[...], wa_ref[...], preferred_element_type=F32)
    else:
        x_ref, g_ref, win_ref, wout_ref, o_ref, hn_ref = refs
        o_ref[...] = x_ref[...]
    hn_ref[...] = _rms_rows(o_ref[...], g_ref[...]).astype(BF16)
    for c0 in range(0, D_FF, FF_CHUNK):
        width = min(FF_CHUNK, D_FF - c0)
        h = hn_ref[...]
        gate = jnp.dot(h, win_ref[:, c0:c0 + width], preferred_element_type=F32)
        up = jnp.dot(h, win_ref[:, D_FF + c0:D_FF + c0 + width], preferred_element_type=F32)
        a = (gate * jax.nn.sigmoid(gate) * up).astype(BF16)
        o_ref[...] += 0.5 * jnp.dot(a, wout_ref[c0:c0 + width, :], preferred_element_type=F32)


def _ffn(x, g, w_in, w_out, layer, attn=None):
    t = x.shape[0]
    const = lambda i: (0, 0)
    this_layer = lambda i: (layer, 0, 0)
    resident = pl.Buffered(1)
    row = pl.BlockSpec((FFN_TILE, D_MODEL), lambda i: (i, 0))
    in_specs, args = [row], [x]
    if attn is not None:
        a, w_a = attn
        in_specs += [pl.BlockSpec((FFN_TILE, a.shape[1]), lambda i: (i, 0)),
                     pl.BlockSpec(w_a.shape, const, pipeline_mode=resident)]
        args += [a, w_a]
    in_specs += [pl.BlockSpec((1, D_MODEL), const),
                 pl.BlockSpec((None, D_MODEL, 2 * D_FF), this_layer, pipeline_mode=resident),
                 pl.BlockSpec((None, D_FF, D_MODEL), this_layer, pipeline_mode=resident)]
    args += [g.reshape(1, D_MODEL), w_in, w_out]
    return pl.pallas_call(
        functools.partial(_ffn_kernel, fused_proj=attn is not None),
        grid=(t // FFN_TILE,),
        in_specs=in_specs,
        out_specs=row,
        out_shape=jax.ShapeDtypeStruct((t, D_MODEL), F32),
        scratch_shapes=[pltpu.VMEM((FFN_TILE, D_MODEL), BF16)],
        compiler_params=_params("parallel"),
        name="ffn_proj" if attn is not None else "ffn",
    )(*args)


def _project_chunk(hn_ref, w_ref, gain_ref, seg_ref, c, normed):
    cols = slice(c * QKV_CHUNK, (c + 1) * QKV_CHUNK)
    y = jnp.dot(hn_ref[...], w_ref[:, cols], preferred_element_type=F32)
    if normed:
        y2 = (y * y).astype(BF16)
        ms = jnp.concatenate(
            [jnp.dot(y2[:, s0:s0 + SEG_COLS], seg_ref[...], preferred_element_type=F32)
             for s0 in range(0, QKV_CHUNK, SEG_COLS)], axis=-1)
        y = (y * lax.rsqrt(ms + RMS_EPS)) * gain_ref[:, cols]
    return y


def _qkv_a_kernel(x_ref, g_ref, w_ref, gain_ref, seg_ref, cos_ref, sin_ref, *rest):
    n_chunks = 3 * N_GROUPS_A
    out_refs, (hn_ref, ys_ref) = rest[:N_GROUPS_A], rest[N_GROUPS_A:]
    hn_ref[...] = _rms_rows(x_ref[...], g_ref[...]).astype(BF16)
    cos, sin = cos_ref[...], sin_ref[...]
    for c in range(n_chunks):
        kind, group = divmod(c, N_GROUPS_A)
        dilation = DILATED_GROUPS[group][1]
        normed = kind < 2
        out_ref, off = out_refs[group], KVQ_OFFSET[kind]
        y = _project_chunk(hn_ref, w_ref, gain_ref, seg_ref, c, normed)
        if normed:
            y = jnp.concatenate(
                [y[:, p * LANES:(p + 1) * LANES] * cos
                 + pltpu.roll(y[:, p * LANES:(p + 1) * LANES], LANES // 2, axis=1) * sin
                 for p in range(PAIRS_PER_CHUNK)], axis=-1)
        if dilation == 1:
            out_ref[0, 0, :, off:off + QKV_CHUNK] = y.astype(BF16)
        else:
            slot = kind * (N_GROUPS_A - 1) + group - 1
            rows = QKV_TILE // dilation
            for p in range(PAIRS_PER_CHUNK):
                ys_ref[slot, p] = y[:, p * LANES:(p + 1) * LANES]
                dst = slice(off + p * LANES, off + (p + 1) * LANES)
                for r in range(dilation):
                    out_ref[0, r, :, dst] = ys_ref[slot, p, pl.ds(r, rows, stride=dilation), :].astype(BF16)


def _qkv_a(x, g, w, gains, seg, rope_tables, batch, seq_len):
    t = x.shape[0]
    tps = seq_len // QKV_TILE
    const = lambda i: (0, 0)
    in_specs = [
        pl.BlockSpec((QKV_TILE, D_MODEL), lambda i: (i, 0)),
        pl.BlockSpec((1, D_MODEL), const),
        pl.BlockSpec((D_MODEL, QKV_A), const, pipeline_mode=pl.Buffered(1)),
        pl.BlockSpec((1, QKV_A), const),
        pl.BlockSpec((SEG_COLS, SEG_COLS), const),
        pl.BlockSpec((QKV_TILE, LANES), lambda i: (i % tps, 0)),
        pl.BlockSpec((QKV_TILE, LANES), lambda i: (i % tps, 0)),
    ]
    out_specs, out_shapes = [], []
    for _, d in DILATED_GROUPS:
        out_specs.append(pl.BlockSpec((1, d, QKV_TILE // d, 3 * QKV_CHUNK),
                                      lambda i: (i // tps, 0, i % tps, 0)))
        out_shapes.append(jax.ShapeDtypeStruct((batch, d, seq_len // d, 3 * QKV_CHUNK), BF16))
    return pl.pallas_call(
        _qkv_a_kernel,
        grid=(t // QKV_TILE,),
        in_specs=in_specs,
        out_specs=out_specs,
        out_shape=out_shapes,
        scratch_shapes=[pltpu.VMEM((QKV_TILE, D_MODEL), BF16),
                        pltpu.VMEM((3 * (N_GROUPS_A - 1), PAIRS_PER_CHUNK, QKV_TILE, LANES), F32)],
        compiler_params=_params("parallel"),
        name="qkv_a",
    )(x, g.reshape(1, D_MODEL), w, gains, seg, *rope_tables)


def _qkv_b_kernel(x_ref, g_ref, w_ref, gain_ref, seg_ref, o_ref, hn_ref):
    hn_ref[...] = _rms_rows(x_ref[...], g_ref[...]).astype(BF16)
    for c in range(QKV_B // QKV_CHUNK):
        y = _project_chunk(hn_ref, w_ref, gain_ref, seg_ref, c, c < 2 * OUT_B // QKV_CHUNK)
        kind, step = divmod(c, OUT_B // QKV_CHUNK)
        for p in range(PAIRS_PER_CHUNK):
            o_ref[0, step, KVQ_OFFSET[kind] // LANES + p] = y[:, p * LANES:(p + 1) * LANES].astype(BF16)


def _qkv_b(x, g, w, gains, seg, batch, seq_len):
    t = x.shape[0]
    tps = seq_len // QKV_TILE
    n_steps = OUT_B // QKV_CHUNK
    n_slots = 3 * PAIRS_PER_CHUNK
    assert PAIRS_PER_CHUNK == ATT_B_PAIRS
    const = lambda i: (0, 0)
    return pl.pallas_call(
        _qkv_b_kernel,
        grid=(t // QKV_TILE,),
        in_specs=[
            pl.BlockSpec((QKV_TILE, D_MODEL), lambda i: (i, 0)),
            pl.BlockSpec((1, D_MODEL), const),
            pl.BlockSpec((D_MODEL, QKV_B), const, pipeline_mode=pl.Buffered(1)),
            pl.BlockSpec((1, QKV_B), const),
            pl.BlockSpec((SEG_COLS, SEG_COLS), const),
        ],
        out_specs=pl.BlockSpec((1, n_steps, n_slots, QKV_TILE, LANES), lambda i: (i // tps, 0, 0, i % tps, 0)),
        out_shape=jax.ShapeDtypeStruct((batch, n_steps, n_slots, seq_len, LANES), BF16),
        scratch_shapes=[pltpu.VMEM((QKV_TILE, D_MODEL), BF16)],
        compiler_params=_params("parallel"),
        name="qkv_b",
    )(x, g.reshape(1, D_MODEL), w, gains, seg)


def _attn_a_kernel(cur_ref, prev_ref, next_ref, o_ref, lse_ref, kw_ref, vw_ref, s_ref, p_ref, m_ref,
                   *, seq_len, tq):
    side, qb = ATT_A_SIDE, ATT_A_QB
    kb = qb + 2 * side
    nb = tq // qb
    q_off, k_off, v_off = KVQ_OFFSET
    kw_ref[0:side] = prev_ref[0, 0, :, k_off:k_off + OUT_A]
    kw_ref[side:side + tq] = cur_ref[0, 0, :, k_off:k_off + OUT_A]
    kw_ref[side + tq:] = next_ref[0, 0, :, k_off:k_off + OUT_A]
    ones = jnp.ones((tq + 2 * side, LANES), BF16)
    for p in range(PAIRS_PER_CHUNK):
        src = slice(v_off + p * LANES, v_off + (p + 1) * LANES)
        dst = slice(2 * p * LANES, (2 * p + 1) * LANES)
        vw_ref[0:side, dst] = prev_ref[0, 0, :, src]
        vw_ref[side:side + tq, dst] = cur_ref[0, 0, :, src]
        vw_ref[side + tq:, dst] = next_ref[0, 0, :, src]
        vw_ref[:, (2 * p + 1) * LANES:(2 * p + 2) * LANES] = ones

    q_start = pl.program_id(2) * tq
    row = lax.broadcasted_iota(jnp.int32, (qb, kb), 0)
    col = lax.broadcasted_iota(jnp.int32, (qb, kb), 1)
    band = (col >= row) & (col - row <= 2 * side)
    biases = []
    for b in range(nb):
        kpos = q_start + b * qb + col - side
        bias = jnp.where(band & (kpos >= 0) & (kpos < seq_len), 0.0, NEG_INF).astype(F32)
        biases.append(jnp.concatenate([bias, bias], axis=0))
    bias4 = jnp.stack(biases)

    lane = lax.broadcasted_iota(jnp.int32, (tq, LANES), 1)
    q_head_a = (lane % HEAD_DIM) < (HEAD_DIM // 2)
    v_head_a = lax.broadcasted_iota(jnp.int32, (qb, LANES), 1) < HEAD_DIM
    lse_lane = lax.broadcasted_iota(jnp.int32, (qb, LANES), 1) // LSE_REP
    zero = jnp.zeros((tq, LANES), BF16)

    for p in range(PAIRS_PER_CHUNK):
        cols = slice(p * LANES, (p + 1) * LANES)
        q2 = cur_ref[0, 0, :, q_off + p * LANES:q_off + (p + 1) * LANES]
        qa = jnp.where(q_head_a, q2, zero)
        qbb = jnp.where(q_head_a, zero, q2)
        for b in range(nb):
            rows = slice(b * qb, (b + 1) * qb)
            lhs = jnp.concatenate([qa[rows], qbb[rows]], axis=0)
            s_ref[p * nb + b] = lax.dot_general(lhs, kw_ref[b * qb:b * qb + kb, cols], NT_DIMS,
                                                preferred_element_type=F32)
        blk = slice(p * nb, (p + 1) * nb)
        s = s_ref[blk] + bias4
        m = jnp.max(s, axis=-1, keepdims=True)
        p_ref[blk] = jnp.exp2(s - m).astype(BF16)
        m_ref[blk] = jnp.broadcast_to(m, (nb, 2 * qb, LANES))
        for b in range(nb):
            rows = slice(b * qb, (b + 1) * qb)
            pv = jnp.dot(p_ref[p * nb + b], vw_ref[b * qb:b * qb + kb, 2 * p * LANES:(2 * p + 2) * LANES],
                         preferred_element_type=F32)
            den = pv[:, LANES:]
            o = pv[:, :LANES] / den
            lse = m_ref[p * nb + b] + jnp.log2(den)
            o_ref[0, 0, rows, cols] = jnp.where(v_head_a, o[:qb], o[qb:]).astype(BF16)
            prev = lse_ref[0, 0, rows, :] if p else jnp.zeros((qb, LANES), F32)
            lse_ref[0, 0, rows, :] = jnp.where(lse_lane == 2 * p, lse[:qb],
                                               jnp.where(lse_lane == 2 * p + 1, lse[qb:], prev))


def _attn_a_group(kvq, dilation, batch, seq_len):
    length = seq_len // dilation
    tq, side = min(ATT_A_TILE, length), ATT_A_SIDE
    n_tiles = length // tq
    halo_blocks = length // side
    nb = tq // ATT_A_QB
    kb = ATT_A_QB + 2 * side
    out = pl.BlockSpec((1, 1, tq, OUT_A), lambda b, r, i: (b, r, i, 0))
    cur = pl.BlockSpec((1, 1, tq, 3 * OUT_A), lambda b, r, i: (b, r, i, 0))
    prev = pl.BlockSpec((1, 1, side, 2 * OUT_A),
                        lambda b, r, i: (b, r, jnp.maximum(i * (tq // side) - 1, 0), 0))
    nxt = pl.BlockSpec((1, 1, side, 2 * OUT_A),
                       lambda b, r, i: (b, r, jnp.minimum((i + 1) * (tq // side), halo_blocks - 1), 0))
    n_blk = PAIRS_PER_CHUNK * nb
    lse_spec = pl.BlockSpec((1, 1, tq, LANES), lambda b, r, i: (b, r, i, 0))
    return pl.pallas_call(
        functools.partial(_attn_a_kernel, seq_len=length, tq=tq),
        grid=(batch, dilation, n_tiles),
        in_specs=[cur, prev, nxt],
        out_specs=[out, lse_spec],
        out_shape=[jax.ShapeDtypeStruct((batch, dilation, length, OUT_A), BF16),
                   jax.ShapeDtypeStruct((batch, dilation, length, LANES), F32)],
        scratch_shapes=[pltpu.VMEM((tq + 2 * side, OUT_A), BF16),
                        pltpu.VMEM((tq + 2 * side, 2 * OUT_A), BF16),
                        pltpu.VMEM((n_blk, 2 * ATT_A_QB, kb), F32),
                        pltpu.VMEM((n_blk, 2 * ATT_A_QB, kb), BF16),
                        pltpu.VMEM((n_blk, 2 * ATT_A_QB, LANES), F32)],
        compiler_params=_params("parallel", "parallel", "arbitrary"),
        name=f"attn_a_d{dilation}",
    )(kvq, kvq, kvq)


def _merge_a_kernel(o0_ref, o1_ref, o2_ref, l0_ref, l1_ref, l2_ref, e_ref, y_ref, ot_ref, lt_ref):
    for slot, (o_ref, l_ref, dilation) in enumerate(
            ((o1_ref, l1_ref, DILATED_GROUPS[1][1]), (o2_ref, l2_ref, DILATED_GROUPS[2][1]))):
        rows = TOKEN_TILE // dilation
        for r in range(dilation):
            lt_ref[slot, pl.ds(r, rows, stride=dilation), :] = l_ref[0, r]
            for p in range(PAIRS_PER_CHUNK):
                cols = slice(p * LANES, (p + 1) * LANES)
                ot_ref[slot, p, pl.ds(r, rows, stride=dilation), :] = o_ref[0, r, :, cols].astype(F32)
    l0, l1, l2 = l0_ref[0, 0], lt_ref[0], lt_ref[1]
    m = jnp.maximum(jnp.maximum(l0, l1), l2)
    e0, e1, e2 = jnp.exp2(l0 - m), jnp.exp2(l1 - m), jnp.exp2(l2 - m)
    inv = 1.0 / (e0 + e1 + e2)

    def spread(w):
        hi = w.astype(BF16)
        lo = (w - hi.astype(F32)).astype(BF16)
        return (jnp.dot(hi, e_ref[...], preferred_element_type=F32)
                + jnp.dot(lo, e_ref[...], preferred_element_type=F32))

    w0, w1, w2 = spread(e0 * inv), spread(e1 * inv), spread(e2 * inv)
    for p in range(PAIRS_PER_CHUNK):
        cols = slice(p * LANES, (p + 1) * LANES)
        y_ref[:, cols] = (w0[:, cols] * o0_ref[0, 0, :, cols].astype(F32) + w1[:, cols] * ot_ref[0, p]
                          + w2[:, cols] * ot_ref[1, p]).astype(BF16)


def _merge_a(outs, lses, batch, seq_len):
    t = batch * seq_len
    tps = seq_len // TOKEN_TILE
    grp = [pl.BlockSpec((1, d, TOKEN_TILE // d, OUT_A), lambda i: (i // tps, 0, i % tps, 0))
           for _, d in DILATED_GROUPS]
    grp_lse = [pl.BlockSpec((1, d, TOKEN_TILE // d, LANES), lambda i: (i // tps, 0, i % tps, 0))
               for _, d in DILATED_GROUPS]
    spread = jnp.asarray(np.arange(LANES)[:, None] == LSE_REP * (np.arange(OUT_A)[None, :] // HEAD_DIM), BF16)
    return pl.pallas_call(
        _merge_a_kernel,
        grid=(t // TOKEN_TILE,),
        in_specs=grp + grp_lse + [pl.BlockSpec((LANES, OUT_A), lambda i: (0, 0))],
        out_specs=pl.BlockSpec((TOKEN_TILE, OUT_A), lambda i: (i, 0)),
        out_shape=jax.ShapeDtypeStruct((t, OUT_A), BF16),
        scratch_shapes=[pltpu.VMEM((N_GROUPS_A - 1, PAIRS_PER_CHUNK, TOKEN_TILE, LANES), F32),
                        pltpu.VMEM((N_GROUPS_A - 1, TOKEN_TILE, LANES), F32)],
        compiler_params=_params("parallel"),
        name="merge_a",
    )(*outs, *lses, spread)


def _attn_b_kernel(cur_ref, prev_ref, next_ref, bias_ref, o_ref, kw_ref, vw_ref, s_ref, p_ref):
    w, halo, rows = GRID_W, ATT_B_HALO, ATT_B_ROWS
    n_sub = rows // ATT_B_SUB
    sub_q = ATT_B_SUB * w
    win = ATT_B_WIN * w
    ones = jnp.ones(((rows + 2 * halo) * w, LANES), BF16)
    q_slot, k_slot, v_slot = (off // LANES for off in KVQ_OFFSET)
    for u in range(ATT_B_PAIRS):
        kw_ref[u, 0:halo * w] = prev_ref[0, 0, k_slot + u]
        kw_ref[u, halo * w:(halo + rows) * w] = cur_ref[0, 0, k_slot + u]
        kw_ref[u, (halo + rows) * w:] = next_ref[0, 0, k_slot + u]
        vw_ref[u, 0:halo * w, :LANES] = prev_ref[0, 0, v_slot + u]
        vw_ref[u, halo * w:(halo + rows) * w, :LANES] = cur_ref[0, 0, v_slot + u]
        vw_ref[u, (halo + rows) * w:, :LANES] = next_ref[0, 0, v_slot + u]
        vw_ref[u, :, LANES:] = ones

    i = pl.program_id(2)
    lo_min = jnp.where(i == 0, halo, 0)
    lo_max = jnp.where(i == pl.num_programs(2) - 1, halo, rows)
    starts = [pl.multiple_of(jnp.clip(ATT_B_SUB * sb, lo_min, lo_max) * w, LANES) for sb in range(n_sub)]
    head_a = lax.broadcasted_iota(jnp.int32, (rows * w, LANES), 1) < HEAD_DIM
    out_a = lax.broadcasted_iota(jnp.int32, (sub_q, LANES), 1) < HEAD_DIM
    zero = jnp.zeros((rows * w, LANES), BF16)
    for u in range(ATT_B_PAIRS):
        q2 = cur_ref[0, 0, q_slot + u]
        qa = jnp.where(head_a, q2, zero)
        qb = jnp.where(head_a, zero, q2)
        for sb in range(n_sub):
            rs = slice(sb * sub_q, (sb + 1) * sub_q)
            lhs = jnp.concatenate([qa[rs], qb[rs]], axis=0)
            s_ref[u, sb] = lax.dot_general(lhs, kw_ref[u, pl.ds(starts[sb], win), :], NT_DIMS,
                                           preferred_element_type=F32)
    for u in range(ATT_B_PAIRS):
        s = s_ref[u] + jnp.concatenate([bias_ref[2 * u, 0], bias_ref[2 * u + 1, 0]], axis=1)
        m = jnp.max(s, axis=-1, keepdims=True)
        p_ref[u] = jnp.exp2(s - m).astype(BF16)
    for u in range(ATT_B_PAIRS):
        for sb in range(n_sub):
            pv = jnp.dot(p_ref[u, sb], vw_ref[u, pl.ds(starts[sb], win), :], preferred_element_type=F32)
            o = pv[:, :LANES] / pv[:, LANES:]
            o_ref[0, sb * sub_q:(sb + 1) * sub_q, u * LANES:(u + 1) * LANES] = (
                jnp.where(out_a, o[:sub_q], o[sub_q:]).astype(BF16))


def _attn_b(qkv, bias, batch, seq_len):
    n_rows = seq_len // GRID_W
    tile = ATT_B_ROWS * GRID_W
    halo = ATT_B_HALO * GRID_W
    n_tiles = n_rows // ATT_B_ROWS
    assert n_tiles >= 2
    per_tile = tile // halo
    halo_blocks = seq_len // halo
    n_steps = HEADS_B // 2 // ATT_B_PAIRS
    n_sub = ATT_B_ROWS // ATT_B_SUB
    sub_q = ATT_B_SUB * GRID_W
    win = ATT_B_WIN * GRID_W

    n_slots = 3 * ATT_B_PAIRS
    cur = pl.BlockSpec((1, 1, n_slots, tile, LANES), lambda hp, b, i: (b, hp, 0, i, 0))
    prev = pl.BlockSpec((1, 1, 2 * ATT_B_PAIRS, halo, LANES),
                        lambda hp, b, i: (b, hp, 0, jnp.maximum(i * per_tile - 1, 0), 0))
    nxt = pl.BlockSpec((1, 1, 2 * ATT_B_PAIRS, halo, LANES),
                       lambda hp, b, i: (b, hp, 0, jnp.minimum((i + 1) * per_tile, halo_blocks - 1), 0))

    def tile_kind(i):
        return jnp.where(i == 0, 0, jnp.where(i == n_tiles - 1, 2, 1))

    return pl.pallas_call(
        _attn_b_kernel,
        grid=(n_steps, batch, n_tiles),
        in_specs=[cur, prev, nxt,
                  pl.BlockSpec((2 * ATT_B_PAIRS, 1, n_sub, sub_q, win),
                               lambda hp, b, i: (hp, tile_kind(i), 0, 0, 0))],
        out_specs=pl.BlockSpec((1, tile, ATT_B_PAIRS * LANES), lambda hp, b, i: (b, i, hp)),
        out_shape=jax.ShapeDtypeStruct((batch, seq_len, OUT_B), BF16),
        scratch_shapes=[pltpu.VMEM((ATT_B_PAIRS, tile + 2 * halo, LANES), BF16),
                        pltpu.VMEM((ATT_B_PAIRS, tile + 2 * halo, 2 * LANES), BF16),
                        pltpu.VMEM((ATT_B_PAIRS, n_sub, 2 * sub_q, win), F32),
                        pltpu.VMEM((ATT_B_PAIRS, n_sub, 2 * sub_q, win), BF16)],
        compiler_params=_params("arbitrary", "arbitrary", "arbitrary"),
        name="attn_b",
    )(qkv, qkv, qkv, bias)


def _rope_layout():
    n = np.arange(QKV_CHUNK)
    pair, lane = n // LANES, n % LANES
    second, hb, i = lane // HEAD_DIM, (lane % HEAD_DIM) // (HEAD_DIM // 2), lane % (HEAD_DIM // 2)
    head = 2 * pair + hb
    return head * HEAD_DIM + second * (HEAD_DIM // 2) + i, head


def _segment_mean_matrix(head_of_column):
    same = head_of_column[:, None] == head_of_column[None, :]
    return jnp.asarray(np.where(same, 1.0 / HEAD_DIM, 0.0), BF16)


def _rope_tables(seq_len):
    half = HEAD_DIM // 2
    inv_freq = np.float32(ROPE_THETA) ** (-np.arange(half, dtype=np.float32) / np.float32(half))
    ang = (np.arange(seq_len, dtype=np.float32)[:, None] * inv_freq[None, :]).astype(np.float64)
    cos, sin = jnp.asarray(np.cos(ang), F32), jnp.asarray(np.sin(ang), F32)
    return (jnp.concatenate([cos, cos, cos, cos], axis=-1),
            jnp.concatenate([-sin, -sin, sin, sin], axis=-1))


def _prep_a(w_qkv, q_gain, k_gain):
    src, _ = _rope_layout()
    n_qk = 2 * N_GROUPS_A
    cols = np.concatenate([c * QKV_CHUNK + src for c in range(n_qk)]
                          + [np.arange(n_qk * QKV_CHUNK, QKV_A)])
    half = HEAD_DIM // 2

    def pair_lanes(gain):
        return jnp.concatenate([gain[:half], gain[:half], gain[half:], gain[half:]])

    n_pairs = N_GROUPS_A * PAIRS_PER_CHUNK
    gains = jnp.concatenate([jnp.tile(pair_lanes(q_gain) * Q_SCALE, n_pairs),
                             jnp.tile(pair_lanes(k_gain), n_pairs),
                             jnp.ones((QKV_A - n_qk * QKV_CHUNK,), F32)])
    return w_qkv[:, cols].astype(BF16), gains.reshape(1, QKV_A)


def _prep_b(q_gain, k_gain):
    gains = jnp.concatenate([jnp.tile(q_gain * Q_SCALE, HEADS_B),
                             jnp.tile(k_gain, HEADS_B),
                             jnp.ones((OUT_B,), F32)])
    return gains.reshape(1, QKV_B)


def _bias_tile_plan():
    halo, rows, n_sub = ATT_B_HALO, ATT_B_ROWS, ATT_B_ROWS // ATT_B_SUB
    plan = {}
    for kind, (lo_min, lo_max) in enumerate(((halo, rows), (0, rows), (0, halo))):
        for sb in range(n_sub):
            ws = min(max(ATT_B_SUB * sb, lo_min), lo_max)
            for ql in range(ATT_B_SUB):
                rho = ATT_B_SUB * sb + ql
                lo = min(max(rho, lo_min), lo_max)
                for jj in range(ATT_B_WIN // 2):
                    jl = ws + 2 * jj
                    dr = jl - rho + NA_ROWS // 2 - 1
                    plan[kind, sb, ql, jj] = (dr, lo <= jl < lo + NA_ROWS, lo <= jl + 1 < lo + NA_ROWS)
    return plan


def _bias_expand_kernel(c_ref, o_ref):
    w = GRID_W
    left = lax.broadcasted_iota(jnp.int32, (w, 2 * w), 1) < w
    neg = jnp.full((w, 2 * w), NEG_INF, F32)
    for (kind, sb, ql, jj), (dr, ok_a, ok_b) in _bias_tile_plan().items():
        if ok_a or ok_b:
            tile = c_ref[0, dr + 1]
            if not ok_b:
                tile = jnp.where(left, tile, neg)
            elif not ok_a:
                tile = jnp.where(left, neg, tile)
        else:
            tile = neg
        o_ref[0, kind, sb, ql * w:(ql + 1) * w, 2 * jj * w:2 * (jj + 1) * w] = tile


def _bias_table_b(rpb):
    w, n_sub, n_dr = GRID_W, ATT_B_ROWS // ATT_B_SUB, 2 * NA_ROWS - 1
    col = np.arange(w)
    col_start = np.clip(col - NA_COLS // 2, 0, w - NA_COLS)
    col_ok = (col[None, :] >= col_start[:, None]) & (col[None, :] < col_start[:, None] + NA_COLS)
    dc = np.clip(col[None, :] - col[:, None] + NA_COLS - 1, 0, 2 * NA_COLS - 2)
    pick = jnp.asarray(np.arange(2 * NA_COLS - 1)[:, None, None] == dc[None], F32)
    by_row = jnp.einsum("hrc,cqk->hrqk", rpb * LOG2E, pick, precision=lax.Precision.HIGHEST)
    by_row = jnp.where(col_ok[None, None], by_row, NEG_INF)
    pad = jnp.full((HEADS_B, 1, w, w), NEG_INF, F32)
    by_row = jnp.concatenate([pad, by_row, pad], axis=1)
    pairs = jnp.concatenate([by_row[:, :-1], by_row[:, 1:]], axis=-1)
    return pl.pallas_call(
        _bias_expand_kernel,
        grid=(HEADS_B,),
        in_specs=[pl.BlockSpec((1, n_dr + 1, w, 2 * w), lambda h: (h, 0, 0, 0))],
        out_specs=pl.BlockSpec((1, 3, n_sub, ATT_B_SUB * w, ATT_B_WIN * w), lambda h: (h, 0, 0, 0, 0)),
        out_shape=jax.ShapeDtypeStruct((HEADS_B, 3, n_sub, ATT_B_SUB * w, ATT_B_WIN * w), F32),
        compiler_params=_params("parallel"),
        name="bias_expand",
    )(pairs)


def _mixer_a(x, batch, seq_len, mix_g, w_qkv, gains, seg, rope_tables):
    qkv = _qkv_a(x, mix_g, w_qkv, gains, seg, rope_tables, batch, seq_len)
    outs, lses = [], []
    for group, (window, dilation) in enumerate(DILATED_GROUPS):
        assert window // (2 * dilation) == ATT_A_SIDE
        o, lse = _attn_a_group(qkv[group], dilation, batch, seq_len)
        outs.append(o)
        lses.append(lse)
    return _merge_a(outs, lses, batch, seq_len)


def _mixer_b(x, batch, seq_len, mix_g, w_qkv, gains, bias, seg):
    qkv = _qkv_b(x, mix_g, w_qkv, gains, seg, batch, seq_len)
    return _attn_b(qkv, bias, batch, seq_len).reshape(batch * seq_len, OUT_B)


def kernel(x_prompt, x_sample, ffn1_norm, ffn1_w_in, ffn1_w_out, mix_norm, ffn2_norm, ffn2_w_in, ffn2_w_out, a_w_qkv, a_q_norm, a_k_norm, a_w_o, b_w_qkv, b_q_norm, b_k_norm, b_rpb, b_w_o):
    depth = ffn1_norm.shape[0]
    bf = lambda w: w.astype(BF16)
    ffn1_w_in, ffn1_w_out, ffn2_w_in, ffn2_w_out = map(bf, (ffn1_w_in, ffn1_w_out, ffn2_w_in, ffn2_w_out))
    a_w_o, b_w_qkv, b_w_o = map(bf, (a_w_o, b_w_qkv, b_w_o))
    seg_a = _segment_mean_matrix(_rope_layout()[1][:SEG_COLS])
    seg_b = _segment_mean_matrix(np.arange(SEG_COLS) // HEAD_DIM)
    prep_a = [_prep_a(a_w_qkv[j], a_q_norm[j], a_k_norm[j]) for j in range(a_w_qkv.shape[0])]
    prep_b = [(_prep_b(b_q_norm[j], b_k_norm[j]), _bias_table_b(b_rpb[j])) for j in range(b_w_qkv.shape[0])]
    rope_tables = _rope_tables(max(x_prompt.shape[1], x_sample.shape[1]))

    def trunk(x3):
        batch, seq_len, _ = x3.shape
        x = x3.reshape(batch * seq_len, D_MODEL)
        for i in range(depth):
            j = i // 2
            x = _ffn(x, ffn1_norm[i], ffn1_w_in, ffn1_w_out, i)
            if i % 2 == 0:
                w_qkv, gains = prep_a[j]
                attn = (_mixer_a(x, batch, seq_len, mix_norm[i], w_qkv, gains, seg_a, rope_tables), a_w_o[j])
            else:
                gains, bias = prep_b[j]
                attn = (_mixer_b(x, batch, seq_len, mix_norm[i], b_w_qkv[j], gains, bias, seg_b), b_w_o[j])
            x = _ffn(x, ffn2_norm[i], ffn2_w_in, ffn2_w_out, i, attn)
        return x.reshape(batch, seq_len, D_MODEL)

    return (trunk(x_prompt), trunk(x_sample))
```

```python
import functools

import jax
import jax.numpy as jnp
import numpy as np
from jax import lax
from jax.experimental import pallas as pl
from jax.experimental.pallas import tpu as pltpu

D_MODEL = 1024
HEAD_DIM = 64
D_FF = 2816
RMS_EPS = 1e-6
ROPE_THETA = 10000.0
NEG_INF = -1e30
LOG2E = 1.4426950408889634
DILATED_GROUPS = ((128, 1), (512, 4), (2048, 16))
HEADS_A = 8
N_GROUPS_A = 3
QKV_A = 3 * N_GROUPS_A * HEADS_A * HEAD_DIM
OUT_A = HEADS_A * HEAD_DIM
HEADS_B = 16
QKV_B = 3 * HEADS_B * HEAD_DIM
OUT_B = HEADS_B * HEAD_DIM
GRID_W = 64
NA_ROWS = 8
NA_COLS = 16

LANES = 128
VMEM_LIMIT_BYTES = 56 * 1024 * 1024

TOKEN_TILE = 1024
QKV_TILE = 1024
FFN_TILE = 1024
FF_CHUNK = 256
SEG_COLS = 256
QKV_CHUNK = 512
PAIRS_PER_CHUNK = QKV_CHUNK // LANES
ATT_A_TILE = 512
ATT_A_QB = 128
KVQ_OFFSET = (2 * QKV_CHUNK, 0, QKV_CHUNK)
ATT_A_SIDE = 64
LSE_REP = LANES // HEADS_A
ATT_B_ROWS = 8
ATT_B_HALO = 4
ATT_B_PAIRS = 4
ATT_B_SUB = 2
ATT_B_WIN = NA_ROWS + ATT_B_SUB

BF16 = jnp.bfloat16
F32 = jnp.float32
NT_DIMS = (((1,), (1,)), ((), ()))
Q_SCALE = HEAD_DIM ** -0.5 * LOG2E


def _params(*semantics):
    return pltpu.CompilerParams(dimension_semantics=semantics,
                                vmem_limit_bytes=VMEM_LIMIT_BYTES)


def _rms_rows(x, g):
    ms = jnp.mean(x * x, axis=-1, keepdims=True)
    return (x * lax.rsqrt(ms + RMS_EPS)) * g


def _ffn_kernel(*refs, fused_proj):
    if fused_proj:
        x_ref, a_ref, wa_ref, g_ref, win_ref, wout_ref, o_ref, hn_ref = refs
        o_ref[...] = x_ref[...] + jnp.dot(a_ref[...], wa_ref[...], preferred_element_type=F32)
    else:
        x_ref, g_ref, win_ref, wout_ref, o_ref, hn_ref = refs
        o_ref[...] = x_ref[...]
    hn_ref[...] = _rms_rows(o_ref[...], g_ref[...]).astype(BF16)
    for c0 in range(0, D_FF, FF_CHUNK):
        width = min(FF_CHUNK, D_FF - c0)
        h = hn_ref[...]
        gate = jnp.dot(h, win_ref[:, c0:c0 + width], preferred_element_type=F32)
        up = jnp.dot(h, win_ref[:, D_FF + c0:D_FF + c0 + width], preferred_element_type=F32)
        a = (gate * jax.nn.sigmoid(gate) * up).astype(BF16)
        o_ref[...] += 0.5 * jnp.dot(a, wout_ref[c0:c0 + width, :], preferred_element_type=F32)


def _ffn(x, g, w_in, w_out, layer, attn=None):
    t = x.shape[0]
    const = lambda i: (0, 0)
    this_layer = lambda i: (layer, 0, 0)
    resident = pl.Buffered(1)
    row = pl.BlockSpec((FFN_TILE, D_MODEL), lambda i: (i, 0))
    in_specs, args = [row], [x]
    if attn is not None:
        a, w_a = attn
        in_specs += [pl.BlockSpec((FFN_TILE, a.shape[1]), lambda i: (i, 0)),
                     pl.BlockSpec(w_a.shape, const, pipeline_mode=resident)]
        args += [a, w_a]
    in_specs += [pl.BlockSpec((1, D_MODEL), const),
                 pl.BlockSpec((None, D_MODEL, 2 * D_FF), this_layer, pipeline_mode=resident),
                 pl.BlockSpec((None, D_FF, D_MODEL), this_layer, pipeline_mode=resident)]
    args += [g.reshape(1, D_MODEL), w_in, w_out]
    return pl.pallas_call(
        functools.partial(_ffn_kernel, fused_proj=attn is not None),
        grid=(t // FFN_TILE,),
        in_specs=in_specs,
        out_specs=row,
        out_shape=jax.ShapeDtypeStruct((t, D_MODEL), F32),
        scratch_shapes=[pltpu.VMEM((FFN_TILE, D_MODEL), BF16)],
        compiler_params=_params("parallel"),
        name="ffn_proj" if attn is not None else "ffn",
    )(*args)


def _project_chunk(hn_ref, w_ref, gain_ref, seg_ref, c, normed):
    cols = slice(c * QKV_CHUNK, (c + 1) * QKV_CHUNK)
    y = jnp.dot(hn_ref[...], w_ref[:, cols], preferred_element_type=F32)
    if normed:
        y2 = (y * y).astype(BF16)
        ms = jnp.concatenate(
            [jnp.dot(y2[:, s0:s0 + SEG_COLS], seg_ref[...], preferred_element_type=F32)
             for s0 in range(0, QKV_CHUNK, SEG_COLS)], axis=-1)
        y = (y * lax.rsqrt(ms + RMS_EPS)) * gain_ref[:, cols]
    return y


def _qkv_a_kernel(x_ref, g_ref, w_ref, gain_ref, seg_ref, cos_ref, sin_ref, *rest):
    n_chunks = 3 * N_GROUPS_A
    out_refs, (hn_ref, ys_ref) = rest[:N_GROUPS_A], rest[N_GROUPS_A:]
    hn_ref[...] = _rms_rows(x_ref[...], g_ref[...]).astype(BF16)
    cos, sin = cos_ref[...], sin_ref[...]
    for c in range(n_chunks):
        kind, group = divmod(c, N_GROUPS_A)
        dilation = DILATED_GROUPS[group][1]
        normed = kind < 2
        out_ref, off = out_refs[group], KVQ_OFFSET[kind]
        y = _project_chunk(hn_ref, w_ref, gain_ref, seg_ref, c, normed)
        if normed:
            y = jnp.concatenate(
                [y[:, p * LANES:(p + 1) * LANES] * cos
                 + pltpu.roll(y[:, p * LANES:(p + 1) * LANES], LANES // 2, axis=1) * sin
                 for p in range(PAIRS_PER_CHUNK)], axis=-1)
        if dilation == 1:
            out_ref[0, 0, :, off:off + QKV_CHUNK] = y.astype(BF16)
        else:
            slot = kind * (N_GROUPS_A - 1) + group - 1
            rows = QKV_TILE // dilation
            for p in range(PAIRS_PER_CHUNK):
                ys_ref[slot, p] = y[:, p * LANES:(p + 1) * LANES]
                dst = slice(off + p * LANES, off + (p + 1) * LANES)
                for r in range(dilation):
                    out_ref[0, r, :, dst] = ys_ref[slot, p, pl.ds(r, rows, stride=dilation), :].astype(BF16)


def _qkv_a(x, g, w, gains, seg, rope_tables, batch, seq_len):
    t = x.shape[0]
    tps = seq_len // QKV_TILE
    const = lambda i: (0, 0)
    in_specs = [
        pl.BlockSpec((QKV_TILE, D_MODEL), lambda i: (i, 0)),
        pl.BlockSpec((1, D_MODEL), const),
        pl.BlockSpec((D_MODEL, QKV_A), const, pipeline_mode=pl.Buffered(1)),
        pl.BlockSpec((1, QKV_A), const),
        pl.BlockSpec((SEG_COLS, SEG_COLS), const),
        pl.BlockSpec((QKV_TILE, LANES), lambda i: (i % tps, 0)),
        pl.BlockSpec((QKV_TILE, LANES), lambda i: (i % tps, 0)),
    ]
    out_specs, out_shapes = [], []
    for _, d in DILATED_GROUPS:
        out_specs.append(pl.BlockSpec((1, d, QKV_TILE // d, 3 * QKV_CHUNK),
                                      lambda i: (i // tps, 0, i % tps, 0)))
        out_shapes.append(jax.ShapeDtypeStruct((batch, d, seq_len // d, 3 * QKV_CHUNK), BF16))
    return pl.pallas_call(
        _qkv_a_kernel,
        grid=(t // QKV_TILE,),
        in_specs=in_specs,
        out_specs=out_specs,
        out_shape=out_shapes,
        scratch_shapes=[pltpu.VMEM((QKV_TILE, D_MODEL), BF16),
                        pltpu.VMEM((3 * (N_GROUPS_A - 1), PAIRS_PER_CHUNK, QKV_TILE, LANES), F32)],
        compiler_params=_params("parallel"),
        name="qkv_a",
    )(x, g.reshape(1, D_MODEL), w, gains, seg, *rope_tables)


def _qkv_b_kernel(x_ref, g_ref, w_ref, gain_ref, seg_ref, o_ref, hn_ref):
    hn_ref[...] = _rms_rows(x_ref[...], g_ref[...]).astype(BF16)
    for c in range(QKV_B // QKV_CHUNK):
        y = _project_chunk(hn_ref, w_ref, gain_ref, seg_ref, c, c < 2 * OUT_B // QKV_CHUNK)
        kind, step = divmod(c, OUT_B // QKV_CHUNK)
        for p in range(PAIRS_PER_CHUNK):
            o_ref[0, step, KVQ_OFFSET[kind] // LANES + p] = y[:, p * LANES:(p + 1) * LANES].astype(BF16)


def _qkv_b(x, g, w, gains, seg, batch, seq_len):
    t = x.shape[0]
    tps = seq_len // QKV_TILE
    n_steps = OUT_B // QKV_CHUNK
    n_slots = 3 * PAIRS_PER_CHUNK
    assert PAIRS_PER_CHUNK == ATT_B_PAIRS
    const = lambda i: (0, 0)
    return pl.pallas_call(
        _qkv_b_kernel,
        grid=(t // QKV_TILE,),
        in_specs=[
            pl.BlockSpec((QKV_TILE, D_MODEL), lambda i: (i, 0)),
            pl.BlockSpec((1, D_MODEL), const),
            pl.BlockSpec((D_MODEL, QKV_B), const, pipeline_mode=pl.Buffered(1)),
            pl.BlockSpec((1, QKV_B), const),
            pl.BlockSpec((SEG_COLS, SEG_COLS), const),
        ],
        out_specs=pl.BlockSpec((1, n_steps, n_slots, QKV_TILE, LANES), lambda i: (i // tps, 0, 0, i % tps, 0)),
        out_shape=jax.ShapeDtypeStruct((batch, n_steps, n_slots, seq_len, LANES), BF16),
        scratch_shapes=[pltpu.VMEM((QKV_TILE, D_MODEL), BF16)],
        compiler_params=_params("parallel"),
        name="qkv_b",
    )(x, g.reshape(1, D_MODEL), w, gains, seg)


def _attn_a_kernel(cur_ref, prev_ref, next_ref, o_ref, lse_ref, kw_ref, vw_ref, s_ref, p_ref, m_ref,
                   *, seq_len, tq):
    side, qb = ATT_A_SIDE, ATT_A_QB
    kb = qb + 2 * side
    nb = tq // qb
    q_off, k_off, v_off = KVQ_OFFSET
    kw_ref[0:side] = prev_ref[0, 0, :, k_off:k_off + OUT_A]
    kw_ref[side:side + tq] = cur_ref[0, 0, :, k_off:k_off + OUT_A]
    kw_ref[side + tq:] = next_ref[0, 0, :, k_off:k_off + OUT_A]
    ones = jnp.ones((tq + 2 * side, LANES), BF16)
    for p in range(PAIRS_PER_CHUNK):
        src = slice(v_off + p * LANES, v_off + (p + 1) * LANES)
        dst = slice(2 * p * LANES, (2 * p + 1) * LANES)
        vw_ref[0:side, dst] = prev_ref[0, 0, :, src]
        vw_ref[side:side + tq, dst] = cur_ref[0, 0, :, src]
        vw_ref[side + tq:, dst] = next_ref[0, 0, :, src]
        vw_ref[:, (2 * p + 1) * LANES:(2 * p + 2) * LANES] = ones

    q_start = pl.program_id(2) * tq
    row = lax.broadcasted_iota(jnp.int32, (qb, kb), 0)
    col = lax.broadcasted_iota(jnp.int32, (qb, kb), 1)
    band = (col >= row) & (col - row <= 2 * side)
    biases = []
    for b in range(nb):
        kpos = q_start + b * qb + col - side
        bias = jnp.where(band & (kpos >= 0) & (kpos < seq_len), 0.0, NEG_INF).astype(F32)
        biases.append(jnp.concatenate([bias, bias], axis=0))
    bias4 = jnp.stack(biases)

    lane = lax.broadcasted_iota(jnp.int32, (tq, LANES), 1)
    q_head_a = (lane % HEAD_DIM) < (HEAD_DIM // 2)
    v_head_a = lax.broadcasted_iota(jnp.int32, (qb, LANES), 1) < HEAD_DIM
    lse_lane = lax.broadcasted_iota(jnp.int32, (qb, LANES), 1) // LSE_REP
    zero = jnp.zeros((tq, LANES), BF16)

    for p in range(PAIRS_PER_CHUNK):
        cols = slice(p * LANES, (p + 1) * LANES)
        q2 = cur_ref[0, 0, :, q_off + p * LANES:q_off + (p + 1) * LANES]
        qa = jnp.where(q_head_a, q2, zero)
        qbb = jnp.where(q_head_a, zero, q2)
        for b in range(nb):
            rows = slice(b * qb, (b + 1) * qb)
            lhs = jnp.concatenate([qa[rows], qbb[rows]], axis=0)
            s_ref[p * nb + b] = lax.dot_general(lhs, kw_ref[b * qb:b * qb + kb, cols], NT_DIMS,
                                                preferred_element_type=F32)
        blk = slice(p * nb, (p + 1) * nb)
        s = s_ref[blk] + bias4
        m = jnp.max(s, axis=-1, keepdims=True)
        p_ref[blk] = jnp.exp2(s - m).astype(BF16)
        m_ref[blk] = jnp.broadcast_to(m, (nb, 2 * qb, LANES))
        for b in range(nb):
            rows = slice(b * qb, (b + 1) * qb)
            pv = jnp.dot(p_ref[p * nb + b], vw_ref[b * qb:b * qb + kb, 2 * p * LANES:(2 * p + 2) * LANES],
                         preferred_element_type=F32)
            den = pv[:, LANES:]
            o = pv[:, :LANES] / den
            lse = m_ref[p * nb + b] + jnp.log2(den)
            o_ref[0, 0, rows, cols] = jnp.where(v_head_a, o[:qb], o[qb:]).astype(BF16)
            prev = lse_ref[0, 0, rows, :] if p else jnp.zeros((qb, LANES), F32)
            lse_ref[0, 0, rows, :] = jnp.where(lse_lane == 2 * p, lse[:qb],
                                               jnp.where(lse_lane == 2 * p + 1, lse[qb:], prev))


def _attn_a_group(kvq, dilation, batch, seq_len):
    length = seq_len // dilation
    tq, side = min(ATT_A_TILE, length), ATT_A_SIDE
    n_tiles = length // tq
    halo_blocks = length // side
    nb = tq // ATT_A_QB
    kb = ATT_A_QB + 2 * side
    out = pl.BlockSpec((1, 1, tq, OUT_A), lambda b, r, i: (b, r, i, 0))
    cur = pl.BlockSpec((1, 1, tq, 3 * OUT_A), lambda b, r, i: (b, r, i, 0))
    prev = pl.BlockSpec((1, 1, side, 2 * OUT_A),
                        lambda b, r, i: (b, r, jnp.maximum(i * (tq // side) - 1, 0), 0))
    nxt = pl.BlockSpec((1, 1, side, 2 * OUT_A),
                       lambda b, r, i: (b, r, jnp.minimum((i + 1) * (tq // side), halo_blocks - 1), 0))
    n_blk = PAIRS_PER_CHUNK * nb
    lse_spec = pl.BlockSpec((1, 1, tq, LANES), lambda b, r, i: (b, r, i, 0))
    return pl.pallas_call(
        functools.partial(_attn_a_kernel, seq_len=length, tq=tq),
        grid=(batch, dilation, n_tiles),
        in_specs=[cur, prev, nxt],
        out_specs=[out, lse_spec],
        out_shape=[jax.ShapeDtypeStruct((batch, dilation, length, OUT_A), BF16),
                   jax.ShapeDtypeStruct((batch, dilation, length, LANES), F32)],
        scratch_shapes=[pltpu.VMEM((tq + 2 * side, OUT_A), BF16),
                        pltpu.VMEM((tq + 2 * side, 2 * OUT_A), BF16),
                        pltpu.VMEM((n_blk, 2 * ATT_A_QB, kb), F32),
                        pltpu.VMEM((n_blk, 2 * ATT_A_QB, kb), BF16),
                        pltpu.VMEM((n_blk, 2 * ATT_A_QB, LANES), F32)],
        compiler_params=_params("parallel", "parallel", "arbitrary"),
        name=f"attn_a_d{dilation}",
    )(kvq, kvq, kvq)


def _merge_a_kernel(o0_ref, o1_ref, o2_ref, l0_ref, l1_ref, l2_ref, e_ref, y_ref, ot_ref, lt_ref):
    for slot, (o_ref, l_ref, dilation) in enumerate(
            ((o1_ref, l1_ref, DILATED_GROUPS[1][1]), (o2_ref, l2_ref, DILATED_GROUPS[2][1]))):
        rows = TOKEN_TILE // dilation
        for r in range(dilation):
            lt_ref[slot, pl.ds(r, rows, stride=dilation), :] = l_ref[0, r]
            for p in range(PAIRS_PER_CHUNK):
                cols = slice(p * LANES, (p + 1) * LANES)
                ot_ref[slot, p, pl.ds(r, rows, stride=dilation), :] = o_ref[0, r, :, cols].astype(F32)
    l0, l1, l2 = l0_ref[0, 0], lt_ref[0], lt_ref[1]
    m = jnp.maximum(jnp.maximum(l0, l1), l2)
    e0, e1, e2 = jnp.exp2(l0 - m), jnp.exp2(l1 - m), jnp.exp2(l2 - m)
    inv = 1.0 / (e0 + e1 + e2)

    def spread(w):
        hi = w.astype(BF16)
        lo = (w - hi.astype(F32)).astype(BF16)
        return (jnp.dot(hi, e_ref[...], preferred_element_type=F32)
                + jnp.dot(lo, e_ref[...], preferred_element_type=F32))

    w0, w1, w2 = spread(e0 * inv), spread(e1 * inv), spread(e2 * inv)
    for p in range(PAIRS_PER_CHUNK):
        cols = slice(p * LANES, (p + 1) * LANES)
        y_ref[:, cols] = (w0[:, cols] * o0_ref[0, 0, :, cols].astype(F32) + w1[:, cols] * ot_ref[0, p]
                          + w2[:, cols] * ot_ref[1, p]).astype(BF16)


def _merge_a(outs, lses, batch, seq_len):
    t = batch * seq_len
    tps = seq_len // TOKEN_TILE
    grp = [pl.BlockSpec((1, d, TOKEN_TILE // d, OUT_A), lambda i: (i // tps, 0, i % tps, 0))
           for _, d in DILATED_GROUPS]
    grp_lse = [pl.BlockSpec((1, d, TOKEN_TILE // d, LANES), lambda i: (i // tps, 0, i % tps, 0))
               for _, d in DILATED_GROUPS]
    spread = jnp.asarray(np.arange(LANES)[:, None] == LSE_REP * (np.arange(OUT_A)[None, :] // HEAD_DIM), BF16)
    return pl.pallas_call(
        _merge_a_kernel,
        grid=(t // TOKEN_TILE,),
        in_specs=grp + grp_lse + [pl.BlockSpec((LANES, OUT_A), lambda i: (0, 0))],
        out_specs=pl.BlockSpec((TOKEN_TILE, OUT_A), lambda i: (i, 0)),
        out_shape=jax.ShapeDtypeStruct((t, OUT_A), BF16),
        scratch_shapes=[pltpu.VMEM((N_GROUPS_A - 1, PAIRS_PER_CHUNK, TOKEN_TILE, LANES), F32),
                        pltpu.VMEM((N_GROUPS_A - 1, TOKEN_TILE, LANES), F32)],
        compiler_params=_params("parallel"),
        name="merge_a",
    )(*outs, *lses, spread)


def _attn_b_kernel(cur_ref, prev_ref, next_ref, bias_ref, o_ref, kw_ref, vw_ref, s_ref, p_ref):
    w, halo, rows = GRID_W, ATT_B_HALO, ATT_B_ROWS
    n_sub = rows // ATT_B_SUB
    sub_q = ATT_B_SUB * w
    win = ATT_B_WIN * w
    ones = jnp.ones(((rows + 2 * halo) * w, LANES), BF16)
    q_slot, k_slot, v_slot = (off // LANES for off in KVQ_OFFSET)
    for u in range(ATT_B_PAIRS):
        kw_ref[u, 0:halo * w] = prev_ref[0, 0, k_slot + u]
        kw_ref[u, halo * w:(halo + rows) * w] = cur_ref[0, 0, k_slot + u]
        kw_ref[u, (halo + rows) * w:] = next_ref[0, 0, k_slot + u]
        vw_ref[u, 0:halo * w, :LANES] = prev_ref[0, 0, v_slot + u]
        vw_ref[u, halo * w:(halo + rows) * w, :LANES] = cur_ref[0, 0, v_slot + u]
        vw_ref[u, (halo + rows) * w:, :LANES] = next_ref[0, 0, v_slot + u]
        vw_ref[u, :, LANES:] = ones

    i = pl.program_id(1)
    lo_min = jnp.where(i == 0, halo, 0)
    lo_max = jnp.where(i == pl.num_programs(1) - 1, halo, rows)
    starts = [pl.multiple_of(jnp.clip(ATT_B_SUB * sb, lo_min, lo_max) * w, LANES) for sb in range(n_sub)]
    head_a = lax.broadcasted_iota(jnp.int32, (rows * w, LANES), 1) < HEAD_DIM
    out_a = lax.broadcasted_iota(jnp.int32, (sub_q, LANES), 1) < HEAD_DIM
    zero = jnp.zeros((rows * w, LANES), BF16)
    for u in range(ATT_B_PAIRS):
        q2 = cur_ref[0, 0, q_slot + u]
        qa = jnp.where(head_a, q2, zero)
        qb = jnp.where(head_a, zero, q2)
        for sb in range(n_sub):
            rs = slice(sb * sub_q, (sb + 1) * sub_q)
            lhs = jnp.concatenate([qa[rs], qb[rs]], axis=0)
            s_ref[u, sb] = lax.dot_general(lhs, kw_ref[u, pl.ds(starts[sb], win), :], NT_DIMS,
                                           preferred_element_type=F32)
    for u in range(ATT_B_PAIRS):
        s = s_ref[u] + jnp.concatenate([bias_ref[2 * u, 0], bias_ref[2 * u + 1, 0]], axis=1)
        m = jnp.max(s, axis=-1, keepdims=True)
        p_ref[u] = jnp.exp2(s - m).astype(BF16)
    for u in range(ATT_B_PAIRS):
        for sb in range(n_sub):
            pv = jnp.dot(p_ref[u, sb], vw_ref[u, pl.ds(starts[sb], win), :], preferred_element_type=F32)
            o = pv[:, :LANES] / pv[:, LANES:]
            o_ref[0, sb * sub_q:(sb + 1) * sub_q, u * LANES:(u + 1) * LANES] = (
                jnp.where(out_a, o[:sub_q], o[sub_q:]).astype(BF16))


def _attn_b(qkv, bias, batch, seq_len):
    n_rows = seq_len // GRID_W
    tile = ATT_B_ROWS * GRID_W
    halo = ATT_B_HALO * GRID_W
    n_tiles = n_rows // ATT_B_ROWS
    assert n_tiles >= 2
    per_tile = tile // halo
    halo_blocks = seq_len // halo
    n_steps = HEADS_B // 2 // ATT_B_PAIRS
    n_sub = ATT_B_ROWS // ATT_B_SUB
    sub_q = ATT_B_SUB * GRID_W
    win = ATT_B_WIN * GRID_W

    n_slots = 3 * ATT_B_PAIRS
    cur = pl.BlockSpec((1, 1, n_slots, tile, LANES), lambda hp, i, b: (b, hp, 0, i, 0))
    prev = pl.BlockSpec((1, 1, 2 * ATT_B_PAIRS, halo, LANES),
                        lambda hp, i, b: (b, hp, 0, jnp.maximum(i * per_tile - 1, 0), 0))
    nxt = pl.BlockSpec((1, 1, 2 * ATT_B_PAIRS, halo, LANES),
                       lambda hp, i, b: (b, hp, 0, jnp.minimum((i + 1) * per_tile, halo_blocks - 1), 0))

    def tile_kind(i):
        return jnp.where(i == 0, 0, jnp.where(i == n_tiles - 1, 2, 1))

    return pl.pallas_call(
        _attn_b_kernel,
        grid=(n_steps, n_tiles, batch),
        in_specs=[cur, prev, nxt,
                  pl.BlockSpec((2 * ATT_B_PAIRS, 1, n_sub, sub_q, win),
                               lambda hp, i, b: (hp, tile_kind(i), 0, 0, 0))],
        out_specs=pl.BlockSpec((1, tile, ATT_B_PAIRS * LANES), lambda hp, i, b: (b, i, hp)),
        out_shape=jax.ShapeDtypeStruct((batch, seq_len, OUT_B), BF16),
        scratch_shapes=[pltpu.VMEM((ATT_B_PAIRS, tile + 2 * halo, LANES), BF16),
                        pltpu.VMEM((ATT_B_PAIRS, tile + 2 * halo, 2 * LANES), BF16),
                        pltpu.VMEM((ATT_B_PAIRS, n_sub, 2 * sub_q, win), F32),
                        pltpu.VMEM((ATT_B_PAIRS, n_sub, 2 * sub_q, win), BF16)],
        compiler_params=_params("arbitrary", "arbitrary", "arbitrary"),
        name="attn_b",
    )(qkv, qkv, qkv, bias)


def _rope_layout():
    n = np.arange(QKV_CHUNK)
    pair, lane = n // LANES, n % LANES
    second, hb, i = lane // HEAD_DIM, (lane % HEAD_DIM) // (HEAD_DIM // 2), lane % (HEAD_DIM // 2)
    head = 2 * pair + hb
    return head * HEAD_DIM + second * (HEAD_DIM // 2) + i, head


def _segment_mean_matrix(head_of_column):
    same = head_of_column[:, None] == head_of_column[None, :]
    return jnp.asarray(np.where(same, 1.0 / HEAD_DIM, 0.0), BF16)


def _rope_tables(seq_len):
    half = HEAD_DIM // 2
    inv_freq = np.float32(ROPE_THETA) ** (-np.arange(half, dtype=np.float32) / np.float32(half))
    ang = (np.arange(seq_len, dtype=np.float32)[:, None] * inv_freq[None, :]).astype(np.float64)
    cos, sin = jnp.asarray(np.cos(ang), F32), jnp.asarray(np.sin(ang), F32)
    return (jnp.concatenate([cos, cos, cos, cos], axis=-1),
            jnp.concatenate([-sin, -sin, sin, sin], axis=-1))


def _prep_a(w_qkv, q_gain, k_gain):
    src, _ = _rope_layout()
    n_qk = 2 * N_GROUPS_A
    cols = np.concatenate([c * QKV_CHUNK + src for c in range(n_qk)]
                          + [np.arange(n_qk * QKV_CHUNK, QKV_A)])
    half = HEAD_DIM // 2

    def pair_lanes(gain):
        return jnp.concatenate([gain[:half], gain[:half], gain[half:], gain[half:]])

    n_pairs = N_GROUPS_A * PAIRS_PER_CHUNK
    gains = jnp.concatenate([jnp.tile(pair_lanes(q_gain) * Q_SCALE, n_pairs),
                             jnp.tile(pair_lanes(k_gain), n_pairs),
                             jnp.ones((QKV_A - n_qk * QKV_CHUNK,), F32)])
    return w_qkv[:, cols].astype(BF16), gains.reshape(1, QKV_A)


def _prep_b(q_gain, k_gain):
    gains = jnp.concatenate([jnp.tile(q_gain * Q_SCALE, HEADS_B),
                             jnp.tile(k_gain, HEADS_B),
                             jnp.ones((OUT_B,), F32)])
    return gains.reshape(1, QKV_B)


def _bias_tile_plan():
    halo, rows, n_sub = ATT_B_HALO, ATT_B_ROWS, ATT_B_ROWS // ATT_B_SUB
    plan = {}
    for kind, (lo_min, lo_max) in enumerate(((halo, rows), (0, rows), (0, halo))):
        for sb in range(n_sub):
            ws = min(max(ATT_B_SUB * sb, lo_min), lo_max)
            for ql in range(ATT_B_SUB):
                rho = ATT_B_SUB * sb + ql
                lo = min(max(rho, lo_min), lo_max)
                for jj in range(ATT_B_WIN // 2):
                    jl = ws + 2 * jj
                    dr = jl - rho + NA_ROWS // 2 - 1
                    plan[kind, sb, ql, jj] = (dr, lo <= jl < lo + NA_ROWS, lo <= jl + 1 < lo + NA_ROWS)
    return plan


def _bias_expand_kernel(c_ref, o_ref):
    w = GRID_W
    left = lax.broadcasted_iota(jnp.int32, (w, 2 * w), 1) < w
    neg = jnp.full((w, 2 * w), NEG_INF, F32)
    for (kind, sb, ql, jj), (dr, ok_a, ok_b) in _bias_tile_plan().items():
        if ok_a or ok_b:
            tile = c_ref[0, dr + 1]
            if not ok_b:
                tile = jnp.where(left, tile, neg)
            elif not ok_a:
                tile = jnp.where(left, neg, tile)
        else:
            tile = neg
        o_ref[0, kind, sb, ql * w:(ql + 1) * w, 2 * jj * w:2 * (jj + 1) * w] = tile


def _bias_table_b(rpb):
    w, n_sub, n_dr = GRID_W, ATT_B_ROWS // ATT_B_SUB, 2 * NA_ROWS - 1
    col = np.arange(w)
    col_start = np.clip(col - NA_COLS // 2, 0, w - NA_COLS)
    col_ok = (col[None, :] >= col_start[:, None]) & (col[None, :] < col_start[:, None] + NA_COLS)
    dc = np.clip(col[None, :] - col[:, None] + NA_COLS - 1, 0, 2 * NA_COLS - 2)
    pick = jnp.asarray(np.arange(2 * NA_COLS - 1)[:, None, None] == dc[None], F32)
    by_row = jnp.einsum("hrc,cqk->hrqk", rpb * LOG2E, pick, precision=lax.Precision.HIGHEST)
    by_row = jnp.where(col_ok[None, None], by_row, NEG_INF)
    pad = jnp.full((HEADS_B, 1, w, w), NEG_INF, F32)
    by_row = jnp.concatenate([pad, by_row, pad], axis=1)
    pairs = jnp.concatenate([by_row[:, :-1], by_row[:, 1:]], axis=-1)
    return pl.pallas_call(
        _bias_expand_kernel,
        grid=(HEADS_B,),
        in_specs=[pl.BlockSpec((1, n_dr + 1, w, 2 * w), lambda h: (h, 0, 0, 0))],
        out_specs=pl.BlockSpec((1, 3, n_sub, ATT_B_SUB * w, ATT_B_WIN * w), lambda h: (h, 0, 0, 0, 0)),
        out_shape=jax.ShapeDtypeStruct((HEADS_B, 3, n_sub, ATT_B_SUB * w, ATT_B_WIN * w), F32),
        compiler_params=_params("parallel"),
        name="bias_expand",
    )(pairs)


def _mixer_a(x, batch, seq_len, mix_g, w_qkv, gains, seg, rope_tables):
    qkv = _qkv_a(x, mix_g, w_qkv, gains, seg, rope_tables, batch, seq_len)
    outs, lses = [], []
    for group, (window, dilation) in enumerate(DILATED_GROUPS):
        assert window // (2 * dilation) == ATT_A_SIDE
        o, lse = _attn_a_group(qkv[group], dilation, batch, seq_len)
        outs.append(o)
        lses.append(lse)
    return _merge_a(outs, lses, batch, seq_len)


def _mixer_b(x, batch, seq_len, mix_g, w_qkv, gains, bias, seg):
    qkv = _qkv_b(x, mix_g, w_qkv, gains, seg, batch, seq_len)
    return _attn_b(qkv, bias, batch, seq_len).reshape(batch * seq_len, OUT_B)


def kernel(x_prompt, x_sample, ffn1_norm, ffn1_w_in, ffn1_w_out, mix_norm, ffn2_norm, ffn2_w_in, ffn2_w_out, a_w_qkv, a_q_norm, a_k_norm, a_w_o, b_w_qkv, b_q_norm, b_k_norm, b_rpb, b_w_o):
    depth = ffn1_norm.shape[0]
    bf = lambda w: w.astype(BF16)
    ffn1_w_in, ffn1_w_out, ffn2_w_in, ffn2_w_out = map(bf, (ffn1_w_in, ffn1_w_out, ffn2_w_in, ffn2_w_out))
    a_w_o, b_w_qkv, b_w_o = map(bf, (a_w_o, b_w_qkv, b_w_o))
    seg_a = _segment_mean_matrix(_rope_layout()[1][:SEG_COLS])
    seg_b = _segment_mean_matrix(np.arange(SEG_COLS) // HEAD_DIM)
    prep_a = [_prep_a(a_w_qkv[j], a_q_norm[j], a_k_norm[j]) for j in range(a_w_qkv.shape[0])]
    prep_b = [(_prep_b(b_q_norm[j], b_k_norm[j]), _bias_table_b(b_rpb[j])) for j in range(b_w_qkv.shape[0])]
    rope_tables = _rope_tables(max(x_prompt.shape[1], x_sample.shape[1]))

    def trunk(x3):
        batch, seq_len, _ = x3.shape
        x = x3.reshape(batch * seq_len, D_MODEL)
        for i in range(depth):
            j = i // 2
            x = _ffn(x, ffn1_norm[i], ffn1_w_in, ffn1_w_out, i)
            if i % 2 == 0:
                w_qkv, gains = prep_a[j]
                attn = (_mixer_a(x, batch, seq_len, mix_norm[i], w_qkv, gains, seg_a, rope_tables), a_w_o[j])
            else:
                gains, bias = prep_b[j]
                attn = (_mixer_b(x, batch, seq_len, mix_norm[i], b_w_qkv[j], gains, bias, seg_b), b_w_o[j])
            x = _ffn(x, ffn2_norm[i], ffn2_w_in, ffn2_w_out, i, attn)
        return x.reshape(batch, seq_len, D_MODEL)

    return (trunk(x_prompt), trunk(x_sample))
```

```python
import functools

import jax
import jax.numpy as jnp
import numpy as np
from jax import lax
from jax.experimental import pallas as pl
from jax.experimental.pallas import tpu as pltpu

D_MODEL = 1024
HEAD_DIM = 64
D_FF = 2816
RMS_EPS = 1e-6
ROPE_THETA = 10000.0
NEG_INF = -1e30
LOG2E = 1.4426950408889634
DILATED_GROUPS = ((128, 1), (512, 4), (2048, 16))
HEADS_A = 8
N_GROUPS_A = 3
QKV_A = 3 * N_GROUPS_A * HEADS_A * HEAD_DIM
OUT_A = HEADS_A * HEAD_DIM
HEADS_B = 16
QKV_B = 3 * HEADS_B * HEAD_DIM
OUT_B = HEADS_B * HEAD_DIM
GRID_W = 64
NA_ROWS = 8
NA_COLS = 16

LANES = 128
VMEM_LIMIT_BYTES = 56 * 1024 * 1024

TOKEN_TILE = 1024
QKV_TILE = 1024
FFN_TILE = 1024
FF_CHUNK = 256
SEG_COLS = 256
QKV_CHUNK = 512
PAIRS_PER_CHUNK = QKV_CHUNK // LANES
ATT_A_TILE = 512
ATT_A_QB = 128
KVQ_OFFSET = (2 * QKV_CHUNK, 0, QKV_CHUNK)
ATT_A_SIDE = 64
LSE_REP = LANES // HEADS_A
ATT_B_ROWS = 8
ATT_B_HALO = 4
ATT_B_PAIRS = 4
ATT_B_SUB = 2
ATT_B_WIN = NA_ROWS + ATT_B_SUB

BF16 = jnp.bfloat16
F32 = jnp.float32
NT_DIMS = (((1,), (1,)), ((), ()))
Q_SCALE = HEAD_DIM ** -0.5 * LOG2E


def _params(*semantics):
    return pltpu.CompilerParams(dimension_semantics=semantics,
                                vmem_limit_bytes=VMEM_LIMIT_BYTES)


def _rms_rows(x, g):
    ms = jnp.mean(x * x, axis=-1, keepdims=True)
    return (x * lax.rsqrt(ms + RMS_EPS)) * g


def _ffn_kernel(*refs, fused_proj):
    if fused_proj:
        x_ref, a_ref, wa_ref, g_ref, win_ref, wout_ref, o_ref, hn_ref = refs
        o_ref[...] = x_ref[...] + jnp.dot(a_ref[...], wa_ref[...], preferred_element_type=F32)
    else:
        x_ref, g_ref, win_ref, wout_ref, o_ref, hn_ref = refs
        o_ref[...] = x_ref[...]
    hn_ref[...] = _rms_rows(o_ref[...], g_ref[...]).astype(BF16)
    for c0 in range(0, D_FF, FF_CHUNK):
        width = min(FF_CHUNK, D_FF - c0)
        h = hn_ref[...]
        gate = jnp.dot(h, win_ref[:, c0:c0 + width], preferred_element_type=F32)
        up = jnp.dot(h, win_ref[:, D_FF + c0:D_FF + c0 + width], preferred_element_type=F32)
        a = (gate * jax.nn.sigmoid(gate) * up).astype(BF16)
        o_ref[...] += 0.5 * jnp.dot(a, wout_ref[c0:c0 + width, :], preferred_element_type=F32)


def _ffn(x, g, w_in, w_out, layer, attn=None):
    t = x.shape[0]
    const = lambda i: (0, 0)
    this_layer = lambda i: (layer, 0, 0)
    resident = pl.Buffered(1)
    row = pl.BlockSpec((FFN_TILE, D_MODEL), lambda i: (i, 0))
    in_specs, args = [row], [x]
    if attn is not None:
        a, w_a = attn
        in_specs += [pl.BlockSpec((FFN_TILE, a.shape[1]), lambda i: (i, 0)),
                     pl.BlockSpec(w_a.shape, const, pipeline_mode=resident)]
        args += [a, w_a]
    in_specs += [pl.BlockSpec((1, D_MODEL), const),
                 pl.BlockSpec((None, D_MODEL, 2 * D_FF), this_layer, pipeline_mode=resident),
                 pl.BlockSpec((None, D_FF, D_MODEL), this_layer, pipeline_mode=resident)]
    args += [g.reshape(1, D_MODEL), w_in, w_out]
    return pl.pallas_call(
        functools.partial(_ffn_kernel, fused_proj=attn is not None),
        grid=(t // FFN_TILE,),
        in_specs=in_specs,
        out_specs=row,
        out_shape=jax.ShapeDtypeStruct((t, D_MODEL), F32),
        scratch_shapes=[pltpu.VMEM((FFN_TILE, D_MODEL), BF16)],
        compiler_params=_params("parallel"),
        name="ffn_proj" if attn is not None else "ffn",
    )(*args)


def _project_chunk(hn_ref, w_ref, gain_ref, seg_ref, c, normed):
    cols = slice(c * QKV_CHUNK, (c + 1) * QKV_CHUNK)
    y = jnp.dot(hn_ref[...], w_ref[:, cols], preferred_element_type=F32)
    if normed:
        y2 = (y * y).astype(BF16)
        ms = jnp.concatenate(
            [jnp.dot(y2[:, s0:s0 + SEG_COLS], seg_ref[...], preferred_element_type=F32)
             for s0 in range(0, QKV_CHUNK, SEG_COLS)], axis=-1)
        y = (y * lax.rsqrt(ms + RMS_EPS)) * gain_ref[:, cols]
    return y


def _qkv_a_kernel(x_ref, g_ref, w_ref, gain_ref, seg_ref, cos_ref, sin_ref, *rest):
    n_chunks = 3 * N_GROUPS_A
    out_refs, (hn_ref, ys_ref) = rest[:N_GROUPS_A], rest[N_GROUPS_A:]
    hn_ref[...] = _rms_rows(x_ref[...], g_ref[...]).astype(BF16)
    cos, sin = cos_ref[...], sin_ref[...]
    for c in range(n_chunks):
        kind, group = divmod(c, N_GROUPS_A)
        dilation = DILATED_GROUPS[group][1]
        normed = kind < 2
        out_ref, off = out_refs[group], KVQ_OFFSET[kind]
        y = _project_chunk(hn_ref, w_ref, gain_ref, seg_ref, c, normed)
        if normed:
            y = jnp.concatenate(
                [y[:, p * LANES:(p + 1) * LANES] * cos
                 + pltpu.roll(y[:, p * LANES:(p + 1) * LANES], LANES // 2, axis=1) * sin
                 for p in range(PAIRS_PER_CHUNK)], axis=-1)
        if dilation == 1:
            out_ref[0, 0, :, off:off + QKV_CHUNK] = y.astype(BF16)
        else:
            slot = kind * (N_GROUPS_A - 1) + group - 1
            rows = QKV_TILE // dilation
            for p in range(PAIRS_PER_CHUNK):
                ys_ref[slot, p] = y[:, p * LANES:(p + 1) * LANES]
                dst = slice(off + p * LANES, off + (p + 1) * LANES)
                for r in range(dilation):
                    out_ref[0, r, :, dst] = ys_ref[slot, p, pl.ds(r, rows, stride=dilation), :].astype(BF16)


def _qkv_a(x, g, w, gains, seg, rope_tables, batch, seq_len):
    t = x.shape[0]
    tps = seq_len // QKV_TILE
    const = lambda i: (0, 0)
    in_specs = [
        pl.BlockSpec((QKV_TILE, D_MODEL), lambda i: (i, 0)),
        pl.BlockSpec((1, D_MODEL), const),
        pl.BlockSpec((D_MODEL, QKV_A), const, pipeline_mode=pl.Buffered(1)),
        pl.BlockSpec((1, QKV_A), const),
        pl.BlockSpec((SEG_COLS, SEG_COLS), const),
        pl.BlockSpec((QKV_TILE, LANES), lambda i: (i % tps, 0)),
        pl.BlockSpec((QKV_TILE, LANES), lambda i: (i % tps, 0)),
    ]
    out_specs, out_shapes = [], []
    for _, d in DILATED_GROUPS:
        out_specs.append(pl.BlockSpec((1, d, QKV_TILE // d, 3 * QKV_CHUNK),
                                      lambda i: (i // tps, 0, i % tps, 0)))
        out_shapes.append(jax.ShapeDtypeStruct((batch, d, seq_len // d, 3 * QKV_CHUNK), BF16))
    return pl.pallas_call(
        _qkv_a_kernel,
        grid=(t // QKV_TILE,),
        in_specs=in_specs,
        out_specs=out_specs,
        out_shape=out_shapes,
        scratch_shapes=[pltpu.VMEM((QKV_TILE, D_MODEL), BF16),
                        pltpu.VMEM((3 * (N_GROUPS_A - 1), PAIRS_PER_CHUNK, QKV_TILE, LANES), F32)],
        compiler_params=_params("parallel"),
        name="qkv_a",
    )(x, g.reshape(1, D_MODEL), w, gains, seg, *rope_tables)


def _qkv_b_kernel(x_ref, g_ref, w_ref, gain_ref, seg_ref, o_ref, hn_ref):
    hn_ref[...] = _rms_rows(x_ref[...], g_ref[...]).astype(BF16)
    for c in range(QKV_B // QKV_CHUNK):
        y = _project_chunk(hn_ref, w_ref, gain_ref, seg_ref, c, c < 2 * OUT_B // QKV_CHUNK)
        kind, step = divmod(c, OUT_B // QKV_CHUNK)
        for p in range(PAIRS_PER_CHUNK):
            o_ref[0, step, KVQ_OFFSET[kind] // LANES + p] = y[:, p * LANES:(p + 1) * LANES].astype(BF16)


def _qkv_b(x, g, w, gains, seg, batch, seq_len):
    t = x.shape[0]
    tps = seq_len // QKV_TILE
    n_steps = OUT_B // QKV_CHUNK
    n_slots = 3 * PAIRS_PER_CHUNK
    assert PAIRS_PER_CHUNK == ATT_B_PAIRS
    const = lambda i: (0, 0)
    return pl.pallas_call(
        _qkv_b_kernel,
        grid=(t // QKV_TILE,),
        in_specs=[
            pl.BlockSpec((QKV_TILE, D_MODEL), lambda i: (i, 0)),
            pl.BlockSpec((1, D_MODEL), const),
            pl.BlockSpec((D_MODEL, QKV_B), const, pipeline_mode=pl.Buffered(1)),
            pl.BlockSpec((1, QKV_B), const),
            pl.BlockSpec((SEG_COLS, SEG_COLS), const),
        ],
        out_specs=pl.BlockSpec((1, n_steps, n_slots, QKV_TILE, LANES), lambda i: (i // tps, 0, 0, i % tps, 0)),
        out_shape=jax.ShapeDtypeStruct((batch, n_steps, n_slots, seq_len, LANES), BF16),
        scratch_shapes=[pltpu.VMEM((QKV_TILE, D_MODEL), BF16)],
        compiler_params=_params("parallel"),
        name="qkv_b",
    )(x, g.reshape(1, D_MODEL), w, gains, seg)


def _attn_a_kernel(cur_ref, prev_ref, next_ref, o_ref, lse_ref, kw_ref, vw_ref, s_ref, p_ref, m_ref, d_ref,
                   *, seq_len, tq):
    side, qb = ATT_A_SIDE, ATT_A_QB
    kb = qb + 2 * side
    nb = tq // qb
    q_off, k_off, v_off = KVQ_OFFSET
    kw_ref[0:side] = prev_ref[0, 0, :, k_off:k_off + OUT_A]
    kw_ref[side:side + tq] = cur_ref[0, 0, :, k_off:k_off + OUT_A]
    kw_ref[side + tq:] = next_ref[0, 0, :, k_off:k_off + OUT_A]
    ones = jnp.ones((tq + 2 * side, LANES), BF16)
    for p in range(PAIRS_PER_CHUNK):
        src = slice(v_off + p * LANES, v_off + (p + 1) * LANES)
        dst = slice(2 * p * LANES, (2 * p + 1) * LANES)
        vw_ref[0:side, dst] = prev_ref[0, 0, :, src]
        vw_ref[side:side + tq, dst] = cur_ref[0, 0, :, src]
        vw_ref[side + tq:, dst] = next_ref[0, 0, :, src]
        vw_ref[:, (2 * p + 1) * LANES:(2 * p + 2) * LANES] = ones

    q_start = pl.program_id(2) * tq
    row = lax.broadcasted_iota(jnp.int32, (qb, kb), 0)
    col = lax.broadcasted_iota(jnp.int32, (qb, kb), 1)
    band = (col >= row) & (col - row <= 2 * side)
    biases = []
    for b in range(nb):
        kpos = q_start + b * qb + col - side
        bias = jnp.where(band & (kpos >= 0) & (kpos < seq_len), 0.0, NEG_INF).astype(F32)
        biases.append(jnp.concatenate([bias, bias], axis=0))
    bias4 = jnp.stack(biases)

    lane = lax.broadcasted_iota(jnp.int32, (tq, LANES), 1)
    q_head_a = (lane % HEAD_DIM) < (HEAD_DIM // 2)
    v_head_a = lax.broadcasted_iota(jnp.int32, (qb, LANES), 1) < HEAD_DIM
    lse_lane = lax.broadcasted_iota(jnp.int32, (qb, LANES), 1) // LSE_REP
    zero = jnp.zeros((tq, LANES), BF16)

    for p in range(PAIRS_PER_CHUNK):
        cols = slice(p * LANES, (p + 1) * LANES)
        q2 = cur_ref[0, 0, :, q_off + p * LANES:q_off + (p + 1) * LANES]
        qa = jnp.where(q_head_a, q2, zero)
        qbb = jnp.where(q_head_a, zero, q2)
        for b in range(nb):
            rows = slice(b * qb, (b + 1) * qb)
            lhs = jnp.concatenate([qa[rows], qbb[rows]], axis=0)
            s_ref[p * nb + b] = lax.dot_general(lhs, kw_ref[b * qb:b * qb + kb, cols], NT_DIMS,
                                                preferred_element_type=F32)
        blk = slice(p * nb, (p + 1) * nb)
        s = s_ref[blk] + bias4
        m = jnp.max(s, axis=-1, keepdims=True)
        p_ref[blk] = jnp.exp2(s - m).astype(BF16)
        m_ref[blk] = jnp.broadcast_to(m, (nb, 2 * qb, LANES))
        for b in range(nb):
            rows = slice(b * qb, (b + 1) * qb)
            pv = jnp.dot(p_ref[p * nb + b], vw_ref[b * qb:b * qb + kb, 2 * p * LANES:(2 * p + 2) * LANES],
                         preferred_element_type=F32)
            den = pv[:, LANES:]
            o = pv[:, :LANES] * (1.0 / den)
            o_ref[0, 0, rows, cols] = jnp.where(v_head_a, o[:qb], o[qb:]).astype(BF16)
            mx = m_ref[p * nb + b]
            prev_m = lse_ref[0, 0, rows, :] if p else jnp.zeros((qb, LANES), F32)
            prev_d = d_ref[rows, :] if p else jnp.ones((qb, LANES), F32)
            lse_ref[0, 0, rows, :] = jnp.where(lse_lane == 2 * p, mx[:qb],
                                               jnp.where(lse_lane == 2 * p + 1, mx[qb:], prev_m))
            d_ref[rows, :] = jnp.where(lse_lane == 2 * p, den[:qb],
                                       jnp.where(lse_lane == 2 * p + 1, den[qb:], prev_d))
    lse_ref[0, 0] = lse_ref[0, 0] + jnp.log2(d_ref[...])


def _attn_a_group(kvq, dilation, batch, seq_len):
    length = seq_len // dilation
    tq, side = min(ATT_A_TILE, length), ATT_A_SIDE
    n_tiles = length // tq
    halo_blocks = length // side
    nb = tq // ATT_A_QB
    kb = ATT_A_QB + 2 * side
    out = pl.BlockSpec((1, 1, tq, OUT_A), lambda b, r, i: (b, r, i, 0))
    cur = pl.BlockSpec((1, 1, tq, 3 * OUT_A), lambda b, r, i: (b, r, i, 0))
    prev = pl.BlockSpec((1, 1, side, 2 * OUT_A),
                        lambda b, r, i: (b, r, jnp.maximum(i * (tq // side) - 1, 0), 0))
    nxt = pl.BlockSpec((1, 1, side, 2 * OUT_A),
                       lambda b, r, i: (b, r, jnp.minimum((i + 1) * (tq // side), halo_blocks - 1), 0))
    n_blk = PAIRS_PER_CHUNK * nb
    lse_spec = pl.BlockSpec((1, 1, tq, LANES), lambda b, r, i: (b, r, i, 0))
    return pl.pallas_call(
        functools.partial(_attn_a_kernel, seq_len=length, tq=tq),
        grid=(batch, dilation, n_tiles),
        in_specs=[cur, prev, nxt],
        out_specs=[out, lse_spec],
        out_shape=[jax.ShapeDtypeStruct((batch, dilation, length, OUT_A), BF16),
                   jax.ShapeDtypeStruct((batch, dilation, length, LANES), F32)],
        scratch_shapes=[pltpu.VMEM((tq + 2 * side, OUT_A), BF16),
                        pltpu.VMEM((tq + 2 * side, 2 * OUT_A), BF16),
                        pltpu.VMEM((n_blk, 2 * ATT_A_QB, kb), F32),
                        pltpu.VMEM((n_blk, 2 * ATT_A_QB, kb), BF16),
                        pltpu.VMEM((n_blk, 2 * ATT_A_QB, LANES), F32),
                        pltpu.VMEM((tq, LANES), F32)],
        compiler_params=_params("parallel", "parallel", "arbitrary"),
        name=f"attn_a_d{dilation}",
    )(kvq, kvq, kvq)


def _merge_a_kernel(o0_ref, o1_ref, o2_ref, l0_ref, l1_ref, l2_ref, e_ref, y_ref, ot_ref, lt_ref):
    for slot, (o_ref, l_ref, dilation) in enumerate(
            ((o1_ref, l1_ref, DILATED_GROUPS[1][1]), (o2_ref, l2_ref, DILATED_GROUPS[2][1]))):
        rows = TOKEN_TILE // dilation
        for r in range(dilation):
            lt_ref[slot, pl.ds(r, rows, stride=dilation), :] = l_ref[0, r]
            for p in range(PAIRS_PER_CHUNK):
                cols = slice(p * LANES, (p + 1) * LANES)
                ot_ref[slot, p, pl.ds(r, rows, stride=dilation), :] = o_ref[0, r, :, cols].astype(F32)
    l0, l1, l2 = l0_ref[0, 0], lt_ref[0], lt_ref[1]
    m = jnp.maximum(jnp.maximum(l0, l1), l2)
    e0, e1, e2 = jnp.exp2(l0 - m), jnp.exp2(l1 - m), jnp.exp2(l2 - m)
    inv = 1.0 / (e0 + e1 + e2)

    def spread(w):
        hi = w.astype(BF16)
        lo = (w - hi.astype(F32)).astype(BF16)
        return (jnp.dot(hi, e_ref[...], preferred_element_type=F32)
                + jnp.dot(lo, e_ref[...], preferred_element_type=F32))

    w0, w1, w2 = spread(e0 * inv), spread(e1 * inv), spread(e2 * inv)
    for p in range(PAIRS_PER_CHUNK):
        cols = slice(p * LANES, (p + 1) * LANES)
        y_ref[:, cols] = (w0[:, cols] * o0_ref[0, 0, :, cols].astype(F32) + w1[:, cols] * ot_ref[0, p]
                          + w2[:, cols] * ot_ref[1, p]).astype(BF16)


def _merge_a(outs, lses, batch, seq_len):
    t = batch * seq_len
    tps = seq_len // TOKEN_TILE
    grp = [pl.BlockSpec((1, d, TOKEN_TILE // d, OUT_A), lambda i: (i // tps, 0, i % tps, 0))
           for _, d in DILATED_GROUPS]
    grp_lse = [pl.BlockSpec((1, d, TOKEN_TILE // d, LANES), lambda i: (i // tps, 0, i % tps, 0))
               for _, d in DILATED_GROUPS]
    spread = jnp.asarray(np.arange(LANES)[:, None] == LSE_REP * (np.arange(OUT_A)[None, :] // HEAD_DIM), BF16)
    return pl.pallas_call(
        _merge_a_kernel,
        grid=(t // TOKEN_TILE,),
        in_specs=grp + grp_lse + [pl.BlockSpec((LANES, OUT_A), lambda i: (0, 0))],
        out_specs=pl.BlockSpec((TOKEN_TILE, OUT_A), lambda i: (i, 0)),
        out_shape=jax.ShapeDtypeStruct((t, OUT_A), BF16),
        scratch_shapes=[pltpu.VMEM((N_GROUPS_A - 1, PAIRS_PER_CHUNK, TOKEN_TILE, LANES), F32),
                        pltpu.VMEM((N_GROUPS_A - 1, TOKEN_TILE, LANES), F32)],
        compiler_params=_params("parallel"),
        name="merge_a",
    )(*outs, *lses, spread)


def _attn_b_kernel(cur_ref, prev_ref, next_ref, bias_ref, o_ref, kw_ref, vw_ref, s_ref, p_ref):
    w, halo, rows = GRID_W, ATT_B_HALO, ATT_B_ROWS
    n_sub = rows // ATT_B_SUB
    sub_q = ATT_B_SUB * w
    win = ATT_B_WIN * w
    ones = jnp.ones(((rows + 2 * halo) * w, LANES), BF16)
    q_slot, k_slot, v_slot = (off // LANES for off in KVQ_OFFSET)
    for u in range(ATT_B_PAIRS):
        kw_ref[u, 0:halo * w] = prev_ref[0, 0, k_slot + u]
        kw_ref[u, halo * w:(halo + rows) * w] = cur_ref[0, 0, k_slot + u]
        kw_ref[u, (halo + rows) * w:] = next_ref[0, 0, k_slot + u]
        vw_ref[u, 0:halo * w, :LANES] = prev_ref[0, 0, v_slot + u]
        vw_ref[u, halo * w:(halo + rows) * w, :LANES] = cur_ref[0, 0, v_slot + u]
        vw_ref[u, (halo + rows) * w:, :LANES] = next_ref[0, 0, v_slot + u]
        vw_ref[u, :, LANES:] = ones

    i = pl.program_id(2)
    lo_min = jnp.where(i == 0, halo, 0)
    lo_max = jnp.where(i == pl.num_programs(2) - 1, halo, rows)
    starts = [pl.multiple_of(jnp.clip(ATT_B_SUB * sb, lo_min, lo_max) * w, LANES) for sb in range(n_sub)]
    head_a = lax.broadcasted_iota(jnp.int32, (rows * w, LANES), 1) < HEAD_DIM
    out_a = lax.broadcasted_iota(jnp.int32, (sub_q, LANES), 1) < HEAD_DIM
    zero = jnp.zeros((rows * w, LANES), BF16)
    for u in range(ATT_B_PAIRS):
        q2 = cur_ref[0, 0, q_slot + u]
        qa = jnp.where(head_a, q2, zero)
        qb = jnp.where(head_a, zero, q2)
        for sb in range(n_sub):
            rs = slice(sb * sub_q, (sb + 1) * sub_q)
            lhs = jnp.concatenate([qa[rs], qb[rs]], axis=0)
            s_ref[u, sb] = lax.dot_general(lhs, kw_ref[u, pl.ds(starts[sb], win), :], NT_DIMS,
                                           preferred_element_type=F32)
    for u in range(ATT_B_PAIRS):
        s = s_ref[u] + jnp.concatenate([bias_ref[2 * u, 0], bias_ref[2 * u + 1, 0]], axis=1)
        m = jnp.max(s, axis=-1, keepdims=True)
        p_ref[u] = jnp.exp2(s - m).astype(BF16)
    for u in range(ATT_B_PAIRS):
        for sb in range(n_sub):
            pv = jnp.dot(p_ref[u, sb], vw_ref[u, pl.ds(starts[sb], win), :], preferred_element_type=F32)
            o = pv[:, :LANES] / pv[:, LANES:]
            o_ref[0, sb * sub_q:(sb + 1) * sub_q, u * LANES:(u + 1) * LANES] = (
                jnp.where(out_a, o[:sub_q], o[sub_q:]).astype(BF16))


def _attn_b(qkv, bias, batch, seq_len):
    n_rows = seq_len // GRID_W
    tile = ATT_B_ROWS * GRID_W
    halo = ATT_B_HALO * GRID_W
    n_tiles = n_rows // ATT_B_ROWS
    assert n_tiles >= 2
    per_tile = tile // halo
    halo_blocks = seq_len // halo
    n_steps = HEADS_B // 2 // ATT_B_PAIRS
    n_sub = ATT_B_ROWS // ATT_B_SUB
    sub_q = ATT_B_SUB * GRID_W
    win = ATT_B_WIN * GRID_W

    n_slots = 3 * ATT_B_PAIRS
    cur = pl.BlockSpec((1, 1, n_slots, tile, LANES), lambda hp, b, i: (b, hp, 0, i, 0))
    prev = pl.BlockSpec((1, 1, 2 * ATT_B_PAIRS, halo, LANES),
                        lambda hp, b, i: (b, hp, 0, jnp.maximum(i * per_tile - 1, 0), 0))
    nxt = pl.BlockSpec((1, 1, 2 * ATT_B_PAIRS, halo, LANES),
                       lambda hp, b, i: (b, hp, 0, jnp.minimum((i + 1) * per_tile, halo_blocks - 1), 0))

    def tile_kind(i):
        return jnp.where(i == 0, 0, jnp.where(i == n_tiles - 1, 2, 1))

    return pl.pallas_call(
        _attn_b_kernel,
        grid=(n_steps, batch, n_tiles),
        in_specs=[cur, prev, nxt,
                  pl.BlockSpec((2 * ATT_B_PAIRS, 1, n_sub, sub_q, win),
                               lambda hp, b, i: (hp, tile_kind(i), 0, 0, 0))],
        out_specs=pl.BlockSpec((1, tile, ATT_B_PAIRS * LANES), lambda hp, b, i: (b, i, hp)),
        out_shape=jax.ShapeDtypeStruct((batch, seq_len, OUT_B), BF16),
        scratch_shapes=[pltpu.VMEM((ATT_B_PAIRS, tile + 2 * halo, LANES), BF16),
                        pltpu.VMEM((ATT_B_PAIRS, tile + 2 * halo, 2 * LANES), BF16),
                        pltpu.VMEM((ATT_B_PAIRS, n_sub, 2 * sub_q, win), F32),
                        pltpu.VMEM((ATT_B_PAIRS, n_sub, 2 * sub_q, win), BF16)],
        compiler_params=_params("arbitrary", "arbitrary", "arbitrary"),
        name="attn_b",
    )(qkv, qkv, qkv, bias)


def _rope_layout():
    n = np.arange(QKV_CHUNK)
    pair, lane = n // LANES, n % LANES
    second, hb, i = lane // HEAD_DIM, (lane % HEAD_DIM) // (HEAD_DIM // 2), lane % (HEAD_DIM // 2)
    head = 2 * pair + hb
    return head * HEAD_DIM + second * (HEAD_DIM // 2) + i, head


def _segment_mean_matrix(head_of_column):
    same = head_of_column[:, None] == head_of_column[None, :]
    return jnp.asarray(np.where(same, 1.0 / HEAD_DIM, 0.0), BF16)


def _rope_tables(seq_len):
    half = HEAD_DIM // 2
    inv_freq = np.float32(ROPE_THETA) ** (-np.arange(half, dtype=np.float32) / np.float32(half))
    ang = (np.arange(seq_len, dtype=np.float32)[:, None] * inv_freq[None, :]).astype(np.float64)
    cos, sin = jnp.asarray(np.cos(ang), F32), jnp.asarray(np.sin(ang), F32)
    return (jnp.concatenate([cos, cos, cos, cos], axis=-1),
            jnp.concatenate([-sin, -sin, sin, sin], axis=-1))


def _prep_a(w_qkv, q_gain, k_gain):
    src, _ = _rope_layout()
    n_qk = 2 * N_GROUPS_A
    cols = np.concatenate([c * QKV_CHUNK + src for c in range(n_qk)]
                          + [np.arange(n_qk * QKV_CHUNK, QKV_A)])
    half = HEAD_DIM // 2

    def pair_lanes(gain):
        return jnp.concatenate([gain[:half], gain[:half], gain[half:], gain[half:]])

    n_pairs = N_GROUPS_A * PAIRS_PER_CHUNK
    gains = jnp.concatenate([jnp.tile(pair_lanes(q_gain) * Q_SCALE, n_pairs),
                             jnp.tile(pair_lanes(k_gain), n_pairs),
                             jnp.ones((QKV_A - n_qk * QKV_CHUNK,), F32)])
    return w_qkv[:, cols].astype(BF16), gains.reshape(1, QKV_A)


def _prep_b(q_gain, k_gain):
    gains = jnp.concatenate([jnp.tile(q_gain * Q_SCALE, HEADS_B),
                             jnp.tile(k_gain, HEADS_B),
                             jnp.ones((OUT_B,), F32)])
    return gains.reshape(1, QKV_B)


def _bias_tile_plan():
    halo, rows, n_sub = ATT_B_HALO, ATT_B_ROWS, ATT_B_ROWS // ATT_B_SUB
    plan = {}
    for kind, (lo_min, lo_max) in enumerate(((halo, rows), (0, rows), (0, halo))):
        for sb in range(n_sub):
            ws = min(max(ATT_B_SUB * sb, lo_min), lo_max)
            for ql in range(ATT_B_SUB):
                rho = ATT_B_SUB * sb + ql
                lo = min(max(rho, lo_min), lo_max)
                for jj in range(ATT_B_WIN // 2):
                    jl = ws + 2 * jj
                    dr = jl - rho + NA_ROWS // 2 - 1
                    plan[kind, sb, ql, jj] = (dr, lo <= jl < lo + NA_ROWS, lo <= jl + 1 < lo + NA_ROWS)
    return plan


def _bias_expand_kernel(c_ref, o_ref):
    w = GRID_W
    left = lax.broadcasted_iota(jnp.int32, (w, 2 * w), 1) < w
    neg = jnp.full((w, 2 * w), NEG_INF, F32)
    for (kind, sb, ql, jj), (dr, ok_a, ok_b) in _bias_tile_plan().items():
        if ok_a or ok_b:
            tile = c_ref[0, dr + 1]
            if not ok_b:
                tile = jnp.where(left, tile, neg)
            elif not ok_a:
                tile = jnp.where(left, neg, tile)
        else:
            tile = neg
        o_ref[0, kind, sb, ql * w:(ql + 1) * w, 2 * jj * w:2 * (jj + 1) * w] = tile


def _bias_table_b(rpb):
    w, n_sub, n_dr = GRID_W, ATT_B_ROWS // ATT_B_SUB, 2 * NA_ROWS - 1
    col = np.arange(w)
    col_start = np.clip(col - NA_COLS // 2, 0, w - NA_COLS)
    col_ok = (col[None, :] >= col_start[:, None]) & (col[None, :] < col_start[:, None] + NA_COLS)
    dc = np.clip(col[None, :] - col[:, None] + NA_COLS - 1, 0, 2 * NA_COLS - 2)
    pick = jnp.asarray(np.arange(2 * NA_COLS - 1)[:, None, None] == dc[None], F32)
    by_row = jnp.einsum("hrc,cqk->hrqk", rpb * LOG2E, pick, precision=lax.Precision.HIGHEST)
    by_row = jnp.where(col_ok[None, None], by_row, NEG_INF)
    pad = jnp.full((HEADS_B, 1, w, w), NEG_INF, F32)
    by_row = jnp.concatenate([pad, by_row, pad], axis=1)
    pairs = jnp.concatenate([by_row[:, :-1], by_row[:, 1:]], axis=-1)
    return pl.pallas_call(
        _bias_expand_kernel,
        grid=(HEADS_B,),
        in_specs=[pl.BlockSpec((1, n_dr + 1, w, 2 * w), lambda h: (h, 0, 0, 0))],
        out_specs=pl.BlockSpec((1, 3, n_sub, ATT_B_SUB * w, ATT_B_WIN * w), lambda h: (h, 0, 0, 0, 0)),
        out_shape=jax.ShapeDtypeStruct((HEADS_B, 3, n_sub, ATT_B_SUB * w, ATT_B_WIN * w), F32),
        compiler_params=_params("parallel"),
        name="bias_expand",
    )(pairs)


def _mixer_a(x, batch, seq_len, mix_g, w_qkv, gains, seg, rope_tables):
    qkv = _qkv_a(x, mix_g, w_qkv, gains, seg, rope_tables, batch, seq_len)
    outs, lses = [], []
    for group, (window, dilation) in enumerate(DILATED_GROUPS):
        assert window // (2 * dilation) == ATT_A_SIDE
        o, lse = _attn_a_group(qkv[group], dilation, batch, seq_len)
        outs.append(o)
        lses.append(lse)
    return _merge_a(outs, lses, batch, seq_len)


def _mixer_b(x, batch, seq_len, mix_g, w_qkv, gains, bias, seg):
    qkv = _qkv_b(x, mix_g, w_qkv, gains, seg, batch, seq_len)
    return _attn_b(qkv, bias, batch, seq_len).reshape(batch * seq_len, OUT_B)


def kernel(x_prompt, x_sample, ffn1_norm, ffn1_w_in, ffn1_w_out, mix_norm, ffn2_norm, ffn2_w_in, ffn2_w_out, a_w_qkv, a_q_norm, a_k_norm, a_w_o, b_w_qkv, b_q_norm, b_k_norm, b_rpb, b_w_o):
    depth = ffn1_norm.shape[0]
    bf = lambda w: w.astype(BF16)
    ffn1_w_in, ffn1_w_out, ffn2_w_in, ffn2_w_out = map(bf, (ffn1_w_in, ffn1_w_out, ffn2_w_in, ffn2_w_out))
    a_w_o, b_w_qkv, b_w_o = map(bf, (a_w_o, b_w_qkv, b_w_o))
    seg_a = _segment_mean_matrix(_rope_layout()[1][:SEG_COLS])
    seg_b = _segment_mean_matrix(np.arange(SEG_COLS) // HEAD_DIM)
    prep_a = [_prep_a(a_w_qkv[j], a_q_norm[j], a_k_norm[j]) for j in range(a_w_qkv.shape[0])]
    prep_b = [(_prep_b(b_q_norm[j], b_k_norm[j]), _bias_table_b(b_rpb[j])) for j in range(b_w_qkv.shape[0])]
    rope_tables = _rope_tables(max(x_prompt.shape[1], x_sample.shape[1]))

    def trunk(x3):
        batch, seq_len, _ = x3.shape
        x = x3.reshape(batch * seq_len, D_MODEL)
        for i in range(depth):
            j = i // 2
            x = _ffn(x, ffn1_norm[i], ffn1_w_in, ffn1_w_out, i)
            if i % 2 == 0:
                w_qkv, gains = prep_a[j]
                attn = (_mixer_a(x, batch, seq_len, mix_norm[i], w_qkv, gains, seg_a, rope_tables), a_w_o[j])
            else:
                gains, bias = prep_b[j]
                attn = (_mixer_b(x, batch, seq_len, mix_norm[i], b_w_qkv[j], gains, bias, seg_b), b_w_o[j])
            x = _ffn(x, ffn2_norm[i], ffn2_w_in, ffn2_w_out, i, attn)
        return x.reshape(batch, seq_len, D_MODEL)

    return (trunk(x_prompt), trunk(x_sample))
```

```python
import functools

import jax
import jax.numpy as jnp
import numpy as np
from jax import lax
from jax.experimental import pallas as pl
from jax.experimental.pallas import tpu as pltpu

D_MODEL = 1024
HEAD_DIM = 64
D_FF = 2816
RMS_EPS = 1e-6
ROPE_THETA = 10000.0
NEG_INF = -1e30
LOG2E = 1.4426950408889634
DILATED_GROUPS = ((128, 1), (512, 4), (2048, 16))
HEADS_A = 8
N_GROUPS_A = 3
QKV_A = 3 * N_GROUPS_A * HEADS_A * HEAD_DIM
OUT_A = HEADS_A * HEAD_DIM
HEADS_B = 16
QKV_B = 3 * HEADS_B * HEAD_DIM
OUT_B = HEADS_B * HEAD_DIM
GRID_W = 64
NA_ROWS = 8
NA_COLS = 16

LANES = 128
VMEM_LIMIT_BYTES = 56 * 1024 * 1024

TOKEN_TILE = 1024
QKV_TILE = 1024
FFN_TILE = 1024
FF_CHUNK = 256
SEG_COLS = 256
QKV_CHUNK = 512
PAIRS_PER_CHUNK = QKV_CHUNK // LANES
ATT_A_TILE = 512
ATT_A_QB = 128
KVQ_OFFSET = (2 * QKV_CHUNK, 0, QKV_CHUNK)
ATT_A_SIDE = 64
LSE_REP = LANES // HEADS_A
ATT_B_ROWS = 8
ATT_B_HALO = 4
ATT_B_PAIRS = 4
ATT_B_SUB = 2
ATT_B_WIN = NA_ROWS + ATT_B_SUB

BF16 = jnp.bfloat16
F32 = jnp.float32
NT_DIMS = (((1,), (1,)), ((), ()))
Q_SCALE = HEAD_DIM ** -0.5 * LOG2E


def _params(*semantics):
    return pltpu.CompilerParams(dimension_semantics=semantics,
                                vmem_limit_bytes=VMEM_LIMIT_BYTES)


def _rms_rows(x, g):
    ms = jnp.mean(x * x, axis=-1, keepdims=True)
    return (x * lax.rsqrt(ms + RMS_EPS)) * g


def _ffn_kernel(*refs, fused_proj):
    if fused_proj:
        x_ref, a_ref, wa_ref, g_ref, win_ref, wout_ref, o_ref, hn_ref = refs
        o_ref[...] = x_ref[...] + jnp.dot(a_ref[...], wa_ref[...], preferred_element_type=F32)
    else:
        x_ref, g_ref, win_ref, wout_ref, o_ref, hn_ref = refs
        o_ref[...] = x_ref[...]
    hn_ref[...] = _rms_rows(o_ref[...], g_ref[...]).astype(BF16)
    for c0 in range(0, D_FF, FF_CHUNK):
        width = min(FF_CHUNK, D_FF - c0)
        h = hn_ref[...]
        gate = jnp.dot(h, win_ref[:, c0:c0 + width], preferred_element_type=F32)
        up = jnp.dot(h, win_ref[:, D_FF + c0:D_FF + c0 + width], preferred_element_type=F32)
        a = (gate * jax.nn.sigmoid(gate) * up).astype(BF16)
        o_ref[...] += 0.5 * jnp.dot(a, wout_ref[c0:c0 + width, :], preferred_element_type=F32)


def _ffn(x, g, w_in, w_out, layer, attn=None):
    t = x.shape[0]
    const = lambda i: (0, 0)
    this_layer = lambda i: (layer, 0, 0)
    resident = pl.Buffered(1)
    row = pl.BlockSpec((FFN_TILE, D_MODEL), lambda i: (i, 0))
    in_specs, args = [row], [x]
    if attn is not None:
        a, w_a = attn
        in_specs += [pl.BlockSpec((FFN_TILE, a.shape[1]), lambda i: (i, 0)),
                     pl.BlockSpec(w_a.shape, const, pipeline_mode=resident)]
        args += [a, w_a]
    in_specs += [pl.BlockSpec((1, D_MODEL), const),
                 pl.BlockSpec((None, D_MODEL, 2 * D_FF), this_layer, pipeline_mode=resident),
                 pl.BlockSpec((None, D_FF, D_MODEL), this_layer, pipeline_mode=resident)]
    args += [g.reshape(1, D_MODEL), w_in, w_out]
    return pl.pallas_call(
        functools.partial(_ffn_kernel, fused_proj=attn is not None),
        grid=(t // FFN_TILE,),
        in_specs=in_specs,
        out_specs=row,
        out_shape=jax.ShapeDtypeStruct((t, D_MODEL), F32),
        scratch_shapes=[pltpu.VMEM((FFN_TILE, D_MODEL), BF16)],
        compiler_params=_params("parallel"),
        name="ffn_proj" if attn is not None else "ffn",
    )(*args)


def _project_chunk(hn_ref, w_ref, gain_ref, seg_ref, c, normed):
    cols = slice(c * QKV_CHUNK, (c + 1) * QKV_CHUNK)
    y = jnp.dot(hn_ref[...], w_ref[:, cols], preferred_element_type=F32)
    if normed:
        y2 = (y * y).astype(BF16)
        ms = jnp.concatenate(
            [jnp.dot(y2[:, s0:s0 + SEG_COLS], seg_ref[...], preferred_element_type=F32)
             for s0 in range(0, QKV_CHUNK, SEG_COLS)], axis=-1)
        y = (y * lax.rsqrt(ms + RMS_EPS)) * gain_ref[:, cols]
    return y


def _qkv_a_kernel(x_ref, g_ref, w_ref, gain_ref, seg_ref, cos_ref, sin_ref, *rest):
    n_chunks = 3 * N_GROUPS_A
    out_refs, (hn_ref, ys_ref) = rest[:N_GROUPS_A], rest[N_GROUPS_A:]
    hn_ref[...] = _rms_rows(x_ref[...], g_ref[...]).astype(BF16)
    cos, sin = cos_ref[...], sin_ref[...]
    for c in range(n_chunks):
        kind, group = divmod(c, N_GROUPS_A)
        dilation = DILATED_GROUPS[group][1]
        normed = kind < 2
        out_ref, off = out_refs[group], KVQ_OFFSET[kind]
        y = _project_chunk(hn_ref, w_ref, gain_ref, seg_ref, c, normed)
        if normed:
            y = jnp.concatenate(
                [y[:, p * LANES:(p + 1) * LANES] * cos
                 + pltpu.roll(y[:, p * LANES:(p + 1) * LANES], LANES // 2, axis=1) * sin
                 for p in range(PAIRS_PER_CHUNK)], axis=-1)
        if dilation == 1:
            out_ref[0, 0, :, off:off + QKV_CHUNK] = y.astype(BF16)
        else:
            slot = kind * (N_GROUPS_A - 1) + group - 1
            rows = QKV_TILE // dilation
            for p in range(PAIRS_PER_CHUNK):
                ys_ref[slot, p] = y[:, p * LANES:(p + 1) * LANES]
                dst = slice(off + p * LANES, off + (p + 1) * LANES)
                for r in range(dilation):
                    out_ref[0, r, :, dst] = ys_ref[slot, p, pl.ds(r, rows, stride=dilation), :].astype(BF16)


def _qkv_a(x, g, w, gains, seg, rope_tables, batch, seq_len):
    t = x.shape[0]
    tps = seq_len // QKV_TILE
    const = lambda i: (0, 0)
    in_specs = [
        pl.BlockSpec((QKV_TILE, D_MODEL), lambda i: (i, 0)),
        pl.BlockSpec((1, D_MODEL), const),
        pl.BlockSpec((D_MODEL, QKV_A), const, pipeline_mode=pl.Buffered(1)),
        pl.BlockSpec((1, QKV_A), const),
        pl.BlockSpec((SEG_COLS, SEG_COLS), const),
        pl.BlockSpec((QKV_TILE, LANES), lambda i: (i % tps, 0)),
        pl.BlockSpec((QKV_TILE, LANES), lambda i: (i % tps, 0)),
    ]
    out_specs, out_shapes = [], []
    for _, d in DILATED_GROUPS:
        out_specs.append(pl.BlockSpec((1, d, QKV_TILE // d, 3 * QKV_CHUNK),
                                      lambda i: (i // tps, 0, i % tps, 0)))
        out_shapes.append(jax.ShapeDtypeStruct((batch, d, seq_len // d, 3 * QKV_CHUNK), BF16))
    return pl.pallas_call(
        _qkv_a_kernel,
        grid=(t // QKV_TILE,),
        in_specs=in_specs,
        out_specs=out_specs,
        out_shape=out_shapes,
        scratch_shapes=[pltpu.VMEM((QKV_TILE, D_MODEL), BF16),
                        pltpu.VMEM((3 * (N_GROUPS_A - 1), PAIRS_PER_CHUNK, QKV_TILE, LANES), F32)],
        compiler_params=_params("parallel"),
        name="qkv_a",
    )(x, g.reshape(1, D_MODEL), w, gains, seg, *rope_tables)


def _qkv_b_kernel(x_ref, g_ref, w_ref, gain_ref, seg_ref, o_ref, hn_ref):
    hn_ref[...] = _rms_rows(x_ref[...], g_ref[...]).astype(BF16)
    for c in range(QKV_B // QKV_CHUNK):
        y = _project_chunk(hn_ref, w_ref, gain_ref, seg_ref, c, c < 2 * OUT_B // QKV_CHUNK)
        kind, step = divmod(c, OUT_B // QKV_CHUNK)
        for p in range(PAIRS_PER_CHUNK):
            o_ref[0, step, KVQ_OFFSET[kind] // LANES + p] = y[:, p * LANES:(p + 1) * LANES].astype(BF16)


def _qkv_b(x, g, w, gains, seg, batch, seq_len):
    t = x.shape[0]
    tps = seq_len // QKV_TILE
    n_steps = OUT_B // QKV_CHUNK
    n_slots = 3 * PAIRS_PER_CHUNK
    assert PAIRS_PER_CHUNK == ATT_B_PAIRS
    const = lambda i: (0, 0)
    return pl.pallas_call(
        _qkv_b_kernel,
        grid=(t // QKV_TILE,),
        in_specs=[
            pl.BlockSpec((QKV_TILE, D_MODEL), lambda i: (i, 0)),
            pl.BlockSpec((1, D_MODEL), const),
            pl.BlockSpec((D_MODEL, QKV_B), const, pipeline_mode=pl.Buffered(1)),
            pl.BlockSpec((1, QKV_B), const),
            pl.BlockSpec((SEG_COLS, SEG_COLS), const),
        ],
        out_specs=pl.BlockSpec((1, n_steps, n_slots, QKV_TILE, LANES), lambda i: (i // tps, 0, 0, i % tps, 0)),
        out_shape=jax.ShapeDtypeStruct((batch, n_steps, n_slots, seq_len, LANES), BF16),
        scratch_shapes=[pltpu.VMEM((QKV_TILE, D_MODEL), BF16)],
        compiler_params=_params("parallel"),
        name="qkv_b",
    )(x, g.reshape(1, D_MODEL), w, gains, seg)


def _attn_a_kernel(cur_ref, prev_ref, next_ref, o_ref, lse_ref, kw_ref, vw_ref, s_ref, p_ref, m_ref, d_ref,
                   *, seq_len, tq):
    side, qb = ATT_A_SIDE, ATT_A_QB
    kb = qb + 2 * side
    nb = tq // qb
    q_off, k_off, v_off = KVQ_OFFSET
    kw_ref[0:side] = prev_ref[0, 0, :, k_off:k_off + OUT_A]
    kw_ref[side:side + tq] = cur_ref[0, 0, :, k_off:k_off + OUT_A]
    kw_ref[side + tq:] = next_ref[0, 0, :, k_off:k_off + OUT_A]
    ones = jnp.ones((tq + 2 * side, LANES), BF16)
    for p in range(PAIRS_PER_CHUNK):
        src = slice(v_off + p * LANES, v_off + (p + 1) * LANES)
        dst = slice(2 * p * LANES, (2 * p + 1) * LANES)
        vw_ref[0:side, dst] = prev_ref[0, 0, :, src]
        vw_ref[side:side + tq, dst] = cur_ref[0, 0, :, src]
        vw_ref[side + tq:, dst] = next_ref[0, 0, :, src]

    @pl.when((pl.program_id(0) == 0) & (pl.program_id(1) == 0) & (pl.program_id(2) == 0))
    def _():
        for p in range(PAIRS_PER_CHUNK):
            vw_ref[:, (2 * p + 1) * LANES:(2 * p + 2) * LANES] = ones

    q_start = pl.program_id(2) * tq
    row = lax.broadcasted_iota(jnp.int32, (qb, kb), 0)
    col = lax.broadcasted_iota(jnp.int32, (qb, kb), 1)
    band = (col >= row) & (col - row <= 2 * side)
    biases = []
    for b in range(nb):
        kpos = q_start + b * qb + col - side
        bias = jnp.where(band & (kpos >= 0) & (kpos < seq_len), 0.0, NEG_INF).astype(F32)
        biases.append(jnp.concatenate([bias, bias], axis=0))
    bias4 = jnp.stack(biases)

    lane = lax.broadcasted_iota(jnp.int32, (tq, LANES), 1)
    q_head_a = (lane % HEAD_DIM) < (HEAD_DIM // 2)
    v_head_a = lax.broadcasted_iota(jnp.int32, (qb, LANES), 1) < HEAD_DIM
    lse_lane = lax.broadcasted_iota(jnp.int32, (qb, LANES), 1) // LSE_REP
    zero = jnp.zeros((tq, LANES), BF16)

    for p in range(PAIRS_PER_CHUNK):
        cols = slice(p * LANES, (p + 1) * LANES)
        q2 = cur_ref[0, 0, :, q_off + p * LANES:q_off + (p + 1) * LANES]
        qa = jnp.where(q_head_a, q2, zero)
        qbb = jnp.where(q_head_a, zero, q2)
        for b in range(nb):
            rows = slice(b * qb, (b + 1) * qb)
            lhs = jnp.concatenate([qa[rows], qbb[rows]], axis=0)
            s_ref[p * nb + b] = lax.dot_general(lhs, kw_ref[b * qb:b * qb + kb, cols], NT_DIMS,
                                                preferred_element_type=F32)
        blk = slice(p * nb, (p + 1) * nb)
        s = s_ref[blk] + bias4
        m = jnp.max(s, axis=-1, keepdims=True)
        p_ref[blk] = jnp.exp2(s - m).astype(BF16)
        m_ref[blk] = jnp.broadcast_to(m, (nb, 2 * qb, LANES))
        for b in range(nb):
            rows = slice(b * qb, (b + 1) * qb)
            pv = jnp.dot(p_ref[p * nb + b], vw_ref[b * qb:b * qb + kb, 2 * p * LANES:(2 * p + 2) * LANES],
                         preferred_element_type=F32)
            den = pv[:, LANES:]
            o = pv[:, :LANES] * (1.0 / den)
            o_ref[0, 0, rows, cols] = jnp.where(v_head_a, o[:qb], o[qb:]).astype(BF16)
            mx = m_ref[p * nb + b]
            prev_m = lse_ref[0, 0, rows, :] if p else jnp.zeros((qb, LANES), F32)
            prev_d = d_ref[rows, :] if p else jnp.ones((qb, LANES), F32)
            lse_ref[0, 0, rows, :] = jnp.where(lse_lane == 2 * p, mx[:qb],
                                               jnp.where(lse_lane == 2 * p + 1, mx[qb:], prev_m))
            d_ref[rows, :] = jnp.where(lse_lane == 2 * p, den[:qb],
                                       jnp.where(lse_lane == 2 * p + 1, den[qb:], prev_d))
    lse_ref[0, 0] = lse_ref[0, 0] + jnp.log2(d_ref[...])


def _attn_a_group(kvq, dilation, batch, seq_len):
    length = seq_len // dilation
    tq, side = min(ATT_A_TILE, length), ATT_A_SIDE
    n_tiles = length // tq
    halo_blocks = length // side
    nb = tq // ATT_A_QB
    kb = ATT_A_QB + 2 * side
    out = pl.BlockSpec((1, 1, tq, OUT_A), lambda b, r, i: (b, r, i, 0))
    cur = pl.BlockSpec((1, 1, tq, 3 * OUT_A), lambda b, r, i: (b, r, i, 0))
    prev = pl.BlockSpec((1, 1, side, 2 * OUT_A),
                        lambda b, r, i: (b, r, jnp.maximum(i * (tq // side) - 1, 0), 0))
    nxt = pl.BlockSpec((1, 1, side, 2 * OUT_A),
                       lambda b, r, i: (b, r, jnp.minimum((i + 1) * (tq // side), halo_blocks - 1), 0))
    n_blk = PAIRS_PER_CHUNK * nb
    lse_spec = pl.BlockSpec((1, 1, tq, LANES), lambda b, r, i: (b, r, i, 0))
    return pl.pallas_call(
        functools.partial(_attn_a_kernel, seq_len=length, tq=tq),
        grid=(batch, dilation, n_tiles),
        in_specs=[cur, prev, nxt],
        out_specs=[out, lse_spec],
        out_shape=[jax.ShapeDtypeStruct((batch, dilation, length, OUT_A), BF16),
                   jax.ShapeDtypeStruct((batch, dilation, length, LANES), F32)],
        scratch_shapes=[pltpu.VMEM((tq + 2 * side, OUT_A), BF16),
                        pltpu.VMEM((tq + 2 * side, 2 * OUT_A), BF16),
                        pltpu.VMEM((n_blk, 2 * ATT_A_QB, kb), F32),
                        pltpu.VMEM((n_blk, 2 * ATT_A_QB, kb), BF16),
                        pltpu.VMEM((n_blk, 2 * ATT_A_QB, LANES), F32),
                        pltpu.VMEM((tq, LANES), F32)],
        compiler_params=_params("arbitrary", "arbitrary", "arbitrary"),
        name=f"attn_a_d{dilation}",
    )(kvq, kvq, kvq)


def _merge_a_kernel(o0_ref, o1_ref, o2_ref, l0_ref, l1_ref, l2_ref, e_ref, y_ref, ot_ref, lt_ref):
    for slot, (o_ref, l_ref, dilation) in enumerate(
            ((o1_ref, l1_ref, DILATED_GROUPS[1][1]), (o2_ref, l2_ref, DILATED_GROUPS[2][1]))):
        rows = TOKEN_TILE // dilation
        for r in range(dilation):
            lt_ref[slot, pl.ds(r, rows, stride=dilation), :] = l_ref[0, r]
            for p in range(PAIRS_PER_CHUNK):
                cols = slice(p * LANES, (p + 1) * LANES)
                ot_ref[slot, p, pl.ds(r, rows, stride=dilation), :] = o_ref[0, r, :, cols].astype(F32)
    l0, l1, l2 = l0_ref[0, 0], lt_ref[0], lt_ref[1]
    m = jnp.maximum(jnp.maximum(l0, l1), l2)
    e0, e1, e2 = jnp.exp2(l0 - m), jnp.exp2(l1 - m), jnp.exp2(l2 - m)
    inv = 1.0 / (e0 + e1 + e2)

    def spread(w):
        hi = w.astype(BF16)
        lo = (w - hi.astype(F32)).astype(BF16)
        return (jnp.dot(hi, e_ref[...], preferred_element_type=F32)
                + jnp.dot(lo, e_ref[...], preferred_element_type=F32))

    w0, w1, w2 = spread(e0 * inv), spread(e1 * inv), spread(e2 * inv)
    for p in range(PAIRS_PER_CHUNK):
        cols = slice(p * LANES, (p + 1) * LANES)
        y_ref[:, cols] = (w0[:, cols] * o0_ref[0, 0, :, cols].astype(F32) + w1[:, cols] * ot_ref[0, p]
                          + w2[:, cols] * ot_ref[1, p]).astype(BF16)


def _merge_a(outs, lses, batch, seq_len):
    t = batch * seq_len
    tps = seq_len // TOKEN_TILE
    grp = [pl.BlockSpec((1, d, TOKEN_TILE // d, OUT_A), lambda i: (i // tps, 0, i % tps, 0))
           for _, d in DILATED_GROUPS]
    grp_lse = [pl.BlockSpec((1, d, TOKEN_TILE // d, LANES), lambda i: (i // tps, 0, i % tps, 0))
               for _, d in DILATED_GROUPS]
    spread = jnp.asarray(np.arange(LANES)[:, None] == LSE_REP * (np.arange(OUT_A)[None, :] // HEAD_DIM), BF16)
    return pl.pallas_call(
        _merge_a_kernel,
        grid=(t // TOKEN_TILE,),
        in_specs=grp + grp_lse + [pl.BlockSpec((LANES, OUT_A), lambda i: (0, 0))],
        out_specs=pl.BlockSpec((TOKEN_TILE, OUT_A), lambda i: (i, 0)),
        out_shape=jax.ShapeDtypeStruct((t, OUT_A), BF16),
        scratch_shapes=[pltpu.VMEM((N_GROUPS_A - 1, PAIRS_PER_CHUNK, TOKEN_TILE, LANES), F32),
                        pltpu.VMEM((N_GROUPS_A - 1, TOKEN_TILE, LANES), F32)],
        compiler_params=_params("parallel"),
        name="merge_a",
    )(*outs, *lses, spread)


def _attn_b_kernel(cur_ref, prev_ref, next_ref, bias_ref, o_ref, kw_ref, vw_ref, s_ref, p_ref):
    w, halo, rows = GRID_W, ATT_B_HALO, ATT_B_ROWS
    n_sub = rows // ATT_B_SUB
    sub_q = ATT_B_SUB * w
    win = ATT_B_WIN * w
    ones = jnp.ones(((rows + 2 * halo) * w, LANES), BF16)
    q_slot, k_slot, v_slot = (off // LANES for off in KVQ_OFFSET)
    for u in range(ATT_B_PAIRS):
        kw_ref[u, 0:halo * w] = prev_ref[0, 0, k_slot + u]
        kw_ref[u, halo * w:(halo + rows) * w] = cur_ref[0, 0, k_slot + u]
        kw_ref[u, (halo + rows) * w:] = next_ref[0, 0, k_slot + u]
        vw_ref[u, 0:halo * w, :LANES] = prev_ref[0, 0, v_slot + u]
        vw_ref[u, halo * w:(halo + rows) * w, :LANES] = cur_ref[0, 0, v_slot + u]
        vw_ref[u, (halo + rows) * w:, :LANES] = next_ref[0, 0, v_slot + u]

    @pl.when((pl.program_id(0) == 0) & (pl.program_id(1) == 0) & (pl.program_id(2) == 0))
    def _():
        for u in range(ATT_B_PAIRS):
            vw_ref[u, :, LANES:] = ones

    i = pl.program_id(2)
    lo_min = jnp.where(i == 0, halo, 0)
    lo_max = jnp.where(i == pl.num_programs(2) - 1, halo, rows)
    starts = [pl.multiple_of(jnp.clip(ATT_B_SUB * sb, lo_min, lo_max) * w, LANES) for sb in range(n_sub)]
    head_a = lax.broadcasted_iota(jnp.int32, (rows * w, LANES), 1) < HEAD_DIM
    out_a = lax.broadcasted_iota(jnp.int32, (sub_q, LANES), 1) < HEAD_DIM
    zero = jnp.zeros((rows * w, LANES), BF16)
    for u in range(ATT_B_PAIRS):
        q2 = cur_ref[0, 0, q_slot + u]
        qa = jnp.where(head_a, q2, zero)
        qb = jnp.where(head_a, zero, q2)
        for sb in range(n_sub):
            rs = slice(sb * sub_q, (sb + 1) * sub_q)
            lhs = jnp.concatenate([qa[rs], qb[rs]], axis=0)
            s_ref[u, sb] = lax.dot_general(lhs, kw_ref[u, pl.ds(starts[sb], win), :], NT_DIMS,
                                           preferred_element_type=F32)
    for u in range(ATT_B_PAIRS):
        s = s_ref[u] + jnp.concatenate([bias_ref[2 * u, 0], bias_ref[2 * u + 1, 0]], axis=1)
        m = jnp.max(s, axis=-1, keepdims=True)
        p_ref[u] = jnp.exp2(s - m).astype(BF16)
    for u in range(ATT_B_PAIRS):
        for sb in range(n_sub):
            pv = jnp.dot(p_ref[u, sb], vw_ref[u, pl.ds(starts[sb], win), :], preferred_element_type=F32)
            o = pv[:, :LANES] / pv[:, LANES:]
            o_ref[0, sb * sub_q:(sb + 1) * sub_q, u * LANES:(u + 1) * LANES] = (
                jnp.where(out_a, o[:sub_q], o[sub_q:]).astype(BF16))


def _attn_b(qkv, bias, batch, seq_len):
    n_rows = seq_len // GRID_W
    tile = ATT_B_ROWS * GRID_W
    halo = ATT_B_HALO * GRID_W
    n_tiles = n_rows // ATT_B_ROWS
    assert n_tiles >= 2
    per_tile = tile // halo
    halo_blocks = seq_len // halo
    n_steps = HEADS_B // 2 // ATT_B_PAIRS
    n_sub = ATT_B_ROWS // ATT_B_SUB
    sub_q = ATT_B_SUB * GRID_W
    win = ATT_B_WIN * GRID_W

    n_slots = 3 * ATT_B_PAIRS
    cur = pl.BlockSpec((1, 1, n_slots, tile, LANES), lambda hp, b, i: (b, hp, 0, i, 0))
    prev = pl.BlockSpec((1, 1, 2 * ATT_B_PAIRS, halo, LANES),
                        lambda hp, b, i: (b, hp, 0, jnp.maximum(i * per_tile - 1, 0), 0))
    nxt = pl.BlockSpec((1, 1, 2 * ATT_B_PAIRS, halo, LANES),
                       lambda hp, b, i: (b, hp, 0, jnp.minimum((i + 1) * per_tile, halo_blocks - 1), 0))

    def tile_kind(i):
        return jnp.where(i == 0, 0, jnp.where(i == n_tiles - 1, 2, 1))

    return pl.pallas_call(
        _attn_b_kernel,
        grid=(n_steps, batch, n_tiles),
        in_specs=[cur, prev, nxt,
                  pl.BlockSpec((2 * ATT_B_PAIRS, 1, n_sub, sub_q, win),
                               lambda hp, b, i: (hp, tile_kind(i), 0, 0, 0))],
        out_specs=pl.BlockSpec((1, tile, ATT_B_PAIRS * LANES), lambda hp, b, i: (b, i, hp)),
        out_shape=jax.ShapeDtypeStruct((batch, seq_len, OUT_B), BF16),
        scratch_shapes=[pltpu.VMEM((ATT_B_PAIRS, tile + 2 * halo, LANES), BF16),
                        pltpu.VMEM((ATT_B_PAIRS, tile + 2 * halo, 2 * LANES), BF16),
                        pltpu.VMEM((ATT_B_PAIRS, n_sub, 2 * sub_q, win), F32),
                        pltpu.VMEM((ATT_B_PAIRS, n_sub, 2 * sub_q, win), BF16)],
        compiler_params=_params("arbitrary", "arbitrary", "arbitrary"),
        name="attn_b",
    )(qkv, qkv, qkv, bias)


def _rope_layout():
    n = np.arange(QKV_CHUNK)
    pair, lane = n // LANES, n % LANES
    second, hb, i = lane // HEAD_DIM, (lane % HEAD_DIM) // (HEAD_DIM // 2), lane % (HEAD_DIM // 2)
    head = 2 * pair + hb
    return head * HEAD_DIM + second * (HEAD_DIM // 2) + i, head


def _segment_mean_matrix(head_of_column):
    same = head_of_column[:, None] == head_of_column[None, :]
    return jnp.asarray(np.where(same, 1.0 / HEAD_DIM, 0.0), BF16)


def _rope_tables(seq_len):
    half = HEAD_DIM // 2
    inv_freq = np.float32(ROPE_THETA) ** (-np.arange(half, dtype=np.float32) / np.float32(half))
    ang = (np.arange(seq_len, dtype=np.float32)[:, None] * inv_freq[None, :]).astype(np.float64)
    cos, sin = jnp.asarray(np.cos(ang), F32), jnp.asarray(np.sin(ang), F32)
    return (jnp.concatenate([cos, cos, cos, cos], axis=-1),
            jnp.concatenate([-sin, -sin, sin, sin], axis=-1))


def _prep_a(w_qkv, q_gain, k_gain):
    src, _ = _rope_layout()
    n_qk = 2 * N_GROUPS_A
    cols = np.concatenate([c * QKV_CHUNK + src for c in range(n_qk)]
                          + [np.arange(n_qk * QKV_CHUNK, QKV_A)])
    half = HEAD_DIM // 2

    def pair_lanes(gain):
        return jnp.concatenate([gain[:half], gain[:half], gain[half:], gain[half:]])

    n_pairs = N_GROUPS_A * PAIRS_PER_CHUNK
    gains = jnp.concatenate([jnp.tile(pair_lanes(q_gain) * Q_SCALE, n_pairs),
                             jnp.tile(pair_lanes(k_gain), n_pairs),
                             jnp.ones((QKV_A - n_qk * QKV_CHUNK,), F32)])
    return w_qkv[:, cols].astype(BF16), gains.reshape(1, QKV_A)


def _prep_b(q_gain, k_gain):
    gains = jnp.concatenate([jnp.tile(q_gain * Q_SCALE, HEADS_B),
                             jnp.tile(k_gain, HEADS_B),
                             jnp.ones((OUT_B,), F32)])
    return gains.reshape(1, QKV_B)


def _bias_tile_plan():
    halo, rows, n_sub = ATT_B_HALO, ATT_B_ROWS, ATT_B_ROWS // ATT_B_SUB
    plan = {}
    for kind, (lo_min, lo_max) in enumerate(((halo, rows), (0, rows), (0, halo))):
        for sb in range(n_sub):
            ws = min(max(ATT_B_SUB * sb, lo_min), lo_max)
            for ql in range(ATT_B_SUB):
                rho = ATT_B_SUB * sb + ql
                lo = min(max(rho, lo_min), lo_max)
                for jj in range(ATT_B_WIN // 2):
                    jl = ws + 2 * jj
                    dr = jl - rho + NA_ROWS // 2 - 1
                    plan[kind, sb, ql, jj] = (dr, lo <= jl < lo + NA_ROWS, lo <= jl + 1 < lo + NA_ROWS)
    return plan


def _bias_expand_kernel(c_ref, o_ref):
    w = GRID_W
    left = lax.broadcasted_iota(jnp.int32, (w, 2 * w), 1) < w
    neg = jnp.full((w, 2 * w), NEG_INF, F32)
    for (kind, sb, ql, jj), (dr, ok_a, ok_b) in _bias_tile_plan().items():
        if ok_a or ok_b:
            tile = c_ref[0, dr + 1]
            if not ok_b:
                tile = jnp.where(left, tile, neg)
            elif not ok_a:
                tile = jnp.where(left, neg, tile)
        else:
            tile = neg
        o_ref[0, kind, sb, ql * w:(ql + 1) * w, 2 * jj * w:2 * (jj + 1) * w] = tile


def _bias_table_b(rpb):
    w, n_sub, n_dr = GRID_W, ATT_B_ROWS // ATT_B_SUB, 2 * NA_ROWS - 1
    col = np.arange(w)
    col_start = np.clip(col - NA_COLS // 2, 0, w - NA_COLS)
    col_ok = (col[None, :] >= col_start[:, None]) & (col[None, :] < col_start[:, None] + NA_COLS)
    dc = np.clip(col[None, :] - col[:, None] + NA_COLS - 1, 0, 2 * NA_COLS - 2)
    pick = jnp.asarray(np.arange(2 * NA_COLS - 1)[:, None, None] == dc[None], F32)
    by_row = jnp.einsum("hrc,cqk->hrqk", rpb * LOG2E, pick, precision=lax.Precision.HIGHEST)
    by_row = jnp.where(col_ok[None, None], by_row, NEG_INF)
    pad = jnp.full((HEADS_B, 1, w, w), NEG_INF, F32)
    by_row = jnp.concatenate([pad, by_row, pad], axis=1)
    pairs = jnp.concatenate([by_row[:, :-1], by_row[:, 1:]], axis=-1)
    return pl.pallas_call(
        _bias_expand_kernel,
        grid=(HEADS_B,),
        in_specs=[pl.BlockSpec((1, n_dr + 1, w, 2 * w), lambda h: (h, 0, 0, 0))],
        out_specs=pl.BlockSpec((1, 3, n_sub, ATT_B_SUB * w, ATT_B_WIN * w), lambda h: (h, 0, 0, 0, 0)),
        out_shape=jax.ShapeDtypeStruct((HEADS_B, 3, n_sub, ATT_B_SUB * w, ATT_B_WIN * w), F32),
        compiler_params=_params("parallel"),
        name="bias_expand",
    )(pairs)


def _mixer_a(x, batch, seq_len, mix_g, w_qkv, gains, seg, rope_tables):
    qkv = _qkv_a(x, mix_g, w_qkv, gains, seg, rope_tables, batch, seq_len)
    outs, lses = [], []
    for group, (window, dilation) in enumerate(DILATED_GROUPS):
        assert window // (2 * dilation) == ATT_A_SIDE
        o, lse = _attn_a_group(qkv[group], dilation, batch, seq_len)
        outs.append(o)
        lses.append(lse)
    return _merge_a(outs, lses, batch, seq_len)


def _mixer_b(x, batch, seq_len, mix_g, w_qkv, gains, bias, seg):
    qkv = _qkv_b(x, mix_g, w_qkv, gains, seg, batch, seq_len)
    return _attn_b(qkv, bias, batch, seq_len).reshape(batch * seq_len, OUT_B)


def kernel(x_prompt, x_sample, ffn1_norm, ffn1_w_in, ffn1_w_out, mix_norm, ffn2_norm, ffn2_w_in, ffn2_w_out, a_w_qkv, a_q_norm, a_k_norm, a_w_o, b_w_qkv, b_q_norm, b_k_norm, b_rpb, b_w_o):
    depth = ffn1_norm.shape[0]
    bf = lambda w: w.astype(BF16)
    ffn1_w_in, ffn1_w_out, ffn2_w_in, ffn2_w_out = map(bf, (ffn1_w_in, ffn1_w_out, ffn2_w_in, ffn2_w_out))
    a_w_o, b_w_qkv, b_w_o = map(bf, (a_w_o, b_w_qkv, b_w_o))
    seg_a = _segment_mean_matrix(_rope_layout()[1][:SEG_COLS])
    seg_b = _segment_mean_matrix(np.arange(SEG_COLS) // HEAD_DIM)
    prep_a = [_prep_a(a_w_qkv[j], a_q_norm[j], a_k_norm[j]) for j in range(a_w_qkv.shape[0])]
    prep_b = [(_prep_b(b_q_norm[j], b_k_norm[j]), _bias_table_b(b_rpb[j])) for j in range(b_w_qkv.shape[0])]
    rope_tables = _rope_tables(max(x_prompt.shape[1], x_sample.shape[1]))

    def trunk(x3):
        batch, seq_len, _ = x3.shape
        x = x3.reshape(batch * seq_len, D_MODEL)
        for i in range(depth):
            j = i // 2
            x = _ffn(x, ffn1_norm[i], ffn1_w_in, ffn1_w_out, i)
            if i % 2 == 0:
                w_qkv, gains = prep_a[j]
                attn = (_mixer_a(x, batch, seq_len, mix_norm[i], w_qkv, gains, seg_a, rope_tables), a_w_o[j])
            else:
                gains, bias = prep_b[j]
                attn = (_mixer_b(x, batch, seq_len, mix_norm[i], b_w_qkv[j], gains, bias, seg_b), b_w_o[j])
            x = _ffn(x, ffn2_norm[i], ffn2_w_in, ffn2_w_out, i, attn)
        return x.reshape(batch, seq_len, D_MODEL)

    return (trunk(x_prompt), trunk(x_sample))
```
